```python
import math
import jax
import jax.numpy as jnp
from jax import lax
import numpy as np

D_MODEL = 1024
BATCH = 2
SEQ = 16384
DEPTH = 2

MEM_TOKENS = 256
MIX_W = D_MODEL // 2
N_BRANCH = 3
NSA_HEADS = 8
NSA_HD = MIX_W // NSA_HEADS
NSA_GROUPS = 2
NSA_HPG = NSA_HEADS // NSA_GROUPS
CMP_LEN = 32
CMP_STRIDE = 16
CMP_HID = D_MODEL // 4
SEL_BLK = 64
SEL_TOPK = 16
WIN = 512
Q_BLK = 128
GDN_HEADS = 4
GDN_HD = MIX_W // GDN_HEADS
CONV_K = 4
CHUNK = 64
GLA_HEADS = 4
GLA_DK = MIX_W // (2 * GLA_HEADS)
GLA_DV = MIX_W // GLA_HEADS
GLA_RANK = 16
GLA_TAU = 16.0
XA_HEADS = 4
XA_HD = D_MODEL // XA_HEADS
D_FF = 4 * D_MODEL
ALPHA = (2.0 * DEPTH) ** 0.25
BETA = (8.0 * DEPTH) ** -0.25
LN_EPS = 1e-5
NORM_EPS = 1e-6
NEG_INF = -1e30
FORCE_BONUS = 1e4

IN_SIZES = (
    NSA_HEADS * NSA_HD,
    6 * NSA_GROUPS * NSA_HD,
    NSA_HEADS * 3,
    3 * MIX_W,
    GDN_HEADS,
    GDN_HEADS,
    MIX_W,
    GLA_HEADS * GLA_DK,
    GLA_HEADS * GLA_DK,
    GLA_HEADS * GLA_DV,
    GLA_RANK,
    MIX_W,
    N_BRANCH * D_MODEL,
)
IN_SPLITS = tuple(sum(IN_SIZES[: i + 1]) for i in range(len(IN_SIZES) - 1))
IN_WIDTH = sum(IN_SIZES)

kernel_name = 'hybrid_nsa_gdn_gla_deepnorm'


def _layer_norm(x, g, b):
    xf = x.astype(jnp.float32)
    mu = jnp.mean(xf, -1, keepdims=True)
    var = jnp.mean(jnp.square(xf - mu), -1, keepdims=True)
    return ((xf - mu) * lax.rsqrt(var + LN_EPS) * g + b).astype(x.dtype)


def _rms_norm(x, g):
    xf = x.astype(jnp.float32)
    return (xf * lax.rsqrt(jnp.mean(jnp.square(xf), -1, keepdims=True) + NORM_EPS) * g).astype(x.dtype)


def _l2norm(x):
    xf = x.astype(jnp.float32)
    return (xf * lax.rsqrt(jnp.sum(jnp.square(xf), -1, keepdims=True) + NORM_EPS)).astype(x.dtype)


def _masked_softmax(s, mask):
    s = jnp.where(mask, s, NEG_INF)
    m = jnp.max(s, -1, keepdims=True)
    e = jnp.where(mask, jnp.exp(s - m), 0.0)
    return e / jnp.maximum(jnp.sum(e, -1, keepdims=True), 1e-30)


def _alibi_slopes(n):
    return jnp.asarray([2.0 ** (-8.0 * (i + 1) / n) for i in range(n)], jnp.float32)


def _causal_conv(x, w):
    c = x.shape[-1]
    return lax.conv_general_dilated(
        x, w[:, None, :].astype(x.dtype), window_strides=(1,), padding=[(w.shape[0] - 1, 0)],
        dimension_numbers=('NWC', 'WIO', 'NWC'), feature_group_count=c)


def _compress(kv, pe, w1, b1, w2):
    b, t, g, hd = kv.shape
    r = CMP_LEN // CMP_STRIDE
    n_chunk = t // CMP_STRIDE
    n_cmp = n_chunk - r + 1
    ch = kv.reshape(b, n_chunk, CMP_STRIDE, g, hd)
    blocks = jnp.concatenate([ch[:, j:j + n_cmp] for j in range(r)], axis=2)
    blocks = blocks + pe[:, None, :]
    flat = blocks.transpose(0, 1, 3, 2, 4).reshape(b, n_cmp, g, CMP_LEN * hd)
    return jax.nn.gelu(flat @ w1 + b1) @ w2


def _nsa(q, kc, vc, ks, vs, kw, vw, gates):
    b, t = q.shape[:2]
    g, hd = NSA_GROUPS, NSA_HD
    n_cmp = kc.shape[1]
    n_sel = t // SEL_BLK
    n_top = min(SEL_TOPK, n_sel)
    scale = hd ** -0.5
    slopes = _alibi_slopes(NSA_HEADS).reshape(g, NSA_HPG)
    sl5 = slopes[None, :, :, None, None]
    c_start = jnp.arange(n_cmp) * CMP_STRIDE
    c_end = c_start + CMP_LEN - 1
    s_start = jnp.arange(n_sel) * SEL_BLK
    overlap = ((c_start[:, None] < s_start[None, :] + SEL_BLK)
               & (c_start[:, None] + CMP_LEN > s_start[None, :])).astype(jnp.float32)
    ks_blk = ks.reshape(b, n_sel, SEL_BLK, g, hd).transpose(0, 3, 1, 2, 4)
    vs_blk = vs.reshape(b, n_sel, SEL_BLK, g, hd).transpose(0, 3, 1, 2, 4)
    kw_pad = jnp.pad(kw, ((0, 0), (WIN, 0), (0, 0), (0, 0)))
    vw_pad = jnp.pad(vw, ((0, 0), (WIN, 0), (0, 0), (0, 0)))
    bi = jnp.arange(b)[:, None, None, None]
    gi = jnp.arange(g)[None, :, None, None]
    blk_ids = jnp.arange(n_sel)

    def block(i):
        qs = i * Q_BLK
        tq = qs + jnp.arange(Q_BLK)
        qb = lax.dynamic_slice_in_dim(q, qs, Q_BLK, 1)
        gb = lax.dynamic_slice_in_dim(gates, qs, Q_BLK, 1)
        dist_c = tq[:, None] - c_end[None, :]
        s = jnp.einsum('bqghd,bcgd->bghqc', qb, kc).astype(jnp.float32) * scale - sl5 * dist_c
        p_cmp = _masked_softmax(s, dist_c >= 0)
        o_cmp = jnp.einsum('bghqc,bcgd->bqghd', p_cmp.astype(vc.dtype), vc)
        imp = jnp.einsum('bghqc,cs->bgqs', p_cmp, overlap)
        cur = tq // SEL_BLK
        forced = ((blk_ids[None, :] == 0) | (blk_ids[None, :] == cur[:, None])
                  | (blk_ids[None, :] == cur[:, None] - 1))
        valid = s_start[None, :] <= tq[:, None]
        score = jnp.where(valid, imp + jnp.where(forced, FORCE_BONUS, 0.0), NEG_INF)
        _, idx = lax.top_k(score, n_top)
        kg = ks_blk[bi, gi, idx]
        vg = vs_blk[bi, gi, idx]
        pos = idx[..., None] * SEL_BLK + jnp.arange(SEL_BLK)
        dist_s = (tq[:, None, None] - pos)[:, :, None]
        s = (jnp.einsum('bqghd,bgqnkd->bghqnk', qb, kg).astype(jnp.float32) * scale
             - slopes[None, :, :, None, None, None] * dist_s)
        p = _masked_softmax(s.reshape(b, g, NSA_HPG, Q_BLK, n_top * SEL_BLK),
                            (dist_s >= 0).reshape(b, g, 1, Q_BLK, n_top * SEL_BLK))
        o_slc = jnp.einsum('bghqn,bgqnd->bqghd', p.astype(vg.dtype),
                           vg.reshape(b, g, Q_BLK, n_top * SEL_BLK, hd))
        kwb = lax.dynamic_slice_in_dim(kw_pad, qs, WIN + Q_BLK, 1)
        vwb = lax.dynamic_slice_in_dim(vw_pad, qs, WIN + Q_BLK, 1)
        pos_w = qs - WIN + jnp.arange(WIN + Q_BLK)
        dist_w = tq[:, None] - pos_w[None, :]
        mask_w = (dist_w >= 0) & (dist_w < WIN) & (pos_w[None, :] >= 0)
        s = jnp.einsum('bqghd,bkgd->bghqk', qb, kwb).astype(jnp.float32) * scale - sl5 * dist_w
        p = _masked_softmax(s, mask_w)
        o_win = jnp.einsum('bghqk,bkgd->bqghd', p.astype(vwb.dtype), vwb)
        o = gb[..., 0:1] * o_cmp + gb[..., 1:2] * o_slc + gb[..., 2:3] * o_win
        return o.reshape(b, Q_BLK, g * NSA_HPG * hd)

    out = lax.map(block, jnp.arange(t // Q_BLK))
    return out.transpose(1, 0, 2, 3).reshape(b, t, -1)


def _gated_delta_net(q, k, v, g, beta):
    dtype = v.dtype
    b, t, h, dk = q.shape
    dv = v.shape[-1]
    n = t // CHUNK

    def to_chunks(a):
        return a.astype(jnp.float32).reshape(b, n, CHUNK, h, a.shape[-1]).transpose(0, 3, 1, 2, 4)

    qc, kc, vc = to_chunks(q * dk ** -0.5), to_chunks(k), to_chunks(v)
    gc = jnp.cumsum(g.astype(jnp.float32).reshape(b, n, CHUNK, h).transpose(0, 3, 1, 2), axis=-1)
    bc = beta.astype(jnp.float32).reshape(b, n, CHUNK, h).transpose(0, 3, 1, 2)
    tri = jnp.tril(jnp.ones((CHUNK, CHUNK), bool))
    strict = jnp.tril(jnp.ones((CHUNK, CHUNK), bool), -1)
    diff = gc[..., :, None] - gc[..., None, :]
    decay = jnp.where(tri, jnp.exp(jnp.where(tri, diff, 0.0)), 0.0)
    kb = kc * bc[..., None]
    a_mat = jnp.where(strict, jnp.einsum('bhnid,bhnjd->bhnij', kb, kc) * decay, 0.0)
    rhs = jnp.concatenate([vc * bc[..., None], kb * jnp.exp(gc)[..., None]], axis=-1)
    sol = lax.linalg.triangular_solve(jnp.eye(CHUNK, dtype=jnp.float32) + a_mat, rhs,
                                      left_side=True, lower=True, unit_diagonal=True)
    u, w = sol[..., :dv], sol[..., dv:]
    qk = jnp.einsum('bhnid,bhnjd->bhnij', qc, kc) * decay
    q_dec = qc * jnp.exp(gc)[..., None]
    g_last = gc[..., -1:]
    k_tail = kc * jnp.exp(g_last - gc)[..., None]
    a_last = jnp.exp(g_last[..., 0])

    def step(s_state, inp):
        q_i, k_i, u_i, w_i, qk_i, a_i = inp
        v_new = u_i - jnp.einsum('bhcd,bhde->bhce', w_i, s_state)
        o = jnp.einsum('bhcd,bhde->bhce', q_i, s_state) + jnp.einsum('bhij,bhje->bhie', qk_i, v_new)
        s_state = s_state * a_i[..., None, None] + jnp.einsum('bhcd,bhce->bhde', k_i, v_new)
        return s_state, o

    xs = tuple(jnp.moveaxis(a, 2, 0) for a in (q_dec, k_tail, u, w, qk, a_last))
    _, o = lax.scan(step, jnp.zeros((b, h, dk, dv), jnp.float32), xs)
    return o.transpose(1, 0, 3, 2, 4).reshape(b, t, h, dv).astype(dtype)


def _gla(q, k, v, log_a):
    dtype = v.dtype
    b, t, h, dk = q.shape
    dv = v.shape[-1]
    n = t // CHUNK

    def to_chunks(a):
        return a.astype(jnp.float32).reshape(b, n, CHUNK, h, a.shape[-1]).transpose(1, 0, 3, 2, 4)

    xs = (to_chunks(q * dk ** -0.5), to_chunks(k), to_chunks(v), to_chunks(log_a))
    tri = jnp.tril(jnp.ones((CHUNK, CHUNK), bool))[:, :, None]

    def step(s_state, inp):
        q_i, k_i, v_i, a_i = inp
        cum = jnp.cumsum(a_i, axis=2)
        diff = cum[:, :, :, None, :] - cum[:, :, None, :, :]
        dec = jnp.where(tri, jnp.exp(jnp.where(tri, diff, 0.0)), 0.0)
        scores = jnp.einsum('bhid,bhjd,bhijd->bhij', q_i, k_i, dec)
        o = (jnp.einsum('bhid,bhde->bhie', q_i * jnp.exp(cum), s_state)
             + jnp.einsum('bhij,bhje->bhie', scores, v_i))
        c_last = cum[:, :, -1:, :]
        s_state = (s_state * jnp.exp(c_last[:, :, 0, :, None])
                   + jnp.einsum('bhcd,bhce->bhde', k_i * jnp.exp(c_last - cum), v_i))
        return s_state, o

    _, o = lax.scan(step, jnp.zeros((b, h, dk, dv), jnp.float32), xs)
    return o.transpose(1, 0, 3, 2, 4).reshape(b, t, h, dv).astype(dtype)


def _hybrid_mixer(u, w_in, cmp_pe, cmp_w1, cmp_b1, cmp_w2, gdn_conv, gdn_a_log, gdn_dt_bias,
                  gdn_norm, gla_w_lr, gla_b_lr, gla_norm, w_up, w_out):
    b, t, _ = u.shape
    (nsa_q, nsa_kv, nsa_g, gdn_qkv, gdn_a, gdn_b, gdn_z,
     gla_q, gla_k, gla_v, gla_lr, gla_r, merge_g) = jnp.split(u @ w_in, IN_SPLITS, axis=-1)
    q = nsa_q.reshape(b, t, NSA_GROUPS, NSA_HPG, NSA_HD)
    kv = nsa_kv.reshape(b, t, 6, NSA_GROUPS, NSA_HD)
    kc = _compress(kv[:, :, 0], cmp_pe[0], cmp_w1[0], cmp_b1[0], cmp_w2[0])
    vc = _compress(kv[:, :, 1], cmp_pe[1], cmp_w1[1], cmp_b1[1], cmp_w2[1])
    nsa_gates = jax.nn.sigmoid(nsa_g).reshape(b, t, NSA_GROUPS, NSA_HPG, 3)
    y_nsa = _nsa(q, kc, vc, kv[:, :, 2], kv[:, :, 3], kv[:, :, 4], kv[:, :, 5], nsa_gates)
    qkv = jax.nn.silu(_causal_conv(gdn_qkv, gdn_conv))
    gq, gk, gv = jnp.split(qkv, 3, axis=-1)
    gq = _l2norm(gq.reshape(b, t, GDN_HEADS, GDN_HD))
    gk = _l2norm(gk.reshape(b, t, GDN_HEADS, GDN_HD))
    gv = gv.reshape(b, t, GDN_HEADS, GDN_HD)
    log_decay = -jnp.exp(gdn_a_log) * jax.nn.softplus(gdn_a + gdn_dt_bias)
    o = _gated_delta_net(gq, gk, gv, log_decay, jax.nn.sigmoid(gdn_b))
    y_gdn = (_rms_norm(o, gdn_norm) * jax.nn.silu(gdn_z.reshape(b, t, GDN_HEADS, GDN_HD))).reshape(b, t, MIX_W)
    log_a = jax.nn.log_sigmoid(gla_lr @ gla_w_lr + gla_b_lr) / GLA_TAU
    o = _gla(gla_q.reshape(b, t, GLA_HEADS, GLA_DK), gla_k.reshape(b, t, GLA_HEADS, GLA_DK),
             gla_v.reshape(b, t, GLA_HEADS, GLA_DV), log_a.reshape(b, t, GLA_HEADS, GLA_DK))
    y_gla = (_rms_norm(o, gla_norm) * jax.nn.silu(gla_r.reshape(b, t, GLA_HEADS, GLA_DV))).reshape(b, t, MIX_W)
    gates = jax.nn.sigmoid(merge_g).reshape(b, t, N_BRANCH, D_MODEL)
    merged = (gates[:, :, 0] * (y_nsa @ w_up[0]) + gates[:, :, 1] * (y_gdn @ w_up[1])
              + gates[:, :, 2] * (y_gla @ w_up[2]))
    return merged @ w_out


def _memory_xattn(x, mem, wq, wk, wv, wo):
    b, t, _ = x.shape
    m = mem.shape[1]
    q = (x @ wq).reshape(b, t, XA_HEADS, XA_HD)
    k = (mem @ wk).reshape(b, m, XA_HEADS, XA_HD)
    v = (mem @ wv).reshape(b, m, XA_HEADS, XA_HD)
    s = jnp.einsum('bthd,bmhd->bhtm', q, k).astype(jnp.float32) * XA_HD ** -0.5
    p = jax.nn.softmax(s, axis=-1).astype(v.dtype)
    return jnp.einsum('bhtm,bmhd->bthd', p, v).reshape(b, t, D_MODEL) @ wo


def _sq_relu_mlp(x, w1, w2):
    return jnp.square(jax.nn.relu(x @ w1)) @ w2


def setup_inputs(seed: int = 0) -> dict:
    key = jax.random.key(seed)
    ks = jax.random.split(key, 26)
    f32 = jnp.float32
    D = D_MODEL

    def nrm(k, shape, fan_in, gain=1.0):
        return jax.random.normal(k, shape, f32) * (gain * fan_in ** -0.5)

    dt = jnp.exp(jax.random.uniform(ks[9], (DEPTH, GDN_HEADS), f32, math.log(1e-3), math.log(1e-1)))
    return {
        'x': jax.random.normal(ks[0], (BATCH, SEQ, D), f32),
        'mem': jax.random.normal(ks[1], (BATCH, MEM_TOKENS, D), f32),
        'w_in': nrm(ks[2], (DEPTH, D, IN_WIDTH), D),
        'cmp_pe': 0.02 * jax.random.normal(ks[3], (DEPTH, 2, CMP_LEN, NSA_HD), f32),
        'cmp_w1': nrm(ks[4], (DEPTH, 2, CMP_LEN * NSA_HD, CMP_HID), CMP_LEN * NSA_HD),
        'cmp_b1': 0.01 * jax.random.normal(ks[5], (DEPTH, 2, CMP_HID), f32),
        'cmp_w2': nrm(ks[6], (DEPTH, 2, CMP_HID, NSA_HD), CMP_HID),
        'gdn_conv': nrm(ks[7], (DEPTH, CONV_K, 3 * MIX_W), CONV_K),
        'gdn_a_log': jnp.log(jax.random.uniform(ks[8], (DEPTH, GDN_HEADS), f32, 1.0, 16.0)),
        'gdn_dt_bias': dt + jnp.log(-jnp.expm1(-dt)),
        'gdn_norm': 1.0 + 0.01 * jax.random.normal(ks[10], (DEPTH, GDN_HD), f32),
        'gla_w_lr': nrm(ks[11], (DEPTH, GLA_RANK, GLA_HEADS * GLA_DK), GLA_RANK),
        'gla_b_lr': 0.1 * jax.random.normal(ks[12], (DEPTH, GLA_HEADS * GLA_DK), f32),
        'gla_norm': 1.0 + 0.01 * jax.random.normal(ks[13], (DEPTH, GLA_DV), f32),
        'w_up': nrm(ks[14], (DEPTH, N_BRANCH, MIX_W, D), MIX_W, BETA),
        'w_out': nrm(ks[15], (DEPTH, D, D), D, BETA),
        'xa_wq': nrm(ks[16], (DEPTH, D, D), D),
        'xa_wk': nrm(ks[17], (DEPTH, D, D), D),
        'xa_wv': nrm(ks[18], (DEPTH, D, D), D, BETA),
        'xa_wo': nrm(ks[19], (DEPTH, D, D), D, BETA),
        'mlp_w1': nrm(ks[20], (DEPTH, D, D_FF), D, BETA),
        'mlp_w2': nrm(ks[21], (DEPTH, D_FF, D), D_FF, BETA),
        'ln_g': 1.0 + 0.01 * jax.random.normal(ks[22], (DEPTH, 3, D), f32),
        'ln_b': 0.01 * jax.random.normal(ks[23], (DEPTH, 3, D), f32),
    }


def reference(x, mem, w_in, cmp_pe, cmp_w1, cmp_b1, cmp_w2, gdn_conv, gdn_a_log, gdn_dt_bias,
              gdn_norm, gla_w_lr, gla_b_lr, gla_norm, w_up, w_out, xa_wq, xa_wk, xa_wv, xa_wo,
              mlp_w1, mlp_w2, ln_g, ln_b):
    for l in range(DEPTH):
        h = _hybrid_mixer(x, w_in[l], cmp_pe[l], cmp_w1[l], cmp_b1[l], cmp_w2[l], gdn_conv[l],
                          gdn_a_log[l], gdn_dt_bias[l], gdn_norm[l], gla_w_lr[l], gla_b_lr[l],
                          gla_norm[l], w_up[l], w_out[l])
        x = _layer_norm(ALPHA * x + h, ln_g[l, 0], ln_b[l, 0])
        h = _memory_xattn(x, mem, xa_wq[l], xa_wk[l], xa_wv[l], xa_wo[l])
        x = _layer_norm(ALPHA * x + h, ln_g[l, 1], ln_b[l, 1])
        h = _sq_relu_mlp(x, mlp_w1[l], mlp_w2[l])
        x = _layer_norm(ALPHA * x + h, ln_g[l, 2], ln_b[l, 2])
    return x
```

```python
import functools

import numpy as np
import jax
import jax.numpy as jnp
from jax import lax
from jax.experimental import pallas as pl
from jax.experimental.pallas import tpu as pltpu

F32 = jnp.float32
BF16 = jnp.bfloat16
HIGHEST = lax.Precision.HIGHEST

D_MODEL = 1024
MIX_W = D_MODEL // 2
N_BRANCH = 3
NSA_HEADS = 8
NSA_HD = MIX_W // NSA_HEADS
NSA_GROUPS = 2
NSA_HPG = NSA_HEADS // NSA_GROUPS
CMP_LEN = 32
CMP_STRIDE = 16
CMP_HID = D_MODEL // 4
SEL_BLK = 64
SEL_TOPK = 16
WIN = 512
Q_BLK = 128
GDN_HEADS = 4
GDN_HD = MIX_W // GDN_HEADS
CONV_K = 4
CHUNK = 64
GLA_HEADS = 4
GLA_DK = MIX_W // (2 * GLA_HEADS)
GLA_DV = MIX_W // GLA_HEADS
GLA_RANK = 16
GLA_TAU = 16.0
XA_HEADS = 4
XA_HD = D_MODEL // XA_HEADS
D_FF = 4 * D_MODEL
LN_EPS = 1e-5
NORM_EPS = 1e-6
NEG_INF = -1e30
FORCE_BONUS = 1e4

LANE = 128
SUBLANE = 8
VMEM_LIMIT_BYTES = 48 * 1024 * 1024
SEL_TILE = 512
REMOVED = -3e38
GLA_LEVELS = (32, 16, 8, 4, 2, 1)
Q_BLK_LOG2 = Q_BLK.bit_length() - 1
SEL_BLK_LOG2 = SEL_BLK.bit_length() - 1
CHUNK_LOG2 = CHUNK.bit_length() - 1

_IN_SIZES = (NSA_HEADS * NSA_HD, 6 * NSA_GROUPS * NSA_HD, NSA_HEADS * 3, 3 * MIX_W, GDN_HEADS, GDN_HEADS,
             MIX_W, GLA_HEADS * GLA_DK, GLA_HEADS * GLA_DK, GLA_HEADS * GLA_DV, GLA_RANK, MIX_W,
             N_BRANCH * D_MODEL)
_OFF = tuple(int(v) for v in np.cumsum((0,) + _IN_SIZES))
(O_NSA_Q, O_NSA_KV, O_NSA_G, O_GDN_QKV, O_GDN_A, O_GDN_B, O_GDN_Z, O_GLA_Q, O_GLA_K, O_GLA_V, O_GLA_LR,
 O_GLA_R, O_MERGE, _) = _OFF


def _params(*sem):
    return pltpu.CompilerParams(dimension_semantics=sem, vmem_limit_bytes=VMEM_LIMIT_BYTES)


def _dot(a, b, precision=None):
    return jnp.dot(a, b, preferred_element_type=F32, precision=precision)


def _dot_nt(a, b):
    return lax.dot_general(a, b, (((1,), (1,)), ((), ())), preferred_element_type=F32)


def _dot_tn(a, b):
    return lax.dot_general(a, b, (((0,), (0,)), ((), ())), preferred_element_type=F32)


def _sigmoid(x):
    return 1.0 / (1.0 + jnp.exp(-x))


def _layer_norm(v, g, b):
    mu = jnp.mean(v, -1, keepdims=True)
    c = v - mu
    var = jnp.mean(c * c, -1, keepdims=True)
    return c * lax.rsqrt(var + LN_EPS) * g + b


def _mm_body(x_ref, w_ref, o_ref):
    o_ref[...] = _dot(x_ref[...].astype(BF16), w_ref[...]).astype(o_ref.dtype)


def _mm(x, w, out_dtype, tm=512):
    m, k = x.shape
    n = w.shape[1]
    tm = min(tm, m)
    return pl.pallas_call(
        _mm_body,
        grid=(m // tm,),
        in_specs=[pl.BlockSpec((tm, k), lambda i: (i, 0)),
                  pl.BlockSpec((k, n), lambda i: (0, 0))],
        out_specs=pl.BlockSpec((tm, n), lambda i: (i, 0)),
        out_shape=jax.ShapeDtypeStruct((m, n), out_dtype),
        compiler_params=_params("parallel"),
        name="proj_mm",
    )(x, w)


def _compress_body(ch_ref, w1_ref, b1_ref, pe_ref, w2_ref, o_ref):
    nc = ch_ref.shape[0]
    ch = ch_ref[...]
    top = _dot(ch, w1_ref[0])
    bot = _dot(ch, w1_ref[1])
    bot_next = pltpu.roll(bot, nc - 1, axis=0)
    pe_term = _dot(pe_ref[0], w1_ref[0]) + _dot(pe_ref[1], w1_ref[1])
    h = top + bot_next + pe_term[0:1] + b1_ref[...]
    h = jax.nn.gelu(h, approximate=True)
    o_ref[...] = _dot(h.astype(BF16), w2_ref[...]).astype(o_ref.dtype)


def _compress(ch, w1, b1, pe, w2, batch):
    nc = ch.shape[1] // batch
    half = CMP_STRIDE * NSA_HD
    return pl.pallas_call(
        _compress_body,
        grid=(2, NSA_GROUPS, batch),
        in_specs=[pl.BlockSpec((None, nc, half), lambda t, g, b: (t * NSA_GROUPS + g, b, 0)),
                  pl.BlockSpec((None, 2, half, CMP_HID), lambda t, g, b: (t, 0, 0, 0)),
                  pl.BlockSpec((None, 1, CMP_HID), lambda t, g, b: (t, 0, 0)),
                  pl.BlockSpec((None, 2, SUBLANE, half), lambda t, g, b: (t, 0, 0, 0)),
                  pl.BlockSpec((None, CMP_HID, NSA_HD), lambda t, g, b: (t, 0, 0))],
        out_specs=pl.BlockSpec((None, nc, NSA_HD), lambda t, g, b: (t * NSA_GROUPS + g, b, 0)),
        out_shape=jax.ShapeDtypeStruct((2 * NSA_GROUPS, batch * nc, NSA_HD), BF16),
        compiler_params=_params("parallel", "parallel", "parallel"),
        name="nsa_compress",
    )(ch, w1, b1, pe, w2)


def _masked_softmax_rows(s, mask):
    s = jnp.where(mask, s, NEG_INF)
    m = jnp.max(s, -1, keepdims=True)
    e = jnp.where(mask, jnp.exp(s - m), 0.0)
    return e / jnp.maximum(jnp.sum(e, -1, keepdims=True), 1e-30)


def _nsa_body(q_ref, kv_ref, kc_ref, vc_ref, gl_ref, ov_ref, o_ref, *, seq):
    g = pl.program_id(1)
    qb = pl.program_id(2)
    nc = seq // CMP_STRIDE
    ns = seq // SEL_BLK
    hq = NSA_HPG * Q_BLK
    qs = qb * Q_BLK
    scale = NSA_HD ** -0.5

    q4 = q_ref[...]
    qst = jnp.concatenate([q4[:, h * NSA_HD:(h + 1) * NSA_HD] for h in range(NSA_HPG)], axis=0)
    row = lax.broadcasted_iota(jnp.int32, (hq, 1), 0)
    hrow = row >> Q_BLK_LOG2
    slope = jnp.zeros((hq, 1), F32)
    for h in range(NSA_HPG):
        s_h = jnp.where(g == 0, 2.0 ** -(h + 1), 2.0 ** -(NSA_HPG + h + 1)).astype(F32)
        slope = jnp.where(hrow == h, s_h, slope)
    tq = qs + (row & (Q_BLK - 1))

    s = _dot_nt(qst, kc_ref[...]) * scale
    c_end = lax.broadcasted_iota(jnp.int32, (1, nc), 1) * CMP_STRIDE + (CMP_LEN - 1)
    dist = tq - c_end
    p_cmp = _masked_softmax_rows(s - slope * dist.astype(F32), dist >= 0)
    o_cmp = _dot(p_cmp.astype(BF16), vc_ref[...])

    p_sum = p_cmp[0:Q_BLK]
    for h in range(1, NSA_HPG):
        p_sum = p_sum + p_cmp[h * Q_BLK:(h + 1) * Q_BLK]
    p_hi = p_sum.astype(BF16)
    p_lo = (p_sum - p_hi.astype(F32)).astype(BF16)
    imp = _dot(p_hi, ov_ref[...]) + _dot(p_lo, ov_ref[...])
    blk = lax.broadcasted_iota(jnp.int32, (Q_BLK, ns), 1)
    tq1 = qs + lax.broadcasted_iota(jnp.int32, (Q_BLK, 1), 0)
    cur = tq1 >> SEL_BLK_LOG2
    forced = (blk == 0) | (blk == cur) | (blk == cur - 1)
    valid = blk * SEL_BLK <= tq1
    score = jnp.where(valid, imp + jnp.where(forced, FORCE_BONUS, 0.0), NEG_INF)

    def pick(_, carry):
        sc, sel = carry
        m = jnp.max(sc, -1, keepdims=True)
        idx = jnp.min(jnp.where(sc == m, blk, ns), -1, keepdims=True)
        hit = blk == idx
        return jnp.where(hit, REMOVED, sc), jnp.where(hit, 1.0, sel)

    _, sel = lax.fori_loop(0, min(SEL_TOPK, ns), pick, (score, jnp.zeros((Q_BLK, ns), F32)))
    sel_b = sel.astype(BF16)

    blk_row = lax.broadcasted_iota(jnp.int32, (ns, SEL_TILE), 0)
    blk_of_key = lax.broadcasted_iota(jnp.int32, (ns, SEL_TILE), 1) >> SEL_BLK_LOG2
    key_in_tile = lax.broadcasted_iota(jnp.int32, (1, SEL_TILE), 1)

    def sel_tile(kt, carry):
        m_i, l_i, acc = carry
        k0 = pl.multiple_of(kt * SEL_TILE, SEL_TILE)
        k = kv_ref[pl.ds(k0, SEL_TILE), 0:NSA_HD]
        v = kv_ref[pl.ds(k0, SEL_TILE), 2 * NSA_HD:3 * NSA_HD]
        d = tq - (k0 + key_in_tile)
        st = _dot_nt(qst, k) * scale - slope * d.astype(F32)
        expand = jnp.where(blk_row == blk_of_key + kt * (SEL_TILE // SEL_BLK), 1.0, 0.0).astype(BF16)
        selx = _dot(sel_b, expand)
        selx = jnp.concatenate([selx] * NSA_HPG, axis=0)
        mask = (d >= 0) & (selx > 0.5)
        st = jnp.where(mask, st, NEG_INF)
        m_new = jnp.maximum(m_i, jnp.max(st, -1, keepdims=True))
        p = jnp.where(mask, jnp.exp(st - m_new), 0.0)
        alpha = jnp.exp(m_i - m_new)
        l_new = alpha * l_i + jnp.sum(p, -1, keepdims=True)
        acc = alpha * acc + _dot(p.astype(BF16), v)
        return m_new, l_new, acc

    n_tiles = (qs + Q_BLK + SEL_TILE - 1) // SEL_TILE
    _, l_s, acc_s = lax.fori_loop(
        0, n_tiles, sel_tile,
        (jnp.full((hq, 1), NEG_INF, F32), jnp.zeros((hq, 1), F32), jnp.zeros((hq, NSA_HD), F32)))
    o_slc = acc_s / jnp.maximum(l_s, 1e-30)

    wlen = WIN + Q_BLK
    w0 = pl.multiple_of(jnp.maximum(qs - WIN, 0), Q_BLK)
    kw = kv_ref[pl.ds(w0, wlen), NSA_HD:2 * NSA_HD]
    vw = kv_ref[pl.ds(w0, wlen), 3 * NSA_HD:4 * NSA_HD]
    d = tq - (w0 + lax.broadcasted_iota(jnp.int32, (1, wlen), 1))
    s = _dot_nt(qst, kw) * scale - slope * d.astype(F32)
    p_win = _masked_softmax_rows(s, (d >= 0) & (d < WIN))
    o_win = _dot(p_win.astype(BF16), vw)

    gates = _sigmoid(gl_ref[...])
    outs = []
    for h in range(NSA_HPG):
        r = slice(h * Q_BLK, (h + 1) * Q_BLK)
        outs.append(gates[:, 3 * h:3 * h + 1] * o_cmp[r] + gates[:, 3 * h + 1:3 * h + 2] * o_slc[r]
                    + gates[:, 3 * h + 2:3 * h + 3] * o_win[r])
    o_ref[...] = jnp.concatenate(outs, axis=1).astype(o_ref.dtype)


def _nsa_attention(main, gate_logits, cmp_tok, overlap, batch, seq):
    nq = seq // Q_BLK
    nc = seq // CMP_STRIDE
    ns = seq // SEL_BLK
    gw = NSA_HPG * NSA_HD
    return pl.pallas_call(
        functools.partial(_nsa_body, seq=seq),
        grid=(batch, NSA_GROUPS, nq),
        in_specs=[pl.BlockSpec((Q_BLK, gw), lambda b, g, i: (b * nq + i, g)),
                  pl.BlockSpec((seq, gw), lambda b, g, i: (b, NSA_GROUPS + g)),
                  pl.BlockSpec((None, nc, NSA_HD), lambda b, g, i: (g, b, 0)),
                  pl.BlockSpec((None, nc, NSA_HD), lambda b, g, i: (NSA_GROUPS + g, b, 0)),
                  pl.BlockSpec((Q_BLK, LANE), lambda b, g, i: (b * nq + i, g)),
                  pl.BlockSpec((nc, ns), lambda b, g, i: (0, 0))],
        out_specs=pl.BlockSpec((Q_BLK, gw), lambda b, g, i: (b * nq + i, g)),
        out_shape=jax.ShapeDtypeStruct((batch * seq, MIX_W), BF16),
        compiler_params=_params("parallel", "parallel", "arbitrary"),
        name="nsa_attention",
    )(main, main, cmp_tok, cmp_tok, gate_logits, overlap)


def _unit_lower_inverse(a):
    c = a.shape[0]
    eye = (lax.broadcasted_iota(jnp.int32, (c, c), 0) == lax.broadcasted_iota(jnp.int32, (c, c), 1)).astype(F32)
    x = eye - a
    p = a
    n = 1
    while 2 * n < c:
        p = _dot(p, p, HIGHEST)
        x = x + _dot(x, p, HIGHEST)
        n *= 2
    return x


def _gdn_body(raw_ref, ab_ref, conv_ref, alog_ref, dtb_ref, o_ref, s_ref, tail_ref, buf_ref, *, tb):
    @pl.when(pl.program_id(1) == 0)
    def _():
        s_ref[...] = jnp.zeros_like(s_ref)
        tail_ref[...] = jnp.zeros_like(tail_ref)

    raw = raw_ref[...]
    buf_ref[0:SUBLANE] = tail_ref[...]
    buf_ref[SUBLANE:SUBLANE + tb] = raw
    tail_ref[...] = raw[tb - SUBLANE:tb]
    x = jnp.zeros_like(raw)
    for k in range(CONV_K):
        x = x + buf_ref[pl.ds(SUBLANE - (CONV_K - 1) + k, tb)] * conv_ref[k:k + 1]
    x = x * _sigmoid(x)

    ab = ab_ref[...]
    zed = ab + dtb_ref[...]
    softplus = jnp.maximum(zed, 0.0) + jnp.log1p(jnp.exp(-jnp.abs(zed)))
    g_all = -jnp.exp(alog_ref[...]) * softplus
    beta_all = _sigmoid(ab)

    c2 = 2 * CHUNK
    ri = lax.broadcasted_iota(jnp.int32, (c2, c2), 0)
    ci = lax.broadcasted_iota(jnp.int32, (c2, c2), 1)
    tri2 = ((ri >= ci) & ((ri >> CHUNK_LOG2) == (ci >> CHUNK_LOG2))).astype(F32)
    r1 = lax.broadcasted_iota(jnp.int32, (CHUNK, CHUNK), 0)
    c1 = lax.broadcasted_iota(jnp.int32, (CHUNK, CHUNK), 1)
    tri = r1 >= c1
    strict = r1 > c1
    qscale = GDN_HD ** -0.5

    def l2n(v):
        return v * lax.rsqrt(jnp.sum(v * v, -1, keepdims=True) + NORM_EPS)

    for pair in range(tb // c2):
        pr = slice(pair * c2, (pair + 1) * c2)
        gc2 = _dot(tri2, g_all[pr], HIGHEST)
        gc2t = gc2.T
        for half in range(2):
            rows = slice(pair * c2 + half * CHUNK, pair * c2 + (half + 1) * CHUNK)
            hr = slice(half * CHUNK, (half + 1) * CHUNK)
            for h in range(GDN_HEADS):
                hs = slice(h * GDN_HD, (h + 1) * GDN_HD)
                gcol = gc2[hr, h:h + 1]
                grow = gc2t[h:h + 1, hr]
                decay = jnp.where(tri, jnp.exp(jnp.where(tri, gcol - grow, 0.0)), 0.0)
                qh = l2n(x[rows, hs]) * qscale
                kh = l2n(x[rows, MIX_W + h * GDN_HD:MIX_W + (h + 1) * GDN_HD])
                vh = x[rows, 2 * MIX_W + h * GDN_HD:2 * MIX_W + (h + 1) * GDN_HD]
                bh = beta_all[rows, GDN_HEADS + h:GDN_HEADS + h + 1]
                kb = kh * bh
                a_mat = jnp.where(strict, _dot_nt(kb, kh) * decay, 0.0)
                t_inv = _unit_lower_inverse(a_mat)
                egc = jnp.exp(gcol)
                rhs = jnp.concatenate([vh * bh, kb * egc], axis=1)
                sol = _dot(t_inv, rhs, HIGHEST)
                u = sol[:, :GDN_HD]
                w = sol[:, GDN_HD:]
                qk = jnp.where(tri, _dot_nt(qh, kh) * decay, 0.0)
                g_last = gcol[CHUNK - 1:CHUNK]
                k_tail = kh * jnp.exp(g_last - gcol)
                state = s_ref[h]
                v_new = u - _dot(w, state)
                o = _dot(qh * egc, state) + _dot(qk, v_new)
                s_ref[h] = state * jnp.exp(g_last) + _dot_tn(k_tail, v_new)
                o_ref[rows, hs] = o


def _gdn(raw, ab, conv, alog, dtb, batch, seq, tb=256):
    nt = seq // tb
    return pl.pallas_call(
        functools.partial(_gdn_body, tb=tb),
        grid=(batch, nt),
        in_specs=[pl.BlockSpec((tb, 3 * MIX_W), lambda b, i: (b * nt + i, 0)),
                  pl.BlockSpec((tb, LANE), lambda b, i: (b * nt + i, 0)),
                  pl.BlockSpec((CONV_K, 3 * MIX_W), lambda b, i: (0, 0)),
                  pl.BlockSpec((1, LANE), lambda b, i: (0, 0)),
                  pl.BlockSpec((1, LANE), lambda b, i: (0, 0))],
        out_specs=pl.BlockSpec((tb, MIX_W), lambda b, i: (b * nt + i, 0)),
        out_shape=jax.ShapeDtypeStruct((batch * seq, MIX_W), F32),
        scratch_shapes=[pltpu.VMEM((GDN_HEADS, GDN_HD, GDN_HD), F32),
                        pltpu.VMEM((SUBLANE, 3 * MIX_W), F32),
                        pltpu.VMEM((SUBLANE + tb, 3 * MIX_W), F32)],
        compiler_params=_params("parallel", "arbitrary"),
        name="gated_delta_net",
    )(raw, ab, conv, alog, dtb)


def _gla_constants():
    c = CHUNK
    t = np.arange(c)[:, None]
    s = np.arange(c)[None, :]
    ops = [(s <= t), (s > t)]
    masks = []
    for m in GLA_LEVELS:
        r = (t // (2 * m)) * (2 * m) + m
        upper = (t % (2 * m)) >= m
        ops.append(np.where(upper, (s > r) & (s <= t), (s > t) & (s <= r)))
        i, j = t, s
        masks.append((i // (2 * m) == j // (2 * m)) & ((i % (2 * m)) >= m) & ((j % (2 * m)) < m))
    return (np.concatenate(ops, 0).astype(np.float32), np.stack(masks).astype(np.float32))


def _gla_body(x_ref, wlr_ref, blr_ref, ops_ref, lm_ref, o_ref, s_ref, *, tb):
    @pl.when(pl.program_id(1) == 0)
    def _():
        s_ref[...] = jnp.zeros_like(s_ref)

    hw = GLA_DV
    kw = GLA_HEADS * hw
    r1 = lax.broadcasted_iota(jnp.int32, (CHUNK, CHUNK), 0)
    c1 = lax.broadcasted_iota(jnp.int32, (CHUNK, CHUNK), 1)
    eye = r1 == c1
    qscale = GLA_DK ** -0.5
    for c in range(tb // CHUNK):
        rows = slice(c * CHUNK, (c + 1) * CHUNK)
        lr = x_ref[rows, 3 * kw:3 * kw + LANE]
        z = _dot(lr, wlr_ref[...]) + blr_ref[...]
        log_a = (jnp.minimum(z, 0.0) - jnp.log1p(jnp.exp(-jnp.abs(z)))) / GLA_TAU
        e_all = jnp.exp(_dot(ops_ref[...], log_a, HIGHEST))
        for h in range(GLA_HEADS):
            hs = slice(h * hw, (h + 1) * hw)
            qh = x_ref[rows, hs] * qscale
            kh = x_ref[rows, kw + h * hw:kw + (h + 1) * hw]
            vh = x_ref[rows, 2 * kw + h * hw:2 * kw + (h + 1) * hw]
            scores = jnp.where(eye, _dot_nt(qh, kh), 0.0)
            for lvl in range(len(GLA_LEVELS)):
                xl = e_all[(2 + lvl) * CHUNK:(3 + lvl) * CHUNK, hs]
                scores = scores + lm_ref[lvl] * _dot_nt(qh * xl, kh * xl)
            e_cum = e_all[0:CHUNK, hs]
            e_tail = e_all[CHUNK:2 * CHUNK, hs]
            state_t = s_ref[h]
            o = _dot_nt(qh * e_cum, state_t) + _dot(scores, vh)
            s_ref[h] = state_t * e_cum[CHUNK - 1:CHUNK] + _dot_tn(vh, kh * e_tail)
            o_ref[rows, hs] = o


def _gla(x, wlr, blr, ops, lmask, batch, seq, tb=256):
    nt = seq // tb
    width = x.shape[1]
    return pl.pallas_call(
        functools.partial(_gla_body, tb=tb),
        grid=(batch, nt),
        in_specs=[pl.BlockSpec((tb, width), lambda b, i: (b * nt + i, 0)),
                  pl.BlockSpec(wlr.shape, lambda b, i: (0, 0)),
                  pl.BlockSpec(blr.shape, lambda b, i: (0, 0)),
                  pl.BlockSpec(ops.shape, lambda b, i: (0, 0)),
                  pl.BlockSpec(lmask.shape, lambda b, i: (0, 0, 0))],
        out_specs=pl.BlockSpec((tb, MIX_W), lambda b, i: (b * nt + i, 0)),
        out_shape=jax.ShapeDtypeStruct((batch * seq, MIX_W), F32),
        scratch_shapes=[pltpu.VMEM((GLA_HEADS, GLA_DV, GLA_DV), F32)],
        compiler_params=_params("parallel", "arbitrary"),
        name="gla",
    )(x, wlr, blr, ops, lmask)


def _merge_body(x_ref, yn_ref, og_ref, ol_ref, wzr_ref, wmg_ref, ng_ref, nl_ref, wup_ref, wout_ref,
                lg_ref, lb_ref, o_ref, *, alpha):
    x = x_ref[...]
    xb = x.astype(BF16)
    zr = _dot(xb, wzr_ref[...])

    def head_rms(o, gain):
        parts = []
        for h in range(MIX_W // LANE):
            oh = o[:, h * LANE:(h + 1) * LANE]
            parts.append(oh * lax.rsqrt(jnp.mean(oh * oh, -1, keepdims=True) + NORM_EPS) * gain)
        return jnp.concatenate(parts, axis=1)

    z = zr[:, :MIX_W]
    r = zr[:, MIX_W:]
    y_gdn = head_rms(og_ref[...], ng_ref[...]) * (z * _sigmoid(z))
    y_gla = head_rms(ol_ref[...], nl_ref[...]) * (r * _sigmoid(r))
    ys = (yn_ref[...], y_gdn.astype(BF16), y_gla.astype(BF16))
    merged = jnp.zeros(x.shape, F32)
    for br in range(N_BRANCH):
        gate = _sigmoid(_dot(xb, wmg_ref[:, br * D_MODEL:(br + 1) * D_MODEL]))
        merged = merged + gate * _dot(ys[br], wup_ref[br])
    h = _dot(merged.astype(BF16), wout_ref[...])
    o_ref[...] = _layer_norm(alpha * x + h, lg_ref[...], lb_ref[...])


def _const_spec(shape):
    nd = len(shape)
    return pl.BlockSpec(shape, lambda i: (0,) * nd, pipeline_mode=pl.Buffered(1))


def _merge(x, y_nsa, o_gdn, o_gla, wzr, wmg, ng, nl, wup, wout, lg, lb, alpha, tm=256):
    m = x.shape[0]
    row = lambda w: pl.BlockSpec((tm, w), lambda i: (i, 0))
    return pl.pallas_call(
        functools.partial(_merge_body, alpha=alpha),
        grid=(m // tm,),
        in_specs=[row(D_MODEL), row(MIX_W), row(MIX_W), row(MIX_W),
                  _const_spec(wzr.shape), _const_spec(wmg.shape), _const_spec(ng.shape), _const_spec(nl.shape),
                  _const_spec(wup.shape), _const_spec(wout.shape), _const_spec(lg.shape), _const_spec(lb.shape)],
        out_specs=row(D_MODEL),
        out_shape=jax.ShapeDtypeStruct((m, D_MODEL), F32),
        compiler_params=_params("parallel"),
        name="mixer_merge",
    )(x, y_nsa, o_gdn, o_gla, wzr, wmg, ng, nl, wup, wout, lg, lb)


def _xattn_body(x_ref, kv_ref, wq_ref, wo_ref, lg_ref, lb_ref, o_ref, *, alpha):
    x = x_ref[...]
    q = _dot(x.astype(BF16), wq_ref[...]).astype(BF16)
    scale = XA_HD ** -0.5
    outs = []
    for h in range(XA_HEADS):
        hs = slice(h * XA_HD, (h + 1) * XA_HD)
        s = _dot_nt(q[:, hs], kv_ref[:, hs]) * scale
        m = jnp.max(s, -1, keepdims=True)
        e = jnp.exp(s - m)
        p = e / jnp.sum(e, -1, keepdims=True)
        outs.append(_dot(p.astype(BF16), kv_ref[:, D_MODEL + h * XA_HD:D_MODEL + (h + 1) * XA_HD]))
    o = jnp.concatenate(outs, axis=1)
    h_out = _dot(o.astype(BF16), wo_ref[...])
    o_ref[...] = _layer_norm(alpha * x + h_out, lg_ref[...], lb_ref[...])


def _xattn(x, kvm, wq, wo, lg, lb, alpha, batch, seq, tm=512):
    nt = seq // tm
    mt = kvm.shape[0] // batch
    cs = lambda shape: pl.BlockSpec(shape, lambda b, i: (0,) * len(shape), pipeline_mode=pl.Buffered(1))
    return pl.pallas_call(
        functools.partial(_xattn_body, alpha=alpha),
        grid=(batch, nt),
        in_specs=[pl.BlockSpec((tm, D_MODEL), lambda b, i: (b * nt + i, 0)),
                  pl.BlockSpec((mt, 2 * D_MODEL), lambda b, i: (b, 0)),
                  cs(wq.shape), cs(wo.shape), cs(lg.shape), cs(lb.shape)],
        out_specs=pl.BlockSpec((tm, D_MODEL), lambda b, i: (b * nt + i, 0)),
        out_shape=jax.ShapeDtypeStruct((batch * seq, D_MODEL), F32),
        compiler_params=_params("parallel", "parallel"),
        name="mem_xattn",
    )(x, kvm, wq, wo, lg, lb)


def _mlp_body(x_ref, w1_ref, w2_ref, lg_ref, lb_ref, o_ref, *, alpha):
    x = x_ref[...]
    h = jnp.maximum(_dot(x.astype(BF16), w1_ref[...]), 0.0)
    h = (h * h).astype(BF16)
    o_ref[...] = _layer_norm(alpha * x + _dot(h, w2_ref[...]), lg_ref[...], lb_ref[...])


def _mlp(x, w1, w2, lg, lb, alpha, tm=512):
    m = x.shape[0]
    return pl.pallas_call(
        functools.partial(_mlp_body, alpha=alpha),
        grid=(m // tm,),
        in_specs=[pl.BlockSpec((tm, D_MODEL), lambda i: (i, 0)),
                  _const_spec(w1.shape), _const_spec(w2.shape), _const_spec(lg.shape), _const_spec(lb.shape)],
        out_specs=pl.BlockSpec((tm, D_MODEL), lambda i: (i, 0)),
        out_shape=jax.ShapeDtypeStruct((m, D_MODEL), F32),
        compiler_params=_params("parallel"),
        name="sq_relu_mlp",
    )(x, w1, w2, lg, lb)


def _pad_cols(w, width):
    return jnp.pad(w, ((0, 0), (0, width - w.shape[1])))


def _nsa_weights(w_in):
    def kv(kind, g):
        o = O_NSA_KV + (kind * NSA_GROUPS + g) * NSA_HD
        return w_in[:, o:o + NSA_HD]
    cols = [w_in[:, O_NSA_Q:O_NSA_Q + NSA_HEADS * NSA_HD]]
    for g in range(NSA_GROUPS):
        cols += [kv(2, g), kv(4, g), kv(3, g), kv(5, g)]
    cols += [kv(0, 0), kv(0, 1), kv(1, 0), kv(1, 1)]
    main = jnp.concatenate(cols, axis=1).astype(BF16)
    gw = NSA_HPG * 3
    gates = [_pad_cols(w_in[:, O_NSA_G + g * gw:O_NSA_G + (g + 1) * gw], LANE) for g in range(NSA_GROUPS)]
    return main, jnp.concatenate(gates, axis=1).astype(BF16)


def _gla_weights(w_in, w_lr, b_lr):
    def pad_heads(w):
        lead = w.shape[0]
        w = w.reshape(lead, GLA_HEADS, GLA_DK)
        return jnp.pad(w, ((0, 0), (0, 0), (0, GLA_DV - GLA_DK))).reshape(lead, GLA_HEADS * GLA_DV)
    cols = [pad_heads(w_in[:, O_GLA_Q:O_GLA_Q + GLA_HEADS * GLA_DK]),
            pad_heads(w_in[:, O_GLA_K:O_GLA_K + GLA_HEADS * GLA_DK]),
            w_in[:, O_GLA_V:O_GLA_V + GLA_HEADS * GLA_DV],
            _pad_cols(w_in[:, O_GLA_LR:O_GLA_LR + GLA_RANK], LANE)]
    wlr = jnp.pad(pad_heads(w_lr), ((0, LANE - GLA_RANK), (0, 0)))
    return jnp.concatenate(cols, axis=1).astype(BF16), wlr, pad_heads(b_lr[None, :])


def _overlap_matrix(seq):
    nc = seq // CMP_STRIDE
    ns = seq // SEL_BLK
    c_start = np.arange(nc)[:, None] * CMP_STRIDE
    s_start = np.arange(ns)[None, :] * SEL_BLK
    return jnp.asarray((c_start < s_start + SEL_BLK) & (c_start + CMP_LEN > s_start), BF16)


def _layer(x, kvm_src, p, l, batch, seq, alpha):
    bt = batch * seq
    w_in = p["w_in"][l]

    w_main, w_gates = _nsa_weights(w_in)
    main = _mm(x, w_main, BF16)
    gate_logits = _mm(x, w_gates, F32)
    nc = seq // CMP_STRIDE
    cmp_in = main[:, 2 * MIX_W:].reshape(bt, 2 * NSA_GROUPS, NSA_HD).transpose(1, 0, 2)
    cmp_in = cmp_in.reshape(2 * NSA_GROUPS, batch * nc, CMP_STRIDE * NSA_HD)
    half = CMP_STRIDE * NSA_HD
    w1 = p["cmp_w1"][l].reshape(2, 2, half, CMP_HID).astype(BF16)
    pe = jnp.broadcast_to(p["cmp_pe"][l].reshape(2, 2, 1, half), (2, 2, SUBLANE, half)).astype(BF16)
    cmp_tok = _compress(cmp_in, w1, p["cmp_b1"][l][:, None, :], pe, p["cmp_w2"][l].astype(BF16), batch)
    y_nsa = _nsa_attention(main, gate_logits, cmp_tok, _overlap_matrix(seq), batch, seq)

    w_gdn = jnp.concatenate([w_in[:, O_GDN_QKV:O_GDN_QKV + 3 * MIX_W],
                             _pad_cols(w_in[:, O_GDN_A:O_GDN_A + 2 * GDN_HEADS], LANE)], axis=1).astype(BF16)
    gdn_in = _mm(x, w_gdn, F32)
    alog = _pad_cols(p["gdn_a_log"][l][None, :], LANE)
    dtb = _pad_cols(p["gdn_dt_bias"][l][None, :], LANE)
    o_gdn = _gdn(gdn_in[:, :3 * MIX_W], gdn_in[:, 3 * MIX_W:], p["gdn_conv"][l], alog, dtb, batch, seq)

    w_gla, wlr, blr = _gla_weights(w_in, p["gla_w_lr"][l], p["gla_b_lr"][l])
    gla_in = _mm(x, w_gla, F32)
    ops, lmask = _gla_constants()
    o_gla = _gla(gla_in, wlr, blr, jnp.asarray(ops), jnp.asarray(lmask), batch, seq)

    wzr = jnp.concatenate([w_in[:, O_GDN_Z:O_GDN_Z + MIX_W], w_in[:, O_GLA_R:O_GLA_R + MIX_W]], axis=1).astype(BF16)
    wmg = w_in[:, O_MERGE:O_MERGE + N_BRANCH * D_MODEL].astype(BF16)
    tile4 = lambda v: jnp.tile(v[None, :], (1, 1))
    x = _merge(x, y_nsa, o_gdn, o_gla, wzr, wmg, tile4(p["gdn_norm"][l]), tile4(p["gla_norm"][l]),
               p["w_up"][l].astype(BF16), p["w_out"][l].astype(BF16),
               p["ln_g"][l, 0][None, :], p["ln_b"][l, 0][None, :], alpha)

    wkv = jnp.concatenate([p["xa_wk"][l], p["xa_wv"][l]], axis=1).astype(BF16)
    kvm = _mm(kvm_src, wkv, BF16)
    x = _xattn(x, kvm, p["xa_wq"][l].astype(BF16), p["xa_wo"][l].astype(BF16),
               p["ln_g"][l, 1][None, :], p["ln_b"][l, 1][None, :], alpha, batch, seq)

    return _mlp(x, p["mlp_w1"][l].astype(BF16), p["mlp_w2"][l].astype(BF16),
                p["ln_g"][l, 2][None, :], p["ln_b"][l, 2][None, :], alpha)


def kernel(x, mem, w_in, cmp_pe, cmp_w1, cmp_b1, cmp_w2, gdn_conv, gdn_a_log, gdn_dt_bias, gdn_norm, gla_w_lr,
           gla_b_lr, gla_norm, w_up, w_out, xa_wq, xa_wk, xa_wv, xa_wo, mlp_w1, mlp_w2, ln_g, ln_b):
    batch, seq, d = x.shape
    depth = w_in.shape[0]
    alpha = (2.0 * depth) ** 0.25
    p = dict(w_in=w_in, cmp_pe=cmp_pe, cmp_w1=cmp_w1, cmp_b1=cmp_b1, cmp_w2=cmp_w2, gdn_conv=gdn_conv,
             gdn_a_log=gdn_a_log, gdn_dt_bias=gdn_dt_bias, gdn_norm=gdn_norm, gla_w_lr=gla_w_lr,
             gla_b_lr=gla_b_lr, gla_norm=gla_norm, w_up=w_up, w_out=w_out, xa_wq=xa_wq, xa_wk=xa_wk,
             xa_wv=xa_wv, xa_wo=xa_wo, mlp_w1=mlp_w1, mlp_w2=mlp_w2, ln_g=ln_g, ln_b=ln_b)
    h = x.reshape(batch * seq, d)
    mem2 = mem.reshape(batch * mem.shape[1], d)
    for l in range(depth):
        h = _layer(h, mem2, p, l, batch, seq, alpha)
    return h.reshape(batch, seq, d)
```

```python
import functools

import numpy as np
import jax
import jax.numpy as jnp
from jax import lax
from jax.experimental import pallas as pl
from jax.experimental.pallas import tpu as pltpu

F32 = jnp.float32
BF16 = jnp.bfloat16
HIGHEST = lax.Precision.HIGHEST

D_MODEL = 1024
MIX_W = D_MODEL // 2
N_BRANCH = 3
NSA_HEADS = 8
NSA_HD = MIX_W // NSA_HEADS
NSA_GROUPS = 2
NSA_HPG = NSA_HEADS // NSA_GROUPS
CMP_LEN = 32
CMP_STRIDE = 16
CMP_HID = D_MODEL // 4
SEL_BLK = 64
SEL_TOPK = 16
WIN = 512
Q_BLK = 128
GDN_HEADS = 4
GDN_HD = MIX_W // GDN_HEADS
CONV_K = 4
CHUNK = 64
GLA_HEADS = 4
GLA_DK = MIX_W // (2 * GLA_HEADS)
GLA_DV = MIX_W // GLA_HEADS
GLA_RANK = 16
GLA_TAU = 16.0
XA_HEADS = 4
XA_HD = D_MODEL // XA_HEADS
D_FF = 4 * D_MODEL
LN_EPS = 1e-5
NORM_EPS = 1e-6
NEG_INF = -1e30
FORCE_BONUS = 1e4

LANE = 128
SUBLANE = 8
VMEM_LIMIT_BYTES = 48 * 1024 * 1024
SEL_TILE = 512
REMOVED = -3e38
GLA_LEVELS = (32, 16, 8, 4, 2, 1)
Q_BLK_LOG2 = Q_BLK.bit_length() - 1
SEL_BLK_LOG2 = SEL_BLK.bit_length() - 1
CHUNK_LOG2 = CHUNK.bit_length() - 1

_IN_SIZES = (NSA_HEADS * NSA_HD, 6 * NSA_GROUPS * NSA_HD, NSA_HEADS * 3, 3 * MIX_W, GDN_HEADS, GDN_HEADS,
             MIX_W, GLA_HEADS * GLA_DK, GLA_HEADS * GLA_DK, GLA_HEADS * GLA_DV, GLA_RANK, MIX_W,
             N_BRANCH * D_MODEL)
_OFF = tuple(int(v) for v in np.cumsum((0,) + _IN_SIZES))
(O_NSA_Q, O_NSA_KV, O_NSA_G, O_GDN_QKV, O_GDN_A, O_GDN_B, O_GDN_Z, O_GLA_Q, O_GLA_K, O_GLA_V, O_GLA_LR,
 O_GLA_R, O_MERGE, _) = _OFF


def _params(*sem):
    return pltpu.CompilerParams(dimension_semantics=sem, vmem_limit_bytes=VMEM_LIMIT_BYTES)


def _dot(a, b, precision=None):
    return jnp.dot(a, b, preferred_element_type=F32, precision=precision)


def _dot_nt(a, b):
    return lax.dot_general(a, b, (((1,), (1,)), ((), ())), preferred_element_type=F32)


def _dot_tn(a, b):
    return lax.dot_general(a, b, (((0,), (0,)), ((), ())), preferred_element_type=F32)


def _sigmoid(x):
    return 1.0 / (1.0 + jnp.exp(-x))


def _layer_norm(v, g, b):
    mu = jnp.mean(v, -1, keepdims=True)
    c = v - mu
    var = jnp.mean(c * c, -1, keepdims=True)
    return c * lax.rsqrt(var + LN_EPS) * g + b


def _mm_body(x_ref, w_ref, o_ref):
    o_ref[...] = _dot(x_ref[...].astype(BF16), w_ref[...]).astype(o_ref.dtype)


def _mm(x, w, out_dtype, tm=512):
    m, k = x.shape
    n = w.shape[1]
    tm = min(tm, m)
    return pl.pallas_call(
        _mm_body,
        grid=(m // tm,),
        in_specs=[pl.BlockSpec((tm, k), lambda i: (i, 0)),
                  pl.BlockSpec((k, n), lambda i: (0, 0))],
        out_specs=pl.BlockSpec((tm, n), lambda i: (i, 0)),
        out_shape=jax.ShapeDtypeStruct((m, n), out_dtype),
        compiler_params=_params("parallel"),
        name="proj_mm",
    )(x, w)


def _compress_body(ch_ref, w1_ref, b1_ref, pe_ref, w2_ref, o_ref):
    nc = ch_ref.shape[0]
    ch = ch_ref[...]
    top = _dot(ch, w1_ref[0])
    bot = _dot(ch, w1_ref[1])
    bot_next = pltpu.roll(bot, nc - 1, axis=0)
    pe_term = _dot(pe_ref[0], w1_ref[0]) + _dot(pe_ref[1], w1_ref[1])
    h = top + bot_next + pe_term[0:1] + b1_ref[...]
    h = jax.nn.gelu(h, approximate=True)
    o_ref[...] = _dot(h.astype(BF16), w2_ref[...]).astype(o_ref.dtype)


def _compress(ch, w1, b1, pe, w2, batch):
    nc = ch.shape[1] // batch
    half = CMP_STRIDE * NSA_HD
    return pl.pallas_call(
        _compress_body,
        grid=(2, NSA_GROUPS, batch),
        in_specs=[pl.BlockSpec((None, nc, half), lambda t, g, b: (t * NSA_GROUPS + g, b, 0)),
                  pl.BlockSpec((None, 2, half, CMP_HID), lambda t, g, b: (t, 0, 0, 0)),
                  pl.BlockSpec((None, 1, CMP_HID), lambda t, g, b: (t, 0, 0)),
                  pl.BlockSpec((None, 2, SUBLANE, half), lambda t, g, b: (t, 0, 0, 0)),
                  pl.BlockSpec((None, CMP_HID, NSA_HD), lambda t, g, b: (t, 0, 0))],
        out_specs=pl.BlockSpec((None, nc, NSA_HD), lambda t, g, b: (t * NSA_GROUPS + g, b, 0)),
        out_shape=jax.ShapeDtypeStruct((2 * NSA_GROUPS, batch * nc, NSA_HD), BF16),
        compiler_params=_params("parallel", "parallel", "parallel"),
        name="nsa_compress",
    )(ch, w1, b1, pe, w2)


def _masked_softmax_rows(s, mask):
    s = jnp.where(mask, s, NEG_INF)
    m = jnp.max(s, -1, keepdims=True)
    e = jnp.where(mask, jnp.exp(s - m), 0.0)
    return e / jnp.maximum(jnp.sum(e, -1, keepdims=True), 1e-30)


def _nsa_body(q_ref, kv_ref, kc_ref, vc_ref, gl_ref, ov_ref, o_ref, *, seq):
    g = pl.program_id(1)
    qb = pl.program_id(2)
    nc = seq // CMP_STRIDE
    ns = seq // SEL_BLK
    hq = NSA_HPG * Q_BLK
    qs = qb * Q_BLK
    scale = NSA_HD ** -0.5

    q4 = q_ref[...] * jnp.asarray(scale, q_ref.dtype)
    qst = jnp.concatenate([q4[:, h * NSA_HD:(h + 1) * NSA_HD] for h in range(NSA_HPG)], axis=0)
    row = lax.broadcasted_iota(jnp.int32, (hq, 1), 0)
    hrow = row >> Q_BLK_LOG2
    slopes = [jnp.where(g == 0, 2.0 ** -(h + 1), 2.0 ** -(NSA_HPG + h + 1)).astype(F32) for h in range(NSA_HPG)]
    slope = jnp.zeros((hq, 1), F32)
    for h in range(NSA_HPG):
        slope = jnp.where(hrow == h, slopes[h], slope)
    tq = qs + (row & (Q_BLK - 1))
    tq1 = qs + lax.broadcasted_iota(jnp.int32, (Q_BLK, 1), 0)

    s = _dot_nt(qst, kc_ref[...])
    c_end = lax.broadcasted_iota(jnp.int32, (1, nc), 1) * CMP_STRIDE + (CMP_LEN - 1)
    dist = tq - c_end
    p_cmp = _masked_softmax_rows(s - slope * dist.astype(F32), dist >= 0)
    o_cmp = _dot(p_cmp.astype(BF16), vc_ref[...])

    p_sum = p_cmp[0:Q_BLK]
    for h in range(1, NSA_HPG):
        p_sum = p_sum + p_cmp[h * Q_BLK:(h + 1) * Q_BLK]
    p_hi = p_sum.astype(BF16)
    p_lo = (p_sum - p_hi.astype(F32)).astype(BF16)
    imp_t = (_dot(p_hi, ov_ref[...]) + _dot(p_lo, ov_ref[...])).T
    blk = lax.broadcasted_iota(jnp.int32, (ns, Q_BLK), 0)
    tq_row = qs + lax.broadcasted_iota(jnp.int32, (1, Q_BLK), 1)
    cur = tq_row >> SEL_BLK_LOG2
    forced = (blk == 0) | (blk == cur) | (blk == cur - 1)
    valid = blk * SEL_BLK <= tq_row
    score = jnp.where(valid, imp_t + jnp.where(forced, FORCE_BONUS, 0.0), NEG_INF)

    def pick(_, carry):
        sc, sel = carry
        m = jnp.max(sc, 0, keepdims=True)
        idx = jnp.min(jnp.where(sc == m, blk, ns), 0, keepdims=True)
        hit = blk == idx
        return jnp.where(hit, REMOVED, sc), jnp.where(hit, 1.0, sel)

    _, sel_t = lax.fori_loop(0, min(SEL_TOPK, ns), pick, (score, jnp.zeros((ns, Q_BLK), F32)))
    sel = sel_t.T
    sel_bias = ((sel - 1.0) * -NEG_INF).astype(BF16)
    blk_any = jnp.max(sel, 0, keepdims=True)
    blk_lane = lax.broadcasted_iota(jnp.int32, (1, ns), 1)

    tile_blks = SEL_TILE // SEL_BLK
    blk_row = lax.broadcasted_iota(jnp.int32, (ns, SEL_TILE), 0)
    blk_of_key = lax.broadcasted_iota(jnp.int32, (ns, SEL_TILE), 1) >> SEL_BLK_LOG2
    key_in_tile = lax.broadcasted_iota(jnp.int32, (1, SEL_TILE), 1)

    def sel_tile(kt, carry):
        def attend(carry):
            m_i, l_i, acc = carry
            k0 = pl.multiple_of(kt * SEL_TILE, SEL_TILE)
            k = kv_ref[pl.ds(k0, SEL_TILE), 0:NSA_HD]
            v = kv_ref[pl.ds(k0, SEL_TILE), 2 * NSA_HD:3 * NSA_HD]
            expand = jnp.where(blk_row == blk_of_key + kt * tile_blks, 1.0, 0.0).astype(BF16)
            d = tq1 - (k0 + key_in_tile)
            bias = jnp.where(d >= 0, _dot(sel_bias, expand), NEG_INF)
            d_f = d.astype(F32)
            st = _dot_nt(qst, k) + jnp.concatenate([bias - slopes[h] * d_f for h in range(NSA_HPG)], axis=0)
            m_new = jnp.maximum(m_i, jnp.max(st, -1, keepdims=True))
            p = jnp.exp(st - m_new)
            alpha = jnp.exp(m_i - m_new)
            l_new = alpha * l_i + jnp.sum(p, -1, keepdims=True)
            return m_new, l_new, alpha * acc + _dot(p.astype(BF16), v)

        in_tile = (blk_lane >= kt * tile_blks) & (blk_lane < (kt + 1) * tile_blks)
        any_sel = jnp.max(jnp.where(in_tile, blk_any, 0.0))
        return lax.cond(any_sel > 0.0, attend, lambda c: c, carry)

    n_tiles = (qs + Q_BLK + SEL_TILE - 1) // SEL_TILE
    _, l_s, acc_s = lax.fori_loop(
        0, n_tiles, sel_tile,
        (jnp.full((hq, 1), NEG_INF, F32), jnp.zeros((hq, 1), F32), jnp.zeros((hq, NSA_HD), F32)))
    o_slc = acc_s / jnp.maximum(l_s, 1e-30)

    wlen = WIN + Q_BLK
    w0 = pl.multiple_of(jnp.maximum(qs - WIN, 0), Q_BLK)
    kw = kv_ref[pl.ds(w0, wlen), NSA_HD:2 * NSA_HD]
    vw = kv_ref[pl.ds(w0, wlen), 3 * NSA_HD:4 * NSA_HD]
    d = tq - (w0 + lax.broadcasted_iota(jnp.int32, (1, wlen), 1))
    s = _dot_nt(qst, kw) - slope * d.astype(F32)
    p_win = _masked_softmax_rows(s, (d >= 0) & (d < WIN))
    o_win = _dot(p_win.astype(BF16), vw)

    gates = _sigmoid(gl_ref[...])
    outs = []
    for h in range(NSA_HPG):
        r = slice(h * Q_BLK, (h + 1) * Q_BLK)
        outs.append(gates[:, 3 * h:3 * h + 1] * o_cmp[r] + gates[:, 3 * h + 1:3 * h + 2] * o_slc[r]
                    + gates[:, 3 * h + 2:3 * h + 3] * o_win[r])
    o_ref[...] = jnp.concatenate(outs, axis=1).astype(o_ref.dtype)


def _nsa_attention(main, gate_logits, cmp_tok, overlap, batch, seq):
    nq = seq // Q_BLK
    nc = seq // CMP_STRIDE
    ns = seq // SEL_BLK
    gw = NSA_HPG * NSA_HD
    return pl.pallas_call(
        functools.partial(_nsa_body, seq=seq),
        grid=(batch, NSA_GROUPS, nq),
        in_specs=[pl.BlockSpec((Q_BLK, gw), lambda b, g, i: (b * nq + i, g)),
                  pl.BlockSpec((seq, gw), lambda b, g, i: (b, NSA_GROUPS + g)),
                  pl.BlockSpec((None, nc, NSA_HD), lambda b, g, i: (g, b, 0)),
                  pl.BlockSpec((None, nc, NSA_HD), lambda b, g, i: (NSA_GROUPS + g, b, 0)),
                  pl.BlockSpec((Q_BLK, LANE), lambda b, g, i: (b * nq + i, g)),
                  pl.BlockSpec((nc, ns), lambda b, g, i: (0, 0))],
        out_specs=pl.BlockSpec((Q_BLK, gw), lambda b, g, i: (b * nq + i, g)),
        out_shape=jax.ShapeDtypeStruct((batch * seq, MIX_W), BF16),
        compiler_params=_params("parallel", "parallel", "arbitrary"),
        name="nsa_attention",
    )(main, main, cmp_tok, cmp_tok, gate_logits, overlap)


def _unit_lower_inverse(a):
    c = a.shape[0]
    eye = (lax.broadcasted_iota(jnp.int32, (c, c), 0) == lax.broadcasted_iota(jnp.int32, (c, c), 1)).astype(F32)
    x = eye - a
    p = a
    n = 1
    while 2 * n < c:
        p = _dot(p, p, HIGHEST)
        x = x + _dot(x, p, HIGHEST)
        n *= 2
    return x


def _gdn_body(raw_ref, ab_ref, conv_ref, alog_ref, dtb_ref, o_ref, s_ref, tail_ref, buf_ref, *, tb):
    @pl.when(pl.program_id(1) == 0)
    def _():
        s_ref[...] = jnp.zeros_like(s_ref)
        tail_ref[...] = jnp.zeros_like(tail_ref)

    raw = raw_ref[...]
    buf_ref[0:SUBLANE] = tail_ref[...]
    buf_ref[SUBLANE:SUBLANE + tb] = raw
    tail_ref[...] = raw[tb - SUBLANE:tb]
    x = jnp.zeros_like(raw)
    for k in range(CONV_K):
        x = x + buf_ref[pl.ds(SUBLANE - (CONV_K - 1) + k, tb)] * conv_ref[k:k + 1]
    x = x * _sigmoid(x)

    ab = ab_ref[...]
    zed = ab + dtb_ref[...]
    softplus = jnp.maximum(zed, 0.0) + jnp.log1p(jnp.exp(-jnp.abs(zed)))
    g_all = -jnp.exp(alog_ref[...]) * softplus
    beta_all = _sigmoid(ab)

    c2 = 2 * CHUNK
    ri = lax.broadcasted_iota(jnp.int32, (c2, c2), 0)
    ci = lax.broadcasted_iota(jnp.int32, (c2, c2), 1)
    tri2 = ((ri >= ci) & ((ri >> CHUNK_LOG2) == (ci >> CHUNK_LOG2))).astype(F32)
    r1 = lax.broadcasted_iota(jnp.int32, (CHUNK, CHUNK), 0)
    c1 = lax.broadcasted_iota(jnp.int32, (CHUNK, CHUNK), 1)
    tri = r1 >= c1
    strict = r1 > c1
    qscale = GDN_HD ** -0.5

    def l2n(v):
        return v * lax.rsqrt(jnp.sum(v * v, -1, keepdims=True) + NORM_EPS)

    for pair in range(tb // c2):
        pr = slice(pair * c2, (pair + 1) * c2)
        gc2 = _dot(tri2, g_all[pr], HIGHEST)
        gc2t = gc2.T
        for half in range(2):
            rows = slice(pair * c2 + half * CHUNK, pair * c2 + (half + 1) * CHUNK)
            hr = slice(half * CHUNK, (half + 1) * CHUNK)
            for h in range(GDN_HEADS):
                hs = slice(h * GDN_HD, (h + 1) * GDN_HD)
                gcol = gc2[hr, h:h + 1]
                grow = gc2t[h:h + 1, hr]
                decay = jnp.where(tri, jnp.exp(jnp.where(tri, gcol - grow, 0.0)), 0.0)
                qh = l2n(x[rows, hs]) * qscale
                kh = l2n(x[rows, MIX_W + h * GDN_HD:MIX_W + (h + 1) * GDN_HD])
                vh = x[rows, 2 * MIX_W + h * GDN_HD:2 * MIX_W + (h + 1) * GDN_HD]
                bh = beta_all[rows, GDN_HEADS + h:GDN_HEADS + h + 1]
                kb = kh * bh
                a_mat = jnp.where(strict, _dot_nt(kb, kh) * decay, 0.0)
                t_inv = _unit_lower_inverse(a_mat)
                egc = jnp.exp(gcol)
                rhs = jnp.concatenate([vh * bh, kb * egc], axis=1)
                sol = _dot(t_inv, rhs, HIGHEST)
                u = sol[:, :GDN_HD]
                w = sol[:, GDN_HD:]
                qk = jnp.where(tri, _dot_nt(qh, kh) * decay, 0.0)
                g_last = gcol[CHUNK - 1:CHUNK]
                k_tail = kh * jnp.exp(g_last - gcol)
                state = s_ref[h]
                v_new = u - _dot(w, state)
                o = _dot(qh * egc, state) + _dot(qk, v_new)
                s_ref[h] = state * jnp.exp(g_last) + _dot_tn(k_tail, v_new)
                o_ref[rows, hs] = o


def _gdn(raw, ab, conv, alog, dtb, batch, seq, tb=256):
    nt = seq // tb
    return pl.pallas_call(
        functools.partial(_gdn_body, tb=tb),
        grid=(batch, nt),
        in_specs=[pl.BlockSpec((tb, 3 * MIX_W), lambda b, i: (b * nt + i, 0)),
                  pl.BlockSpec((tb, LANE), lambda b, i: (b * nt + i, 0)),
                  pl.BlockSpec((CONV_K, 3 * MIX_W), lambda b, i: (0, 0)),
                  pl.BlockSpec((1, LANE), lambda b, i: (0, 0)),
                  pl.BlockSpec((1, LANE), lambda b, i: (0, 0))],
        out_specs=pl.BlockSpec((tb, MIX_W), lambda b, i: (b * nt + i, 0)),
        out_shape=jax.ShapeDtypeStruct((batch * seq, MIX_W), F32),
        scratch_shapes=[pltpu.VMEM((GDN_HEADS, GDN_HD, GDN_HD), F32),
                        pltpu.VMEM((SUBLANE, 3 * MIX_W), F32),
                        pltpu.VMEM((SUBLANE + tb, 3 * MIX_W), F32)],
        compiler_params=_params("parallel", "arbitrary"),
        name="gated_delta_net",
    )(raw, ab, conv, alog, dtb)


def _gla_constants():
    c = CHUNK
    t = np.arange(c)[:, None]
    s = np.arange(c)[None, :]
    ops = [(s <= t), (s > t)]
    masks = []
    for m in GLA_LEVELS:
        r = (t // (2 * m)) * (2 * m) + m
        upper = (t % (2 * m)) >= m
        ops.append(np.where(upper, (s > r) & (s <= t), (s > t) & (s <= r)))
        i, j = t, s
        masks.append((i // (2 * m) == j // (2 * m)) & ((i % (2 * m)) >= m) & ((j % (2 * m)) < m))
    return (np.concatenate(ops, 0).astype(np.float32), np.stack(masks).astype(np.float32))


def _gla_body(x_ref, wlr_ref, blr_ref, ops_ref, lm_ref, o_ref, s_ref, *, tb):
    @pl.when(pl.program_id(1) == 0)
    def _():
        s_ref[...] = jnp.zeros_like(s_ref)

    hw = GLA_DV
    kw = GLA_HEADS * hw
    r1 = lax.broadcasted_iota(jnp.int32, (CHUNK, CHUNK), 0)
    c1 = lax.broadcasted_iota(jnp.int32, (CHUNK, CHUNK), 1)
    eye = r1 == c1
    qscale = GLA_DK ** -0.5
    for c in range(tb // CHUNK):
        rows = slice(c * CHUNK, (c + 1) * CHUNK)
        lr = x_ref[rows, 3 * kw:3 * kw + LANE]
        z = _dot(lr, wlr_ref[...]) + blr_ref[...]
        log_a = (jnp.minimum(z, 0.0) - jnp.log1p(jnp.exp(-jnp.abs(z)))) / GLA_TAU
        e_all = jnp.exp(_dot(ops_ref[...], log_a, HIGHEST))
        for h in range(GLA_HEADS):
            hs = slice(h * hw, (h + 1) * hw)
            qh = x_ref[rows, hs] * qscale
            kh = x_ref[rows, kw + h * hw:kw + (h + 1) * hw]
            vh = x_ref[rows, 2 * kw + h * hw:2 * kw + (h + 1) * hw]
            scores = jnp.where(eye, _dot_nt(qh, kh), 0.0)
            for lvl in range(len(GLA_LEVELS)):
                xl = e_all[(2 + lvl) * CHUNK:(3 + lvl) * CHUNK, hs]
                scores = scores + lm_ref[lvl] * _dot_nt(qh * xl, kh * xl)
            e_cum = e_all[0:CHUNK, hs]
            e_tail = e_all[CHUNK:2 * CHUNK, hs]
            state_t = s_ref[h]
            o = _dot_nt(qh * e_cum, state_t) + _dot(scores, vh)
            s_ref[h] = state_t * e_cum[CHUNK - 1:CHUNK] + _dot_tn(vh, kh * e_tail)
            o_ref[rows, hs] = o


def _gla(x, wlr, blr, ops, lmask, batch, seq, tb=256):
    nt = seq // tb
    width = x.shape[1]
    return pl.pallas_call(
        functools.partial(_gla_body, tb=tb),
        grid=(batch, nt),
        in_specs=[pl.BlockSpec((tb, width), lambda b, i: (b * nt + i, 0)),
                  pl.BlockSpec(wlr.shape, lambda b, i: (0, 0)),
                  pl.BlockSpec(blr.shape, lambda b, i: (0, 0)),
                  pl.BlockSpec(ops.shape, lambda b, i: (0, 0)),
                  pl.BlockSpec(lmask.shape, lambda b, i: (0, 0, 0))],
        out_specs=pl.BlockSpec((tb, MIX_W), lambda b, i: (b * nt + i, 0)),
        out_shape=jax.ShapeDtypeStruct((batch * seq, MIX_W), F32),
        scratch_shapes=[pltpu.VMEM((GLA_HEADS, GLA_DV, GLA_DV), F32)],
        compiler_params=_params("parallel", "arbitrary"),
        name="gla",
    )(x, wlr, blr, ops, lmask)


def _merge_body(x_ref, yn_ref, og_ref, ol_ref, wzr_ref, wmg_ref, ng_ref, nl_ref, wup_ref, wout_ref,
                lg_ref, lb_ref, o_ref, *, alpha):
    x = x_ref[...]
    xb = x.astype(BF16)
    zr = _dot(xb, wzr_ref[...])

    def head_rms(o, gain):
        parts = []
        for h in range(MIX_W // LANE):
            oh = o[:, h * LANE:(h + 1) * LANE]
            parts.append(oh * lax.rsqrt(jnp.mean(oh * oh, -1, keepdims=True) + NORM_EPS) * gain)
        return jnp.concatenate(parts, axis=1)

    z = zr[:, :MIX_W]
    r = zr[:, MIX_W:]
    y_gdn = head_rms(og_ref[...], ng_ref[...]) * (z * _sigmoid(z))
    y_gla = head_rms(ol_ref[...], nl_ref[...]) * (r * _sigmoid(r))
    ys = (yn_ref[...], y_gdn.astype(BF16), y_gla.astype(BF16))
    merged = jnp.zeros(x.shape, F32)
    for br in range(N_BRANCH):
        gate = _sigmoid(_dot(xb, wmg_ref[:, br * D_MODEL:(br + 1) * D_MODEL]))
        merged = merged + gate * _dot(ys[br], wup_ref[br])
    h = _dot(merged.astype(BF16), wout_ref[...])
    o_ref[...] = _layer_norm(alpha * x + h, lg_ref[...], lb_ref[...])


def _const_spec(shape):
    nd = len(shape)
    return pl.BlockSpec(shape, lambda i: (0,) * nd, pipeline_mode=pl.Buffered(1))


def _merge(x, y_nsa, o_gdn, o_gla, wzr, wmg, ng, nl, wup, wout, lg, lb, alpha, tm=256):
    m = x.shape[0]
    row = lambda w: pl.BlockSpec((tm, w), lambda i: (i, 0))
    return pl.pallas_call(
        functools.partial(_merge_body, alpha=alpha),
        grid=(m // tm,),
        in_specs=[row(D_MODEL), row(MIX_W), row(MIX_W), row(MIX_W),
                  _const_spec(wzr.shape), _const_spec(wmg.shape), _const_spec(ng.shape), _const_spec(nl.shape),
                  _const_spec(wup.shape), _const_spec(wout.shape), _const_spec(lg.shape), _const_spec(lb.shape)],
        out_specs=row(D_MODEL),
        out_shape=jax.ShapeDtypeStruct((m, D_MODEL), F32),
        compiler_params=_params("parallel"),
        name="mixer_merge",
    )(x, y_nsa, o_gdn, o_gla, wzr, wmg, ng, nl, wup, wout, lg, lb)


def _xattn_body(x_ref, kv_ref, wq_ref, wo_ref, lg_ref, lb_ref, o_ref, *, alpha):
    x = x_ref[...]
    q = _dot(x.astype(BF16), wq_ref[...]).astype(BF16)
    scale = XA_HD ** -0.5
    outs = []
    for h in range(XA_HEADS):
        hs = slice(h * XA_HD, (h + 1) * XA_HD)
        s = _dot_nt(q[:, hs], kv_ref[:, hs]) * scale
        m = jnp.max(s, -1, keepdims=True)
        e = jnp.exp(s - m)
        p = e / jnp.sum(e, -1, keepdims=True)
        outs.append(_dot(p.astype(BF16), kv_ref[:, D_MODEL + h * XA_HD:D_MODEL + (h + 1) * XA_HD]))
    o = jnp.concatenate(outs, axis=1)
    h_out = _dot(o.astype(BF16), wo_ref[...])
    o_ref[...] = _layer_norm(alpha * x + h_out, lg_ref[...], lb_ref[...])


def _xattn(x, kvm, wq, wo, lg, lb, alpha, batch, seq, tm=512):
    nt = seq // tm
    mt = kvm.shape[0] // batch
    cs = lambda shape: pl.BlockSpec(shape, lambda b, i: (0,) * len(shape), pipeline_mode=pl.Buffered(1))
    return pl.pallas_call(
        functools.partial(_xattn_body, alpha=alpha),
        grid=(batch, nt),
        in_specs=[pl.BlockSpec((tm, D_MODEL), lambda b, i: (b * nt + i, 0)),
                  pl.BlockSpec((mt, 2 * D_MODEL), lambda b, i: (b, 0)),
                  cs(wq.shape), cs(wo.shape), cs(lg.shape), cs(lb.shape)],
        out_specs=pl.BlockSpec((tm, D_MODEL), lambda b, i: (b * nt + i, 0)),
        out_shape=jax.ShapeDtypeStruct((batch * seq, D_MODEL), F32),
        compiler_params=_params("parallel", "parallel"),
        name="mem_xattn",
    )(x, kvm, wq, wo, lg, lb)


def _mlp_body(x_ref, w1_ref, w2_ref, lg_ref, lb_ref, o_ref, *, alpha):
    x = x_ref[...]
    h = jnp.maximum(_dot(x.astype(BF16), w1_ref[...]), 0.0)
    h = (h * h).astype(BF16)
    o_ref[...] = _layer_norm(alpha * x + _dot(h, w2_ref[...]), lg_ref[...], lb_ref[...])


def _mlp(x, w1, w2, lg, lb, alpha, tm=512):
    m = x.shape[0]
    return pl.pallas_call(
        functools.partial(_mlp_body, alpha=alpha),
        grid=(m // tm,),
        in_specs=[pl.BlockSpec((tm, D_MODEL), lambda i: (i, 0)),
                  _const_spec(w1.shape), _const_spec(w2.shape), _const_spec(lg.shape), _const_spec(lb.shape)],
        out_specs=pl.BlockSpec((tm, D_MODEL), lambda i: (i, 0)),
        out_shape=jax.ShapeDtypeStruct((m, D_MODEL), F32),
        compiler_params=_params("parallel"),
        name="sq_relu_mlp",
    )(x, w1, w2, lg, lb)


def _pad_cols(w, width):
    return jnp.pad(w, ((0, 0), (0, width - w.shape[1])))


def _nsa_weights(w_in):
    def kv(kind, g):
        o = O_NSA_KV + (kind * NSA_GROUPS + g) * NSA_HD
        return w_in[:, o:o + NSA_HD]
    cols = [w_in[:, O_NSA_Q:O_NSA_Q + NSA_HEADS * NSA_HD]]
    for g in range(NSA_GROUPS):
        cols += [kv(2, g), kv(4, g), kv(3, g), kv(5, g)]
    cols += [kv(0, 0), kv(0, 1), kv(1, 0), kv(1, 1)]
    main = jnp.concatenate(cols, axis=1).astype(BF16)
    gw = NSA_HPG * 3
    gates = [_pad_cols(w_in[:, O_NSA_G + g * gw:O_NSA_G + (g + 1) * gw], LANE) for g in range(NSA_GROUPS)]
    return main, jnp.concatenate(gates, axis=1).astype(BF16)


def _gla_weights(w_in, w_lr, b_lr):
    def pad_heads(w):
        lead = w.shape[0]
        w = w.reshape(lead, GLA_HEADS, GLA_DK)
        return jnp.pad(w, ((0, 0), (0, 0), (0, GLA_DV - GLA_DK))).reshape(lead, GLA_HEADS * GLA_DV)
    cols = [pad_heads(w_in[:, O_GLA_Q:O_GLA_Q + GLA_HEADS * GLA_DK]),
            pad_heads(w_in[:, O_GLA_K:O_GLA_K + GLA_HEADS * GLA_DK]),
            w_in[:, O_GLA_V:O_GLA_V + GLA_HEADS * GLA_DV],
            _pad_cols(w_in[:, O_GLA_LR:O_GLA_LR + GLA_RANK], LANE)]
    wlr = jnp.pad(pad_heads(w_lr), ((0, LANE - GLA_RANK), (0, 0)))
    return jnp.concatenate(cols, axis=1).astype(BF16), wlr, pad_heads(b_lr[None, :])


def _overlap_matrix(seq):
    nc = seq // CMP_STRIDE
    ns = seq // SEL_BLK
    c_start = np.arange(nc)[:, None] * CMP_STRIDE
    s_start = np.arange(ns)[None, :] * SEL_BLK
    return jnp.asarray((c_start < s_start + SEL_BLK) & (c_start + CMP_LEN > s_start), BF16)


def _layer(x, kvm_src, p, l, batch, seq, alpha):
    bt = batch * seq
    w_in = p["w_in"][l]

    w_main, w_gates = _nsa_weights(w_in)
    main = _mm(x, w_main, BF16)
    gate_logits = _mm(x, w_gates, F32)
    nc = seq // CMP_STRIDE
    cmp_in = main[:, 2 * MIX_W:].reshape(bt, 2 * NSA_GROUPS, NSA_HD).transpose(1, 0, 2)
    cmp_in = cmp_in.reshape(2 * NSA_GROUPS, batch * nc, CMP_STRIDE * NSA_HD)
    half = CMP_STRIDE * NSA_HD
    w1 = p["cmp_w1"][l].reshape(2, 2, half, CMP_HID).astype(BF16)
    pe = jnp.broadcast_to(p["cmp_pe"][l].reshape(2, 2, 1, half), (2, 2, SUBLANE, half)).astype(BF16)
    cmp_tok = _compress(cmp_in, w1, p["cmp_b1"][l][:, None, :], pe, p["cmp_w2"][l].astype(BF16), batch)
    y_nsa = _nsa_attention(main, gate_logits, cmp_tok, _overlap_matrix(seq), batch, seq)

    w_gdn = jnp.concatenate([w_in[:, O_GDN_QKV:O_GDN_QKV + 3 * MIX_W],
                             _pad_cols(w_in[:, O_GDN_A:O_GDN_A + 2 * GDN_HEADS], LANE)], axis=1).astype(BF16)
    gdn_in = _mm(x, w_gdn, F32)
    alog = _pad_cols(p["gdn_a_log"][l][None, :], LANE)
    dtb = _pad_cols(p["gdn_dt_bias"][l][None, :], LANE)
    o_gdn = _gdn(gdn_in[:, :3 * MIX_W], gdn_in[:, 3 * MIX_W:], p["gdn_conv"][l], alog, dtb, batch, seq)

    w_gla, wlr, blr = _gla_weights(w_in, p["gla_w_lr"][l], p["gla_b_lr"][l])
    gla_in = _mm(x, w_gla, F32)
    ops, lmask = _gla_constants()
    o_gla = _gla(gla_in, wlr, blr, jnp.asarray(ops), jnp.asarray(lmask), batch, seq)

    wzr = jnp.concatenate([w_in[:, O_GDN_Z:O_GDN_Z + MIX_W], w_in[:, O_GLA_R:O_GLA_R + MIX_W]], axis=1).astype(BF16)
    wmg = w_in[:, O_MERGE:O_MERGE + N_BRANCH * D_MODEL].astype(BF16)
    tile4 = lambda v: jnp.tile(v[None, :], (1, 1))
    x = _merge(x, y_nsa, o_gdn, o_gla, wzr, wmg, tile4(p["gdn_norm"][l]), tile4(p["gla_norm"][l]),
               p["w_up"][l].astype(BF16), p["w_out"][l].astype(BF16),
               p["ln_g"][l, 0][None, :], p["ln_b"][l, 0][None, :], alpha)

    wkv = jnp.concatenate([p["xa_wk"][l], p["xa_wv"][l]], axis=1).astype(BF16)
    kvm = _mm(kvm_src, wkv, BF16)
    x = _xattn(x, kvm, p["xa_wq"][l].astype(BF16), p["xa_wo"][l].astype(BF16),
               p["ln_g"][l, 1][None, :], p["ln_b"][l, 1][None, :], alpha, batch, seq)

    return _mlp(x, p["mlp_w1"][l].astype(BF16), p["mlp_w2"][l].astype(BF16),
                p["ln_g"][l, 2][None, :], p["ln_b"][l, 2][None, :], alpha)


def kernel(x, mem, w_in, cmp_pe, cmp_w1, cmp_b1, cmp_w2, gdn_conv, gdn_a_log, gdn_dt_bias, gdn_norm, gla_w_lr,
           gla_b_lr, gla_norm, w_up, w_out, xa_wq, xa_wk, xa_wv, xa_wo, mlp_w1, mlp_w2, ln_g, ln_b):
    batch, seq, d = x.shape
    depth = w_in.shape[0]
    alpha = (2.0 * depth) ** 0.25
    p = dict(w_in=w_in, cmp_pe=cmp_pe, cmp_w1=cmp_w1, cmp_b1=cmp_b1, cmp_w2=cmp_w2, gdn_conv=gdn_conv,
             gdn_a_log=gdn_a_log, gdn_dt_bias=gdn_dt_bias, gdn_norm=gdn_norm, gla_w_lr=gla_w_lr,
             gla_b_lr=gla_b_lr, gla_norm=gla_norm, w_up=w_up, w_out=w_out, xa_wq=xa_wq, xa_wk=xa_wk,
             xa_wv=xa_wv, xa_wo=xa_wo, mlp_w1=mlp_w1, mlp_w2=mlp_w2, ln_g=ln_g, ln_b=ln_b)
    h = x.reshape(batch * seq, d)
    mem2 = mem.reshape(batch * mem.shape[1], d)
    for l in range(depth):
        h = _layer(h, mem2, p, l, batch, seq, alpha)
    return h.reshape(batch, seq, d)
```

```python
import functools

import numpy as np
import jax
import jax.numpy as jnp
from jax import lax
from jax.experimental import pallas as pl
from jax.experimental.pallas import tpu as pltpu

F32 = jnp.float32
BF16 = jnp.bfloat16

D_MODEL = 1024
MIX_W = D_MODEL // 2
N_BRANCH = 3
NSA_HEADS = 8
NSA_HD = MIX_W // NSA_HEADS
NSA_GROUPS = 2
NSA_HPG = NSA_HEADS // NSA_GROUPS
CMP_LEN = 32
CMP_STRIDE = 16
CMP_HID = D_MODEL // 4
SEL_BLK = 64
SEL_TOPK = 16
WIN = 512
Q_BLK = 128
GDN_HEADS = 4
GDN_HD = MIX_W // GDN_HEADS
CONV_K = 4
CHUNK = 64
GLA_HEADS = 4
GLA_DK = MIX_W // (2 * GLA_HEADS)
GLA_DV = MIX_W // GLA_HEADS
GLA_RANK = 16
GLA_TAU = 16.0
XA_HEADS = 4
XA_HD = D_MODEL // XA_HEADS
D_FF = 4 * D_MODEL
LN_EPS = 1e-5
NORM_EPS = 1e-6
NEG_INF = -1e30
FORCE_BONUS = 1e4

LANE = 128
SUBLANE = 8
VMEM_LIMIT_BYTES = 48 * 1024 * 1024
SEL_TILE = 512
NSA_ROWS = 16
REMOVED = -3e38
GLA_LEVELS = (32, 16, 8, 4, 2, 1)
Q_BLK_LOG2 = Q_BLK.bit_length() - 1
SEL_BLK_LOG2 = SEL_BLK.bit_length() - 1
CHUNK_LOG2 = CHUNK.bit_length() - 1

_IN_SIZES = (NSA_HEADS * NSA_HD, 6 * NSA_GROUPS * NSA_HD, NSA_HEADS * 3, 3 * MIX_W, GDN_HEADS, GDN_HEADS,
             MIX_W, GLA_HEADS * GLA_DK, GLA_HEADS * GLA_DK, GLA_HEADS * GLA_DV, GLA_RANK, MIX_W,
             N_BRANCH * D_MODEL)
_OFF = tuple(int(v) for v in np.cumsum((0,) + _IN_SIZES))
(O_NSA_Q, O_NSA_KV, O_NSA_G, O_GDN_QKV, O_GDN_A, O_GDN_B, O_GDN_Z, O_GLA_Q, O_GLA_K, O_GLA_V, O_GLA_LR,
 O_GLA_R, O_MERGE, _) = _OFF


def _params(*sem):
    return pltpu.CompilerParams(dimension_semantics=sem, vmem_limit_bytes=VMEM_LIMIT_BYTES)


def _dot(a, b):
    return jnp.dot(a, b, preferred_element_type=F32)


def _dot_nt(a, b):
    return lax.dot_general(a, b, (((1,), (1,)), ((), ())), preferred_element_type=F32)


def _dot_tn(a, b):
    return lax.dot_general(a, b, (((0,), (0,)), ((), ())), preferred_element_type=F32)


def _sigmoid(x):
    return 1.0 / (1.0 + jnp.exp(-x))


def _layer_norm(v, g, b):
    mu = jnp.mean(v, -1, keepdims=True)
    c = v - mu
    var = jnp.mean(c * c, -1, keepdims=True)
    return c * lax.rsqrt(var + LN_EPS) * g + b


def _split2(a):
    hi = a.astype(BF16)
    return hi, (a - hi.astype(F32)).astype(BF16)


def _dot_split(a, b):
    return _dot(a[0], b[0]) + (_dot(a[0], b[1]) + _dot(a[1], b[0]))


def _dot_exact_lhs(op, x):
    x1 = x.astype(BF16)
    r1 = x - x1.astype(F32)
    x2 = r1.astype(BF16)
    x3 = (r1 - x2.astype(F32)).astype(BF16)
    return _dot(op, x1) + (_dot(op, x2) + _dot(op, x3))


def _mm_body(x_ref, w_ref, o_ref):
    o_ref[...] = _dot(x_ref[...].astype(BF16), w_ref[...]).astype(o_ref.dtype)


def _mm(x, w, out_dtype, tm=512):
    m, k = x.shape
    n = w.shape[1]
    tm = min(tm, m)
    return pl.pallas_call(
        _mm_body,
        grid=(m // tm,),
        in_specs=[pl.BlockSpec((tm, k), lambda i: (i, 0)),
                  pl.BlockSpec((k, n), lambda i: (0, 0))],
        out_specs=pl.BlockSpec((tm, n), lambda i: (i, 0)),
        out_shape=jax.ShapeDtypeStruct((m, n), out_dtype),
        compiler_params=_params("parallel"),
        name="proj_mm",
    )(x, w)


def _compress_body(ch_ref, w1_ref, b1_ref, pe_ref, w2_ref, o_ref):
    nc = ch_ref.shape[0]
    ch = ch_ref[...]
    top = _dot(ch, w1_ref[0])
    bot = _dot(ch, w1_ref[1])
    bot_next = pltpu.roll(bot, nc - 1, axis=0)
    pe_term = _dot(pe_ref[0], w1_ref[0]) + _dot(pe_ref[1], w1_ref[1])
    h = top + bot_next + pe_term[0:1] + b1_ref[...]
    h = jax.nn.gelu(h, approximate=True)
    o_ref[...] = _dot(h.astype(BF16), w2_ref[...]).astype(o_ref.dtype)


def _compress(ch, w1, b1, pe, w2, batch):
    nc = ch.shape[1] // batch
    half = CMP_STRIDE * NSA_HD
    return pl.pallas_call(
        _compress_body,
        grid=(2, NSA_GROUPS, batch),
        in_specs=[pl.BlockSpec((None, nc, half), lambda t, g, b: (t * NSA_GROUPS + g, b, 0)),
                  pl.BlockSpec((None, 2, half, CMP_HID), lambda t, g, b: (t, 0, 0, 0)),
                  pl.BlockSpec((None, 1, CMP_HID), lambda t, g, b: (t, 0, 0)),
                  pl.BlockSpec((None, 2, SUBLANE, half), lambda t, g, b: (t, 0, 0, 0)),
                  pl.BlockSpec((None, CMP_HID, NSA_HD), lambda t, g, b: (t, 0, 0))],
        out_specs=pl.BlockSpec((None, nc, NSA_HD), lambda t, g, b: (t * NSA_GROUPS + g, b, 0)),
        out_shape=jax.ShapeDtypeStruct((2 * NSA_GROUPS, batch * nc, NSA_HD), BF16),
        compiler_params=_params("parallel", "parallel", "parallel"),
        name="nsa_compress",
    )(ch, w1, b1, pe, w2)


def _nsa_body(q_ref, kv_ref, kc_ref, vc_ref, gl_ref, ov_ref, o_ref, m_scr, l_scr, acc_scr, tiles_ref, *, seq):
    g = pl.program_id(1)
    qb = pl.program_id(2)
    nc = seq // CMP_STRIDE
    ns = seq // SEL_BLK
    qs = qb * Q_BLK
    scale = NSA_HD ** -0.5
    n_chunks = Q_BLK // NSA_ROWS

    q4 = q_ref[...] * jnp.asarray(scale, q_ref.dtype)
    qst = jnp.concatenate([q4[:, h * NSA_HD:(h + 1) * NSA_HD] for h in range(NSA_HPG)], axis=0)
    slopes = [jnp.where(g == 0, 2.0 ** -(h + 1), 2.0 ** -(NSA_HPG + h + 1)).astype(F32) for h in range(NSA_HPG)]
    tq1 = qs + lax.broadcasted_iota(jnp.int32, (Q_BLK, 1), 0)

    def chunk_rows(h, rc):
        return slice(h * Q_BLK + rc * NSA_ROWS, h * Q_BLK + (rc + 1) * NSA_ROWS)

    order = [(h, rc) for h in range(NSA_HPG) for rc in range(n_chunks)]

    def stacked(parts):
        return jnp.concatenate([parts[i] for i in order], axis=0)

    def lane_blocks(x, op):
        out = x[:, 0:LANE]
        for j in range(1, x.shape[1] // LANE):
            out = op(out, x[:, j * LANE:(j + 1) * LANE])
        return out

    def lane_tiled(x, width):
        return jnp.concatenate([x] * (width // LANE), axis=1)

    def exp_pass(logit_parts, m_all, masks=None):
        width = logit_parts[order[0]].shape[1]
        m_lanes = jnp.broadcast_to(m_all, (m_all.shape[0], LANE))
        e_parts, sums = {}, {}
        for h, rc in order:
            e = jnp.exp(logit_parts[h, rc] - lane_tiled(m_lanes[chunk_rows(h, rc)], width))
            if masks is not None:
                e = jnp.where(masks[rc], e, 0.0)
            e_parts[h, rc] = e
            sums[h, rc] = lane_blocks(e, jnp.add)
        return e_parts, jnp.sum(stacked(sums), -1, keepdims=True)

    def masked_logits(s_all, key_pos, is_valid):
        parts, maxes, masks = {}, {}, {}
        for rc in range(n_chunks):
            dist = tq1[rc * NSA_ROWS:(rc + 1) * NSA_ROWS] - key_pos
            masks[rc] = is_valid(dist)
            d_f = dist.astype(F32)
            for h in range(NSA_HPG):
                s = jnp.where(masks[rc], s_all[chunk_rows(h, rc)] - slopes[h] * d_f, NEG_INF)
                parts[h, rc] = s
                maxes[h, rc] = lane_blocks(s, jnp.maximum)
        return parts, masks, jnp.max(stacked(maxes), -1, keepdims=True)

    c_end = lax.broadcasted_iota(jnp.int32, (1, nc), 1) * CMP_STRIDE + (CMP_LEN - 1)
    logit_c, mask_c, m_c = masked_logits(_dot_nt(qst, kc_ref[...]), c_end, lambda d: d >= 0)
    e_c, l_c = exp_pass(logit_c, m_c, mask_c)
    r_c = 1.0 / jnp.maximum(l_c, 1e-30)
    o_cmp = _dot(stacked({i: e_c[i].astype(BF16) for i in order}), vc_ref[...]) * r_c

    r_lanes = jnp.broadcast_to(r_c, (r_c.shape[0], LANE))
    p_hi, p_lo = [], []
    for rc in range(n_chunks):
        p_sum = None
        for h in range(NSA_HPG):
            p = e_c[h, rc] * lane_tiled(r_lanes[chunk_rows(h, rc)], nc)
            p_sum = p if p_sum is None else p_sum + p
        hi = p_sum.astype(BF16)
        p_hi.append(hi)
        p_lo.append((p_sum - hi.astype(F32)).astype(BF16))
    p_hi = jnp.concatenate(p_hi, axis=0)
    p_lo = jnp.concatenate(p_lo, axis=0)

    imp_t = (_dot(p_hi, ov_ref[...]) + _dot(p_lo, ov_ref[...])).T
    blk = lax.broadcasted_iota(jnp.int32, (ns, Q_BLK), 0)
    tq_row = qs + lax.broadcasted_iota(jnp.int32, (1, Q_BLK), 1)
    cur = tq_row >> SEL_BLK_LOG2
    forced = (blk == 0) | (blk == cur) | (blk == cur - 1)
    valid = blk * SEL_BLK <= tq_row
    score = jnp.where(valid, imp_t + jnp.where(forced, FORCE_BONUS, 0.0), NEG_INF)

    def pick(_, carry):
        sc, sel = carry
        m = jnp.max(sc, 0, keepdims=True)
        idx = jnp.min(jnp.where(sc == m, blk, ns), 0, keepdims=True)
        hit = blk == idx
        return jnp.where(hit, REMOVED, sc), jnp.where(hit, 1.0, sel)

    _, sel_t = lax.fori_loop(0, min(SEL_TOPK, ns), pick, (score, jnp.zeros((ns, Q_BLK), F32)))
    sel = sel_t.T
    sel_bias = ((sel - 1.0) * -NEG_INF).astype(BF16)
    blk_any = jnp.max(sel, 0, keepdims=True)

    tile_blks = SEL_TILE // SEL_BLK
    blk_row = lax.broadcasted_iota(jnp.int32, (ns, SEL_TILE), 0)
    blk_of_key = lax.broadcasted_iota(jnp.int32, (ns, SEL_TILE), 1) >> SEL_BLK_LOG2
    key_in_tile = lax.broadcasted_iota(jnp.int32, (1, SEL_TILE), 1)
    m_scr[...] = jnp.full(m_scr.shape, NEG_INF, F32)
    l_scr[...] = jnp.zeros(l_scr.shape, F32)
    acc_scr[...] = jnp.zeros(acc_scr.shape, F32)

    tile_of_blk = lax.broadcasted_iota(jnp.int32, (ns, LANE), 0) >> (SEL_TILE.bit_length() - 1 - SEL_BLK_LOG2)
    in_tile = jnp.where(tile_of_blk == lax.broadcasted_iota(jnp.int32, (ns, LANE), 1), 1.0, 0.0).astype(BF16)
    tile_cnt = _dot(jnp.broadcast_to(blk_any, (SUBLANE, ns)).astype(BF16), in_tile)
    n_visible = (qs + Q_BLK + SEL_TILE - 1) // SEL_TILE
    n_active = jnp.int32(0)
    for kt in range(ns // tile_blks):
        tiles_ref[n_active] = kt
        n_active = n_active + ((tile_cnt[0, kt] > 0.0) & (kt < n_visible)).astype(jnp.int32)

    def sel_tile(j, carry):
        kt = tiles_ref[j]
        k0 = pl.multiple_of(kt * SEL_TILE, SEL_TILE)
        st_all = _dot_nt(qst, kv_ref[pl.ds(k0, SEL_TILE), 0:NSA_HD])
        expand = jnp.where(blk_row == blk_of_key + kt * tile_blks, 1.0, 0.0).astype(BF16)
        key_pos = k0 + key_in_tile
        bias_all = jnp.where(tq1 - key_pos >= 0, _dot(sel_bias, expand), NEG_INF)
        logits, maxes = {}, {}
        for rc in range(n_chunks):
            qr = slice(rc * NSA_ROWS, (rc + 1) * NSA_ROWS)
            d_f = (tq1[qr] - key_pos).astype(F32)
            for h in range(NSA_HPG):
                s = st_all[chunk_rows(h, rc)] + (bias_all[qr] - slopes[h] * d_f)
                logits[h, rc] = s
                maxes[h, rc] = lane_blocks(s, jnp.maximum)
        m_old = m_scr[...]
        m_new = jnp.maximum(m_old, jnp.max(stacked(maxes), -1, keepdims=True))
        p_parts, l_tile = exp_pass(logits, m_new)
        alpha = jnp.exp(m_old - m_new)
        m_scr[...] = m_new
        l_scr[...] = alpha * l_scr[...] + l_tile
        v = kv_ref[pl.ds(k0, SEL_TILE), 2 * NSA_HD:3 * NSA_HD]
        p_all = stacked({i: p_parts[i].astype(BF16) for i in order})
        acc_scr[...] = alpha * acc_scr[...] + _dot(p_all, v)
        return carry

    lax.fori_loop(0, n_active, sel_tile, 0)
    o_slc = acc_scr[...] * (1.0 / jnp.maximum(l_scr[...], 1e-30))

    wlen = WIN + Q_BLK
    w0 = pl.multiple_of(jnp.maximum(qs - WIN, 0), Q_BLK)
    pos_w = w0 + lax.broadcasted_iota(jnp.int32, (1, wlen), 1)
    logit_w, _, m_w = masked_logits(_dot_nt(qst, kv_ref[pl.ds(w0, wlen), NSA_HD:2 * NSA_HD]), pos_w,
                                    lambda d: (d >= 0) & (d < WIN))
    e_w, l_w = exp_pass(logit_w, m_w)
    o_win = (_dot(stacked({i: e_w[i].astype(BF16) for i in order}), kv_ref[pl.ds(w0, wlen), 3 * NSA_HD:4 * NSA_HD])
             * (1.0 / jnp.maximum(l_w, 1e-30)))

    gates = _sigmoid(gl_ref[...])
    outs = []
    for h in range(NSA_HPG):
        r = slice(h * Q_BLK, (h + 1) * Q_BLK)
        outs.append(gates[:, 3 * h:3 * h + 1] * o_cmp[r] + gates[:, 3 * h + 1:3 * h + 2] * o_slc[r]
                    + gates[:, 3 * h + 2:3 * h + 3] * o_win[r])
    o_ref[...] = jnp.concatenate(outs, axis=1).astype(o_ref.dtype)


def _nsa_attention(main, gate_logits, cmp_tok, overlap, batch, seq):
    nq = seq // Q_BLK
    nc = seq // CMP_STRIDE
    ns = seq // SEL_BLK
    gw = NSA_HPG * NSA_HD
    hq = NSA_HPG * Q_BLK
    wlen = WIN + Q_BLK
    return pl.pallas_call(
        functools.partial(_nsa_body, seq=seq),
        grid=(batch, NSA_GROUPS, nq),
        in_specs=[pl.BlockSpec((Q_BLK, gw), lambda b, g, i: (b * nq + i, g)),
                  pl.BlockSpec((seq, gw), lambda b, g, i: (b, NSA_GROUPS + g)),
                  pl.BlockSpec((None, nc, NSA_HD), lambda b, g, i: (g, b, 0)),
                  pl.BlockSpec((None, nc, NSA_HD), lambda b, g, i: (NSA_GROUPS + g, b, 0)),
                  pl.BlockSpec((Q_BLK, LANE), lambda b, g, i: (b * nq + i, g)),
                  pl.BlockSpec((nc, ns), lambda b, g, i: (0, 0))],
        out_specs=pl.BlockSpec((Q_BLK, gw), lambda b, g, i: (b * nq + i, g)),
        out_shape=jax.ShapeDtypeStruct((batch * seq, MIX_W), BF16),
        scratch_shapes=[pltpu.VMEM((hq, 1), F32), pltpu.VMEM((hq, 1), F32), pltpu.VMEM((hq, NSA_HD), F32),
                        pltpu.SMEM((seq // SEL_TILE,), jnp.int32)],
        compiler_params=_params("parallel", "parallel", "arbitrary"),
        name="nsa_attention",
    )(main, main, cmp_tok, cmp_tok, gate_logits, overlap)


def _unit_lower_inverse(a):
    n = a.shape[0]
    eye = (lax.broadcasted_iota(jnp.int32, (n, n), 0) == lax.broadcasted_iota(jnp.int32, (n, n), 1)).astype(F32)
    x = _split2(eye - a)
    p = _split2(a)
    power = 1
    while 2 * power < CHUNK:
        p = _split2(_dot_split(p, p))
        x = _split2((x[0].astype(F32) + x[1].astype(F32)) + _dot_split(x, p))
        power *= 2
    return x


def _gdn_body(raw_ref, ab_ref, conv_ref, alog_ref, dtb_ref, o_ref, s_ref, tail_ref, buf_ref, *, tb):
    @pl.when(pl.program_id(1) == 0)
    def _():
        s_ref[...] = jnp.zeros_like(s_ref)
        tail_ref[...] = jnp.zeros_like(tail_ref)

    raw = raw_ref[...]
    buf_ref[0:SUBLANE] = tail_ref[...]
    buf_ref[SUBLANE:SUBLANE + tb] = raw
    tail_ref[...] = raw[tb - SUBLANE:tb]
    x = jnp.zeros_like(raw)
    for k in range(CONV_K):
        x = x + buf_ref[pl.ds(SUBLANE - (CONV_K - 1) + k, tb)] * conv_ref[k:k + 1]
    x = x * _sigmoid(x)

    ab = ab_ref[...]
    zed = ab + dtb_ref[...]
    softplus = jnp.maximum(zed, 0.0) + jnp.log1p(jnp.exp(-jnp.abs(zed)))
    g_all = -jnp.exp(alog_ref[...]) * softplus
    beta_all = _sigmoid(ab)

    c2 = 2 * CHUNK
    ri = lax.broadcasted_iota(jnp.int32, (c2, c2), 0)
    ci = lax.broadcasted_iota(jnp.int32, (c2, c2), 1)
    tri2 = jnp.where((ri >= ci) & ((ri >> CHUNK_LOG2) == (ci >> CHUNK_LOG2)), 1.0, 0.0).astype(BF16)
    hc = GDN_HEADS * CHUNK
    rs = lax.broadcasted_iota(jnp.int32, (hc, hc), 0)
    cs = lax.broadcasted_iota(jnp.int32, (hc, hc), 1)
    same_head = (rs >> CHUNK_LOG2) == (cs >> CHUNK_LOG2)
    tri = same_head & (rs >= cs)
    strict = same_head & (rs > cs)
    qscale = GDN_HD ** -0.5

    def l2n(v):
        return v * lax.rsqrt(jnp.sum(v * v, -1, keepdims=True) + NORM_EPS)

    def stack(fn):
        return jnp.concatenate([fn(h) for h in range(GDN_HEADS)], axis=0)

    for pair in range(tb // c2):
        pr = slice(pair * c2, (pair + 1) * c2)
        gc2 = _dot_exact_lhs(tri2, g_all[pr])
        gc2t = gc2.T
        for half in range(2):
            rows = slice(pair * c2 + half * CHUNK, pair * c2 + (half + 1) * CHUNK)
            hr = slice(half * CHUNK, (half + 1) * CHUNK)
            gcol = stack(lambda h: gc2[hr, h:h + 1])
            grow = jnp.concatenate([gc2t[h:h + 1, hr] for h in range(GDN_HEADS)], axis=1)
            g_last = stack(lambda h: jnp.broadcast_to(gc2[hr, h:h + 1][CHUNK - 1:CHUNK], (CHUNK, 1)))
            beta = stack(lambda h: beta_all[rows, GDN_HEADS + h:GDN_HEADS + h + 1])
            qs_ = stack(lambda h: l2n(x[rows, h * GDN_HD:(h + 1) * GDN_HD])) * qscale
            ks_ = stack(lambda h: l2n(x[rows, MIX_W + h * GDN_HD:MIX_W + (h + 1) * GDN_HD]))
            vs_ = stack(lambda h: x[rows, 2 * MIX_W + h * GDN_HD:2 * MIX_W + (h + 1) * GDN_HD])
            decay = jnp.where(tri, jnp.exp(jnp.where(tri, gcol - grow, 0.0)), 0.0)
            kb = ks_ * beta
            a_mat = jnp.where(strict, _dot_nt(kb, ks_) * decay, 0.0)
            t_inv = _unit_lower_inverse(a_mat)
            egc = jnp.exp(gcol)
            sol = _dot_split(t_inv, _split2(jnp.concatenate([vs_ * beta, kb * egc], axis=1)))
            qk = jnp.where(tri, _dot_nt(qs_, ks_) * decay, 0.0)
            q_dec = qs_ * egc
            k_tail = ks_ * jnp.exp(g_last - gcol)
            a_last = jnp.exp(g_last)
            states = [s_ref[h] for h in range(GDN_HEADS)]
            hrows = [slice(h * CHUNK, (h + 1) * CHUNK) for h in range(GDN_HEADS)]
            v_new = stack(lambda h: sol[hrows[h], :GDN_HD] - _dot(sol[hrows[h], GDN_HD:], states[h]))
            o = _dot(qk, v_new) + stack(lambda h: _dot(q_dec[hrows[h]], states[h]))
            for h in range(GDN_HEADS):
                s_ref[h] = (states[h] * a_last[hrows[h]][0:1] + _dot_tn(k_tail[hrows[h]], v_new[hrows[h]]))
                o_ref[rows, h * GDN_HD:(h + 1) * GDN_HD] = o[hrows[h]]


def _gdn(raw, ab, conv, alog, dtb, batch, seq, tb=256):
    nt = seq // tb
    return pl.pallas_call(
        functools.partial(_gdn_body, tb=tb),
        grid=(batch, nt),
        in_specs=[pl.BlockSpec((tb, 3 * MIX_W), lambda b, i: (b * nt + i, 0)),
                  pl.BlockSpec((tb, LANE), lambda b, i: (b * nt + i, 0)),
                  pl.BlockSpec((CONV_K, 3 * MIX_W), lambda b, i: (0, 0)),
                  pl.BlockSpec((1, LANE), lambda b, i: (0, 0)),
                  pl.BlockSpec((1, LANE), lambda b, i: (0, 0))],
        out_specs=pl.BlockSpec((tb, MIX_W), lambda b, i: (b * nt + i, 0)),
        out_shape=jax.ShapeDtypeStruct((batch * seq, MIX_W), F32),
        scratch_shapes=[pltpu.VMEM((GDN_HEADS, GDN_HD, GDN_HD), F32),
                        pltpu.VMEM((SUBLANE, 3 * MIX_W), F32),
                        pltpu.VMEM((SUBLANE + tb, 3 * MIX_W), F32)],
        compiler_params=_params("parallel", "arbitrary"),
        name="gated_delta_net",
    )(raw, ab, conv, alog, dtb)


def _gla_constants():
    c = CHUNK
    t = np.arange(c)[:, None]
    s = np.arange(c)[None, :]
    ops = [(s <= t), (s > t)]
    masks = []
    for m in GLA_LEVELS:
        r = (t // (2 * m)) * (2 * m) + m
        upper = (t % (2 * m)) >= m
        ops.append(np.where(upper, (s > r) & (s <= t), (s > t) & (s <= r)))
        i, j = t, s
        masks.append((i // (2 * m) == j // (2 * m)) & ((i % (2 * m)) >= m) & ((j % (2 * m)) < m))
    return (np.concatenate(ops, 0).astype(np.float32), np.stack(masks).astype(np.float32))


def _gla_body(x_ref, wlr_ref, blr_ref, ops_ref, lm_ref, o_ref, s_ref, *, tb):
    @pl.when(pl.program_id(1) == 0)
    def _():
        s_ref[...] = jnp.zeros_like(s_ref)

    hw = GLA_DV
    kw = GLA_HEADS * hw
    r1 = lax.broadcasted_iota(jnp.int32, (CHUNK, CHUNK), 0)
    c1 = lax.broadcasted_iota(jnp.int32, (CHUNK, CHUNK), 1)
    eye = r1 == c1
    qscale = GLA_DK ** -0.5
    for c in range(tb // CHUNK):
        rows = slice(c * CHUNK, (c + 1) * CHUNK)
        lr = x_ref[rows, 3 * kw:3 * kw + LANE]
        z = _dot(lr, wlr_ref[...]) + blr_ref[...]
        log_a = (jnp.minimum(z, 0.0) - jnp.log1p(jnp.exp(-jnp.abs(z)))) / GLA_TAU
        e_all = jnp.exp(_dot_exact_lhs(ops_ref[...], log_a))
        for h in range(GLA_HEADS):
            hs = slice(h * hw, (h + 1) * hw)
            qh = x_ref[rows, hs] * qscale
            kh = x_ref[rows, kw + h * hw:kw + (h + 1) * hw]
            vh = x_ref[rows, 2 * kw + h * hw:2 * kw + (h + 1) * hw]
            scores = jnp.where(eye, _dot_nt(qh, kh), 0.0)
            for lvl in range(len(GLA_LEVELS)):
                xl = e_all[(2 + lvl) * CHUNK:(3 + lvl) * CHUNK, hs]
                scores = scores + lm_ref[lvl] * _dot_nt(qh * xl, kh * xl)
            e_cum = e_all[0:CHUNK, hs]
            e_tail = e_all[CHUNK:2 * CHUNK, hs]
            state_t = s_ref[h]
            o = _dot_nt(qh * e_cum, state_t) + _dot(scores, vh)
            s_ref[h] = state_t * e_cum[CHUNK - 1:CHUNK] + _dot_tn(vh, kh * e_tail)
            o_ref[rows, hs] = o


def _gla(x, wlr, blr, ops, lmask, batch, seq, tb=256):
    nt = seq // tb
    width = x.shape[1]
    return pl.pallas_call(
        functools.partial(_gla_body, tb=tb),
        grid=(batch, nt),
        in_specs=[pl.BlockSpec((tb, width), lambda b, i: (b * nt + i, 0)),
                  pl.BlockSpec(wlr.shape, lambda b, i: (0, 0)),
                  pl.BlockSpec(blr.shape, lambda b, i: (0, 0)),
                  pl.BlockSpec(ops.shape, lambda b, i: (0, 0)),
                  pl.BlockSpec(lmask.shape, lambda b, i: (0, 0, 0))],
        out_specs=pl.BlockSpec((tb, MIX_W), lambda b, i: (b * nt + i, 0)),
        out_shape=jax.ShapeDtypeStruct((batch * seq, MIX_W), F32),
        scratch_shapes=[pltpu.VMEM((GLA_HEADS, GLA_DV, GLA_DV), F32)],
        compiler_params=_params("parallel", "arbitrary"),
        name="gla",
    )(x, wlr, blr, ops, lmask)


def _merge_body(x_ref, yn_ref, og_ref, ol_ref, wzr_ref, wmg_ref, ng_ref, nl_ref, wup_ref, wout_ref,
                lg_ref, lb_ref, o_ref, *, alpha):
    x = x_ref[...]
    xb = x.astype(BF16)
    zr = _dot(xb, wzr_ref[...])

    def head_rms(o, gain):
        parts = []
        for h in range(MIX_W // LANE):
            oh = o[:, h * LANE:(h + 1) * LANE]
            parts.append(oh * lax.rsqrt(jnp.mean(oh * oh, -1, keepdims=True) + NORM_EPS) * gain)
        return jnp.concatenate(parts, axis=1)

    z = zr[:, :MIX_W]
    r = zr[:, MIX_W:]
    y_gdn = head_rms(og_ref[...], ng_ref[...]) * (z * _sigmoid(z))
    y_gla = head_rms(ol_ref[...], nl_ref[...]) * (r * _sigmoid(r))
    ys = (yn_ref[...], y_gdn.astype(BF16), y_gla.astype(BF16))
    merged = jnp.zeros(x.shape, F32)
    for br in range(N_BRANCH):
        gate = _sigmoid(_dot(xb, wmg_ref[:, br * D_MODEL:(br + 1) * D_MODEL]))
        merged = merged + gate * _dot(ys[br], wup_ref[br])
    h = _dot(merged.astype(BF16), wout_ref[...])
    o_ref[...] = _layer_norm(alpha * x + h, lg_ref[...], lb_ref[...])


def _const_spec(shape):
    nd = len(shape)
    return pl.BlockSpec(shape, lambda i: (0,) * nd, pipeline_mode=pl.Buffered(1))


def _merge(x, y_nsa, o_gdn, o_gla, wzr, wmg, ng, nl, wup, wout, lg, lb, alpha, tm=256):
    m = x.shape[0]
    row = lambda w: pl.BlockSpec((tm, w), lambda i: (i, 0))
    return pl.pallas_call(
        functools.partial(_merge_body, alpha=alpha),
        grid=(m // tm,),
        in_specs=[row(D_MODEL), row(MIX_W), row(MIX_W), row(MIX_W),
                  _const_spec(wzr.shape), _const_spec(wmg.shape), _const_spec(ng.shape), _const_spec(nl.shape),
                  _const_spec(wup.shape), _const_spec(wout.shape), _const_spec(lg.shape), _const_spec(lb.shape)],
        out_specs=row(D_MODEL),
        out_shape=jax.ShapeDtypeStruct((m, D_MODEL), F32),
        compiler_params=_params("parallel"),
        name="mixer_merge",
    )(x, y_nsa, o_gdn, o_gla, wzr, wmg, ng, nl, wup, wout, lg, lb)


def _xattn_body(x_ref, kv_ref, wq_ref, wo_ref, lg_ref, lb_ref, o_ref, *, alpha):
    x = x_ref[...]
    q = _dot(x.astype(BF16), wq_ref[...]).astype(BF16)
    scale = XA_HD ** -0.5
    outs = []
    for h in range(XA_HEADS):
        hs = slice(h * XA_HD, (h + 1) * XA_HD)
        s = _dot_nt(q[:, hs], kv_ref[:, hs]) * scale
        m = jnp.max(s, -1, keepdims=True)
        e = jnp.exp(s - m)
        p = e / jnp.sum(e, -1, keepdims=True)
        outs.append(_dot(p.astype(BF16), kv_ref[:, D_MODEL + h * XA_HD:D_MODEL + (h + 1) * XA_HD]))
    o = jnp.concatenate(outs, axis=1)
    h_out = _dot(o.astype(BF16), wo_ref[...])
    o_ref[...] = _layer_norm(alpha * x + h_out, lg_ref[...], lb_ref[...])


def _xattn(x, kvm, wq, wo, lg, lb, alpha, batch, seq, tm=512):
    nt = seq // tm
    mt = kvm.shape[0] // batch
    cs = lambda shape: pl.BlockSpec(shape, lambda b, i: (0,) * len(shape), pipeline_mode=pl.Buffered(1))
    return pl.pallas_call(
        functools.partial(_xattn_body, alpha=alpha),
        grid=(batch, nt),
        in_specs=[pl.BlockSpec((tm, D_MODEL), lambda b, i: (b * nt + i, 0)),
                  pl.BlockSpec((mt, 2 * D_MODEL), lambda b, i: (b, 0)),
                  cs(wq.shape), cs(wo.shape), cs(lg.shape), cs(lb.shape)],
        out_specs=pl.BlockSpec((tm, D_MODEL), lambda b, i: (b * nt + i, 0)),
        out_shape=jax.ShapeDtypeStruct((batch * seq, D_MODEL), F32),
        compiler_params=_params("parallel", "parallel"),
        name="mem_xattn",
    )(x, kvm, wq, wo, lg, lb)


def _mlp_body(x_ref, w1_ref, w2_ref, lg_ref, lb_ref, o_ref, *, alpha):
    x = x_ref[...]
    h = jnp.maximum(_dot(x.astype(BF16), w1_ref[...]), 0.0)
    h = (h * h).astype(BF16)
    o_ref[...] = _layer_norm(alpha * x + _dot(h, w2_ref[...]), lg_ref[...], lb_ref[...])


def _mlp(x, w1, w2, lg, lb, alpha, tm=512):
    m = x.shape[0]
    return pl.pallas_call(
        functools.partial(_mlp_body, alpha=alpha),
        grid=(m // tm,),
        in_specs=[pl.BlockSpec((tm, D_MODEL), lambda i: (i, 0)),
                  _const_spec(w1.shape), _const_spec(w2.shape), _const_spec(lg.shape), _const_spec(lb.shape)],
        out_specs=pl.BlockSpec((tm, D_MODEL), lambda i: (i, 0)),
        out_shape=jax.ShapeDtypeStruct((m, D_MODEL), F32),
        compiler_params=_params("parallel"),
        name="sq_relu_mlp",
    )(x, w1, w2, lg, lb)


def _pad_cols(w, width):
    return jnp.pad(w, ((0, 0), (0, width - w.shape[1])))


def _nsa_weights(w_in):
    def kv(kind, g):
        o = O_NSA_KV + (kind * NSA_GROUPS + g) * NSA_HD
        return w_in[:, o:o + NSA_HD]
    cols = [w_in[:, O_NSA_Q:O_NSA_Q + NSA_HEADS * NSA_HD]]
    for g in range(NSA_GROUPS):
        cols += [kv(2, g), kv(4, g), kv(3, g), kv(5, g)]
    cols += [kv(0, 0), kv(0, 1), kv(1, 0), kv(1, 1)]
    main = jnp.concatenate(cols, axis=1).astype(BF16)
    gw = NSA_HPG * 3
    gates = [_pad_cols(w_in[:, O_NSA_G + g * gw:O_NSA_G + (g + 1) * gw], LANE) for g in range(NSA_GROUPS)]
    return main, jnp.concatenate(gates, axis=1).astype(BF16)


def _gla_weights(w_in, w_lr, b_lr):
    def pad_heads(w):
        lead = w.shape[0]
        w = w.reshape(lead, GLA_HEADS, GLA_DK)
        return jnp.pad(w, ((0, 0), (0, 0), (0, GLA_DV - GLA_DK))).reshape(lead, GLA_HEADS * GLA_DV)
    cols = [pad_heads(w_in[:, O_GLA_Q:O_GLA_Q + GLA_HEADS * GLA_DK]),
            pad_heads(w_in[:, O_GLA_K:O_GLA_K + GLA_HEADS * GLA_DK]),
            w_in[:, O_GLA_V:O_GLA_V + GLA_HEADS * GLA_DV],
            _pad_cols(w_in[:, O_GLA_LR:O_GLA_LR + GLA_RANK], LANE)]
    wlr = jnp.pad(pad_heads(w_lr), ((0, LANE - GLA_RANK), (0, 0)))
    return jnp.concatenate(cols, axis=1).astype(BF16), wlr, pad_heads(b_lr[None, :])


def _overlap_matrix(seq):
    nc = seq // CMP_STRIDE
    ns = seq // SEL_BLK
    c_start = np.arange(nc)[:, None] * CMP_STRIDE
    s_start = np.arange(ns)[None, :] * SEL_BLK
    return jnp.asarray((c_start < s_start + SEL_BLK) & (c_start + CMP_LEN > s_start), BF16)


def _layer(x, kvm_src, p, l, batch, seq, alpha):
    bt = batch * seq
    w_in = p["w_in"][l]

    w_main, w_gates = _nsa_weights(w_in)
    main = _mm(x, w_main, BF16)
    gate_logits = _mm(x, w_gates, F32)
    nc = seq // CMP_STRIDE
    cmp_in = main[:, 2 * MIX_W:].reshape(bt, 2 * NSA_GROUPS, NSA_HD).transpose(1, 0, 2)
    cmp_in = cmp_in.reshape(2 * NSA_GROUPS, batch * nc, CMP_STRIDE * NSA_HD)
    half = CMP_STRIDE * NSA_HD
    w1 = p["cmp_w1"][l].reshape(2, 2, half, CMP_HID).astype(BF16)
    pe = jnp.broadcast_to(p["cmp_pe"][l].reshape(2, 2, 1, half), (2, 2, SUBLANE, half)).astype(BF16)
    cmp_tok = _compress(cmp_in, w1, p["cmp_b1"][l][:, None, :], pe, p["cmp_w2"][l].astype(BF16), batch)
    y_nsa = _nsa_attention(main, gate_logits, cmp_tok, _overlap_matrix(seq), batch, seq)

    w_gdn = jnp.concatenate([w_in[:, O_GDN_QKV:O_GDN_QKV + 3 * MIX_W],
                             _pad_cols(w_in[:, O_GDN_A:O_GDN_A + 2 * GDN_HEADS], LANE)], axis=1).astype(BF16)
    gdn_in = _mm(x, w_gdn, F32)
    alog = _pad_cols(p["gdn_a_log"][l][None, :], LANE)
    dtb = _pad_cols(p["gdn_dt_bias"][l][None, :], LANE)
    o_gdn = _gdn(gdn_in[:, :3 * MIX_W], gdn_in[:, 3 * MIX_W:], p["gdn_conv"][l], alog, dtb, batch, seq)

    w_gla, wlr, blr = _gla_weights(w_in, p["gla_w_lr"][l], p["gla_b_lr"][l])
    gla_in = _mm(x, w_gla, F32)
    ops, lmask = _gla_constants()
    o_gla = _gla(gla_in, wlr, blr, jnp.asarray(ops, BF16), jnp.asarray(lmask), batch, seq)

    wzr = jnp.concatenate([w_in[:, O_GDN_Z:O_GDN_Z + MIX_W], w_in[:, O_GLA_R:O_GLA_R + MIX_W]], axis=1).astype(BF16)
    wmg = w_in[:, O_MERGE:O_MERGE + N_BRANCH * D_MODEL].astype(BF16)
    x = _merge(x, y_nsa, o_gdn, o_gla, wzr, wmg, p["gdn_norm"][l][None, :], p["gla_norm"][l][None, :],
               p["w_up"][l].astype(BF16), p["w_out"][l].astype(BF16),
               p["ln_g"][l, 0][None, :], p["ln_b"][l, 0][None, :], alpha)

    wkv = jnp.concatenate([p["xa_wk"][l], p["xa_wv"][l]], axis=1).astype(BF16)
    kvm = _mm(kvm_src, wkv, BF16)
    x = _xattn(x, kvm, p["xa_wq"][l].astype(BF16), p["xa_wo"][l].astype(BF16),
               p["ln_g"][l, 1][None, :], p["ln_b"][l, 1][None, :], alpha, batch, seq)

    return _mlp(x, p["mlp_w1"][l].astype(BF16), p["mlp_w2"][l].astype(BF16),
                p["ln_g"][l, 2][None, :], p["ln_b"][l, 2][None, :], alpha)


def kernel(x, mem, w_in, cmp_pe, cmp_w1, cmp_b1, cmp_w2, gdn_conv, gdn_a_log, gdn_dt_bias, gdn_norm, gla_w_lr,
           gla_b_lr, gla_norm, w_up, w_out, xa_wq, xa_wk, xa_wv, xa_wo, mlp_w1, mlp_w2, ln_g, ln_b):
    batch, seq, d = x.shape
    depth = w_in.shape[0]
    alpha = (2.0 * depth) ** 0.25
    p = dict(w_in=w_in, cmp_pe=cmp_pe, cmp_w1=cmp_w1, cmp_b1=cmp_b1, cmp_w2=cmp_w2, gdn_conv=gdn_conv,
             gdn_a_log=gdn_a_log, gdn_dt_bias=gdn_dt_bias, gdn_norm=gdn_norm, gla_w_lr=gla_w_lr,
             gla_b_lr=gla_b_lr, gla_norm=gla_norm, w_up=w_up, w_out=w_out, xa_wq=xa_wq, xa_wk=xa_wk,
             xa_wv=xa_wv, xa_wo=xa_wo, mlp_w1=mlp_w1, mlp_w2=mlp_w2, ln_g=ln_g, ln_b=ln_b)
    h = x.reshape(batch * seq, d)
    mem2 = mem.reshape(batch * mem.shape[1], d)
    for l in range(depth):
        h = _layer(h, mem2, p, l, batch, seq, alpha)
    return h.reshape(batch, seq, d)
```

```python
import functools

import numpy as np
import jax
import jax.numpy as jnp
from jax import lax
from jax.experimental import pallas as pl
from jax.experimental.pallas import tpu as pltpu

F32 = jnp.float32
BF16 = jnp.bfloat16

D_MODEL = 1024
MIX_W = D_MODEL // 2
N_BRANCH = 3
NSA_HEADS = 8
NSA_HD = MIX_W // NSA_HEADS
NSA_GROUPS = 2
NSA_HPG = NSA_HEADS // NSA_GROUPS
CMP_LEN = 32
CMP_STRIDE = 16
CMP_HID = D_MODEL // 4
SEL_BLK = 64
SEL_TOPK = 16
WIN = 512
Q_BLK = 128
GDN_HEADS = 4
GDN_HD = MIX_W // GDN_HEADS
CONV_K = 4
CHUNK = 64
GLA_HEADS = 4
GLA_DK = MIX_W // (2 * GLA_HEADS)
GLA_DV = MIX_W // GLA_HEADS
GLA_RANK = 16
GLA_TAU = 16.0
XA_HEADS = 4
XA_HD = D_MODEL // XA_HEADS
D_FF = 4 * D_MODEL
LN_EPS = 1e-5
NORM_EPS = 1e-6
NEG_INF = -1e30
FORCE_BONUS = 1e4

LANE = 128
SUBLANE = 8
VMEM_LIMIT_BYTES = 48 * 1024 * 1024
SEL_TILE = 512
NSA_ROWS = 16
NSA_RANGES = 4
REMOVED = -3e38
GLA_LEVELS = (32, 16, 8, 4, 2, 1)
Q_BLK_LOG2 = Q_BLK.bit_length() - 1
SEL_BLK_LOG2 = SEL_BLK.bit_length() - 1
CHUNK_LOG2 = CHUNK.bit_length() - 1

_IN_SIZES = (NSA_HEADS * NSA_HD, 6 * NSA_GROUPS * NSA_HD, NSA_HEADS * 3, 3 * MIX_W, GDN_HEADS, GDN_HEADS,
             MIX_W, GLA_HEADS * GLA_DK, GLA_HEADS * GLA_DK, GLA_HEADS * GLA_DV, GLA_RANK, MIX_W,
             N_BRANCH * D_MODEL)
_OFF = tuple(int(v) for v in np.cumsum((0,) + _IN_SIZES))
(O_NSA_Q, O_NSA_KV, O_NSA_G, O_GDN_QKV, O_GDN_A, O_GDN_B, O_GDN_Z, O_GLA_Q, O_GLA_K, O_GLA_V, O_GLA_LR,
 O_GLA_R, O_MERGE, _) = _OFF


def _params(*sem):
    return pltpu.CompilerParams(dimension_semantics=sem, vmem_limit_bytes=VMEM_LIMIT_BYTES)


def _dot(a, b):
    return jnp.dot(a, b, preferred_element_type=F32)


def _dot_nt(a, b):
    return lax.dot_general(a, b, (((1,), (1,)), ((), ())), preferred_element_type=F32)


def _dot_tn(a, b):
    return lax.dot_general(a, b, (((0,), (0,)), ((), ())), preferred_element_type=F32)


def _sigmoid(x):
    return 1.0 / (1.0 + jnp.exp(-x))


def _layer_norm(v, g, b):
    mu = jnp.mean(v, -1, keepdims=True)
    c = v - mu
    var = jnp.mean(c * c, -1, keepdims=True)
    return c * lax.rsqrt(var + LN_EPS) * g + b


def _split2(a):
    hi = a.astype(BF16)
    return hi, (a - hi.astype(F32)).astype(BF16)


def _dot_split(a, b):
    return _dot(a[0], b[0]) + (_dot(a[0], b[1]) + _dot(a[1], b[0]))


def _dot_exact_lhs(op, x):
    x1 = x.astype(BF16)
    r1 = x - x1.astype(F32)
    x2 = r1.astype(BF16)
    x3 = (r1 - x2.astype(F32)).astype(BF16)
    return _dot(op, x1) + (_dot(op, x2) + _dot(op, x3))


def _mm_body(x_ref, w_ref, *o_refs):
    res = _dot(x_ref[...].astype(BF16), w_ref[...])
    off = 0
    for o_ref in o_refs:
        o_ref[...] = res[:, off:off + o_ref.shape[1]].astype(o_ref.dtype)
        off += o_ref.shape[1]


def _mm(x, w, outs, tm=512):
    m, k = x.shape
    n = w.shape[1]
    assert n == sum(wd for wd, _ in outs) and all(wd % LANE == 0 for wd, _ in outs)
    tm = min(tm, m)
    return pl.pallas_call(
        _mm_body,
        grid=(m // tm,),
        in_specs=[pl.BlockSpec((tm, k), lambda i: (i, 0)),
                  pl.BlockSpec((k, n), lambda i: (0, 0))],
        out_specs=[pl.BlockSpec((tm, wd), lambda i: (i, 0)) for wd, _ in outs],
        out_shape=[jax.ShapeDtypeStruct((m, wd), dt) for wd, dt in outs],
        compiler_params=_params("parallel"),
        name="proj_mm",
    )(x, w)


def _compress_body(ch_ref, w1_ref, b1_ref, pe_ref, w2_ref, o_ref):
    nc = ch_ref.shape[0]
    ch = ch_ref[...]
    top = _dot(ch, w1_ref[0])
    bot = _dot(ch, w1_ref[1])
    bot_next = pltpu.roll(bot, nc - 1, axis=0)
    pe_term = _dot(pe_ref[0], w1_ref[0]) + _dot(pe_ref[1], w1_ref[1])
    h = top + bot_next + pe_term[0:1] + b1_ref[...]
    h = jax.nn.gelu(h, approximate=True)
    o_ref[...] = _dot(h.astype(BF16), w2_ref[...]).astype(o_ref.dtype)


def _compress(ch, w1, b1, pe, w2, batch):
    nc = ch.shape[1] // batch
    half = CMP_STRIDE * NSA_HD
    return pl.pallas_call(
        _compress_body,
        grid=(2, NSA_GROUPS, batch),
        in_specs=[pl.BlockSpec((None, nc, half), lambda t, g, b: (t * NSA_GROUPS + g, b, 0)),
                  pl.BlockSpec((None, 2, half, CMP_HID), lambda t, g, b: (t, 0, 0, 0)),
                  pl.BlockSpec((None, 1, CMP_HID), lambda t, g, b: (t, 0, 0)),
                  pl.BlockSpec((None, 2, SUBLANE, half), lambda t, g, b: (t, 0, 0, 0)),
                  pl.BlockSpec((None, CMP_HID, NSA_HD), lambda t, g, b: (t, 0, 0))],
        out_specs=pl.BlockSpec((None, nc, NSA_HD), lambda t, g, b: (t * NSA_GROUPS + g, b, 0)),
        out_shape=jax.ShapeDtypeStruct((2 * NSA_GROUPS, batch * nc, NSA_HD), BF16),
        compiler_params=_params("parallel", "parallel", "parallel"),
        name="nsa_compress",
    )(ch, w1, b1, pe, w2)


def _nsa_body(q_ref, kv_ref, kc_ref, vc_ref, gl_ref, ov_ref, o_ref, m_scr, l_scr, acc_scr, tiles_ref, *,
              nc, ns, qb0):
    g = pl.program_id(1)
    qb = qb0 + pl.program_id(2)
    qs = qb * Q_BLK
    scale = NSA_HD ** -0.5
    n_chunks = Q_BLK // NSA_ROWS

    q4 = q_ref[...] * jnp.asarray(scale, q_ref.dtype)
    qst = jnp.concatenate([q4[:, h * NSA_HD:(h + 1) * NSA_HD] for h in range(NSA_HPG)], axis=0)
    slopes = [jnp.where(g == 0, 2.0 ** -(h + 1), 2.0 ** -(NSA_HPG + h + 1)).astype(F32) for h in range(NSA_HPG)]
    tq1 = qs + lax.broadcasted_iota(jnp.int32, (Q_BLK, 1), 0)

    def chunk_rows(h, rc):
        return slice(h * Q_BLK + rc * NSA_ROWS, h * Q_BLK + (rc + 1) * NSA_ROWS)

    order = [(h, rc) for h in range(NSA_HPG) for rc in range(n_chunks)]

    def stacked(parts):
        return jnp.concatenate([parts[i] for i in order], axis=0)

    def lane_blocks(x, op):
        out = x[:, 0:LANE]
        for j in range(1, x.shape[1] // LANE):
            out = op(out, x[:, j * LANE:(j + 1) * LANE])
        return out

    def lane_tiled(x, width):
        return jnp.concatenate([x] * (width // LANE), axis=1)

    def exp_pass(logit_parts, m_all, masks=None):
        width = logit_parts[order[0]].shape[1]
        m_lanes = jnp.broadcast_to(m_all, (m_all.shape[0], LANE))
        e_parts, sums = {}, {}
        for h, rc in order:
            e = jnp.exp(logit_parts[h, rc] - lane_tiled(m_lanes[chunk_rows(h, rc)], width))
            if masks is not None:
                e = jnp.where(masks[rc], e, 0.0)
            e_parts[h, rc] = e
            sums[h, rc] = lane_blocks(e, jnp.add)
        return e_parts, jnp.sum(stacked(sums), -1, keepdims=True)

    def masked_logits(s_all, key_pos, is_valid):
        parts, maxes, masks = {}, {}, {}
        for rc in range(n_chunks):
            dist = tq1[rc * NSA_ROWS:(rc + 1) * NSA_ROWS] - key_pos
            masks[rc] = is_valid(dist)
            d_f = dist.astype(F32)
            for h in range(NSA_HPG):
                s = jnp.where(masks[rc], s_all[chunk_rows(h, rc)] - slopes[h] * d_f, NEG_INF)
                parts[h, rc] = s
                maxes[h, rc] = lane_blocks(s, jnp.maximum)
        return parts, masks, jnp.max(stacked(maxes), -1, keepdims=True)

    c_end = lax.broadcasted_iota(jnp.int32, (1, nc), 1) * CMP_STRIDE + (CMP_LEN - 1)
    logit_c, mask_c, m_c = masked_logits(_dot_nt(qst, kc_ref[...]), c_end, lambda d: d >= 0)
    e_c, l_c = exp_pass(logit_c, m_c, mask_c)
    r_c = 1.0 / jnp.maximum(l_c, 1e-30)
    o_cmp = _dot(stacked({i: e_c[i].astype(BF16) for i in order}), vc_ref[...]) * r_c

    r_lanes = jnp.broadcast_to(r_c, (r_c.shape[0], LANE))
    p_hi, p_lo = [], []
    for rc in range(n_chunks):
        p_sum = None
        for h in range(NSA_HPG):
            p = e_c[h, rc] * lane_tiled(r_lanes[chunk_rows(h, rc)], nc)
            p_sum = p if p_sum is None else p_sum + p
        hi = p_sum.astype(BF16)
        p_hi.append(hi)
        p_lo.append((p_sum - hi.astype(F32)).astype(BF16))
    p_hi = jnp.concatenate(p_hi, axis=0)
    p_lo = jnp.concatenate(p_lo, axis=0)

    imp_t = (_dot(p_hi, ov_ref[...]) + _dot(p_lo, ov_ref[...])).T
    blk = lax.broadcasted_iota(jnp.int32, (ns, Q_BLK), 0)
    tq_row = qs + lax.broadcasted_iota(jnp.int32, (1, Q_BLK), 1)
    cur = tq_row >> SEL_BLK_LOG2
    forced = (blk == 0) | (blk == cur) | (blk == cur - 1)
    valid = blk * SEL_BLK <= tq_row
    score = jnp.where(valid, imp_t + jnp.where(forced, FORCE_BONUS, 0.0), NEG_INF)

    def pick(_, carry):
        sc, sel = carry
        m = jnp.max(sc, 0, keepdims=True)
        idx = jnp.min(jnp.where(sc == m, blk, ns), 0, keepdims=True)
        hit = blk == idx
        return jnp.where(hit, REMOVED, sc), jnp.where(hit, 1.0, sel)

    _, sel_t = lax.fori_loop(0, min(SEL_TOPK, ns), pick, (score, jnp.zeros((ns, Q_BLK), F32)))
    sel = sel_t.T
    sel_bias = ((sel - 1.0) * -NEG_INF).astype(BF16)
    blk_any = jnp.max(sel, 0, keepdims=True)

    tile_blks = SEL_TILE // SEL_BLK
    blk_row = lax.broadcasted_iota(jnp.int32, (ns, SEL_TILE), 0)
    blk_of_key = lax.broadcasted_iota(jnp.int32, (ns, SEL_TILE), 1) >> SEL_BLK_LOG2
    key_in_tile = lax.broadcasted_iota(jnp.int32, (1, SEL_TILE), 1)
    m_scr[...] = jnp.full(m_scr.shape, NEG_INF, F32)
    l_scr[...] = jnp.zeros(l_scr.shape, F32)
    acc_scr[...] = jnp.zeros(acc_scr.shape, F32)

    tile_of_blk = lax.broadcasted_iota(jnp.int32, (ns, LANE), 0) >> (SEL_TILE.bit_length() - 1 - SEL_BLK_LOG2)
    in_tile = jnp.where(tile_of_blk == lax.broadcasted_iota(jnp.int32, (ns, LANE), 1), 1.0, 0.0).astype(BF16)
    tile_cnt = _dot(jnp.broadcast_to(blk_any, (SUBLANE, ns)).astype(BF16), in_tile)
    n_visible = (qs + Q_BLK + SEL_TILE - 1) // SEL_TILE
    n_active = jnp.int32(0)
    for kt in range(ns // tile_blks):
        tiles_ref[n_active] = kt
        n_active = n_active + ((tile_cnt[0, kt] > 0.0) & (kt < n_visible)).astype(jnp.int32)

    def sel_tile(j, carry):
        kt = tiles_ref[j]
        k0 = pl.multiple_of(kt * SEL_TILE, SEL_TILE)
        st_all = _dot_nt(qst, kv_ref[pl.ds(k0, SEL_TILE), 0:NSA_HD])
        expand = jnp.where(blk_row == blk_of_key + kt * tile_blks, 1.0, 0.0).astype(BF16)
        key_pos = k0 + key_in_tile
        bias_all = jnp.where(tq1 - key_pos >= 0, _dot(sel_bias, expand), NEG_INF)
        logits, maxes = {}, {}
        for rc in range(n_chunks):
            qr = slice(rc * NSA_ROWS, (rc + 1) * NSA_ROWS)
            d_f = (tq1[qr] - key_pos).astype(F32)
            for h in range(NSA_HPG):
                s = st_all[chunk_rows(h, rc)] + (bias_all[qr] - slopes[h] * d_f)
                logits[h, rc] = s
                maxes[h, rc] = lane_blocks(s, jnp.maximum)
        m_old = m_scr[...]
        m_new = jnp.maximum(m_old, jnp.max(stacked(maxes), -1, keepdims=True))
        p_parts, l_tile = exp_pass(logits, m_new)
        alpha = jnp.exp(m_old - m_new)
        m_scr[...] = m_new
        l_scr[...] = alpha * l_scr[...] + l_tile
        v = kv_ref[pl.ds(k0, SEL_TILE), 2 * NSA_HD:3 * NSA_HD]
        p_all = stacked({i: p_parts[i].astype(BF16) for i in order})
        acc_scr[...] = alpha * acc_scr[...] + _dot(p_all, v)
        return carry

    lax.fori_loop(0, n_active, sel_tile, 0)
    o_slc = acc_scr[...] * (1.0 / jnp.maximum(l_scr[...], 1e-30))

    wlen = WIN + Q_BLK
    w0 = pl.multiple_of(jnp.maximum(qs - WIN, 0), Q_BLK)
    pos_w = w0 + lax.broadcasted_iota(jnp.int32, (1, wlen), 1)
    logit_w, _, m_w = masked_logits(_dot_nt(qst, kv_ref[pl.ds(w0, wlen), NSA_HD:2 * NSA_HD]), pos_w,
                                    lambda d: (d >= 0) & (d < WIN))
    e_w, l_w = exp_pass(logit_w, m_w)
    o_win = (_dot(stacked({i: e_w[i].astype(BF16) for i in order}), kv_ref[pl.ds(w0, wlen), 3 * NSA_HD:4 * NSA_HD])
             * (1.0 / jnp.maximum(l_w, 1e-30)))

    gates = _sigmoid(gl_ref[...])
    outs = []
    for h in range(NSA_HPG):
        r = slice(h * Q_BLK, (h + 1) * Q_BLK)
        outs.append(gates[:, 3 * h:3 * h + 1] * o_cmp[r] + gates[:, 3 * h + 1:3 * h + 2] * o_slc[r]
                    + gates[:, 3 * h + 2:3 * h + 3] * o_win[r])
    o_ref[...] = jnp.concatenate(outs, axis=1).astype(o_ref.dtype)


def _round_up(v, m):
    return (v + m - 1) // m * m


def _nsa_attention(main, gate_logits, cmp_tok, overlap, batch, seq):
    nq = seq // Q_BLK
    gw = NSA_HPG * NSA_HD
    hq = NSA_HPG * Q_BLK
    n_ranges = min(NSA_RANGES, nq)
    per = nq // n_ranges
    outs = []
    for r in range(n_ranges):
        q0 = r * per
        seen = (q0 + per) * Q_BLK
        nc = min(seq // CMP_STRIDE, _round_up(seen // CMP_STRIDE, LANE))
        ns = min(seq // SEL_BLK, _round_up(seen // SEL_BLK, LANE))
        out = pl.pallas_call(
            functools.partial(_nsa_body, nc=nc, ns=ns, qb0=q0),
            grid=(batch, NSA_GROUPS, per),
            in_specs=[pl.BlockSpec((Q_BLK, gw), lambda b, g, i, q0=q0: (b * nq + q0 + i, g)),
                      pl.BlockSpec((seq, gw), lambda b, g, i: (b, NSA_GROUPS + g)),
                      pl.BlockSpec((None, None, nc, NSA_HD), lambda b, g, i: (g, b, 0, 0)),
                      pl.BlockSpec((None, None, nc, NSA_HD), lambda b, g, i: (NSA_GROUPS + g, b, 0, 0)),
                      pl.BlockSpec((Q_BLK, LANE), lambda b, g, i, q0=q0: (b * nq + q0 + i, g)),
                      pl.BlockSpec((nc, ns), lambda b, g, i: (0, 0))],
            out_specs=pl.BlockSpec((Q_BLK, gw), lambda b, g, i: (b * per + i, g)),
            out_shape=jax.ShapeDtypeStruct((batch * per * Q_BLK, MIX_W), BF16),
            scratch_shapes=[pltpu.VMEM((hq, 1), F32), pltpu.VMEM((hq, 1), F32), pltpu.VMEM((hq, NSA_HD), F32),
                            pltpu.SMEM((ns * SEL_BLK // SEL_TILE,), jnp.int32)],
            compiler_params=_params("parallel", "parallel", "arbitrary"),
            name="nsa_attention",
        )(main, main, cmp_tok, cmp_tok, gate_logits, overlap)
        outs.append(out.reshape(batch, per * Q_BLK, MIX_W))
    return jnp.concatenate(outs, axis=1).reshape(batch * seq, MIX_W)


def _unit_lower_inverse(a):
    n = a.shape[0]
    eye = (lax.broadcasted_iota(jnp.int32, (n, n), 0) == lax.broadcasted_iota(jnp.int32, (n, n), 1)).astype(F32)
    x = _split2(eye - a)
    p = _split2(a)
    power = 1
    while 2 * power < CHUNK:
        p = _split2(_dot_split(p, p))
        x = _split2((x[0].astype(F32) + x[1].astype(F32)) + _dot_split(x, p))
        power *= 2
    return x


def _gdn_body(raw_ref, ab_ref, conv_ref, alog_ref, dtb_ref, o_ref, s_ref, tail_ref, buf_ref, *, tb):
    @pl.when(pl.program_id(1) == 0)
    def _():
        s_ref[...] = jnp.zeros_like(s_ref)
        tail_ref[...] = jnp.zeros_like(tail_ref)

    raw = raw_ref[...]
    buf_ref[0:SUBLANE] = tail_ref[...]
    buf_ref[SUBLANE:SUBLANE + tb] = raw
    tail_ref[...] = raw[tb - SUBLANE:tb]
    x = jnp.zeros_like(raw)
    for k in range(CONV_K):
        x = x + buf_ref[pl.ds(SUBLANE - (CONV_K - 1) + k, tb)] * conv_ref[k:k + 1]
    x = x * _sigmoid(x)

    ab = ab_ref[...]
    zed = ab + dtb_ref[...]
    softplus = jnp.maximum(zed, 0.0) + jnp.log1p(jnp.exp(-jnp.abs(zed)))
    g_all = -jnp.exp(alog_ref[...]) * softplus
    beta_all = _sigmoid(ab)

    c2 = 2 * CHUNK
    ri = lax.broadcasted_iota(jnp.int32, (c2, c2), 0)
    ci = lax.broadcasted_iota(jnp.int32, (c2, c2), 1)
    tri2 = jnp.where((ri >= ci) & ((ri >> CHUNK_LOG2) == (ci >> CHUNK_LOG2)), 1.0, 0.0).astype(BF16)
    hc = GDN_HEADS * CHUNK
    rs = lax.broadcasted_iota(jnp.int32, (hc, hc), 0)
    cs = lax.broadcasted_iota(jnp.int32, (hc, hc), 1)
    same_head = (rs >> CHUNK_LOG2) == (cs >> CHUNK_LOG2)
    tri = same_head & (rs >= cs)
    strict = same_head & (rs > cs)
    qscale = GDN_HD ** -0.5

    def l2n(v):
        return v * lax.rsqrt(jnp.sum(v * v, -1, keepdims=True) + NORM_EPS)

    def stack(fn):
        return jnp.concatenate([fn(h) for h in range(GDN_HEADS)], axis=0)

    for pair in range(tb // c2):
        pr = slice(pair * c2, (pair + 1) * c2)
        gc2 = _dot_exact_lhs(tri2, g_all[pr])
        gc2t = gc2.T
        for half in range(2):
            rows = slice(pair * c2 + half * CHUNK, pair * c2 + (half + 1) * CHUNK)
            hr = slice(half * CHUNK, (half + 1) * CHUNK)
            gcol = stack(lambda h: gc2[hr, h:h + 1])
            grow = jnp.concatenate([gc2t[h:h + 1, hr] for h in range(GDN_HEADS)], axis=1)
            g_last = stack(lambda h: jnp.broadcast_to(gc2[hr, h:h + 1][CHUNK - 1:CHUNK], (CHUNK, 1)))
            beta = stack(lambda h: beta_all[rows, GDN_HEADS + h:GDN_HEADS + h + 1])
            qs_ = stack(lambda h: l2n(x[rows, h * GDN_HD:(h + 1) * GDN_HD])) * qscale
            ks_ = stack(lambda h: l2n(x[rows, MIX_W + h * GDN_HD:MIX_W + (h + 1) * GDN_HD]))
            vs_ = stack(lambda h: x[rows, 2 * MIX_W + h * GDN_HD:2 * MIX_W + (h + 1) * GDN_HD])
            decay = jnp.where(tri, jnp.exp(jnp.where(tri, gcol - grow, 0.0)), 0.0)
            kb = ks_ * beta
            a_mat = jnp.where(strict, _dot_nt(kb, ks_) * decay, 0.0)
            t_inv = _unit_lower_inverse(a_mat)
            egc = jnp.exp(gcol)
            sol = _dot_split(t_inv, _split2(jnp.concatenate([vs_ * beta, kb * egc], axis=1)))
            qk = jnp.where(tri, _dot_nt(qs_, ks_) * decay, 0.0)
            q_dec = qs_ * egc
            k_tail = ks_ * jnp.exp(g_last - gcol)
            a_last = jnp.exp(g_last)
            states = [s_ref[h] for h in range(GDN_HEADS)]
            hrows = [slice(h * CHUNK, (h + 1) * CHUNK) for h in range(GDN_HEADS)]
            v_new = stack(lambda h: sol[hrows[h], :GDN_HD] - _dot(sol[hrows[h], GDN_HD:], states[h]))
            o = _dot(qk, v_new) + stack(lambda h: _dot(q_dec[hrows[h]], states[h]))
            for h in range(GDN_HEADS):
                s_ref[h] = (states[h] * a_last[hrows[h]][0:1] + _dot_tn(k_tail[hrows[h]], v_new[hrows[h]]))
                o_ref[rows, h * GDN_HD:(h + 1) * GDN_HD] = o[hrows[h]]


def _gdn(raw, ab, conv, alog, dtb, batch, seq, tb=256):
    nt = seq // tb
    return pl.pallas_call(
        functools.partial(_gdn_body, tb=tb),
        grid=(batch, nt),
        in_specs=[pl.BlockSpec((tb, 3 * MIX_W), lambda b, i: (b * nt + i, 0)),
                  pl.BlockSpec((tb, LANE), lambda b, i: (b * nt + i, 0)),
                  pl.BlockSpec((CONV_K, 3 * MIX_W), lambda b, i: (0, 0)),
                  pl.BlockSpec((1, LANE), lambda b, i: (0, 0)),
                  pl.BlockSpec((1, LANE), lambda b, i: (0, 0))],
        out_specs=pl.BlockSpec((tb, MIX_W), lambda b, i: (b * nt + i, 0)),
        out_shape=jax.ShapeDtypeStruct((batch * seq, MIX_W), F32),
        scratch_shapes=[pltpu.VMEM((GDN_HEADS, GDN_HD, GDN_HD), F32),
                        pltpu.VMEM((SUBLANE, 3 * MIX_W), F32),
                        pltpu.VMEM((SUBLANE + tb, 3 * MIX_W), F32)],
        compiler_params=_params("parallel", "arbitrary"),
        name="gated_delta_net",
    )(raw, ab, conv, alog, dtb)


def _gla_constants():
    c = CHUNK
    t = np.arange(c)[:, None]
    s = np.arange(c)[None, :]
    ops = [(s <= t), (s > t)]
    masks = []
    for m in GLA_LEVELS:
        r = (t // (2 * m)) * (2 * m) + m
        upper = (t % (2 * m)) >= m
        ops.append(np.where(upper, (s > r) & (s <= t), (s > t) & (s <= r)))
        i, j = t, s
        masks.append((i // (2 * m) == j // (2 * m)) & ((i % (2 * m)) >= m) & ((j % (2 * m)) < m))
    return (np.concatenate(ops, 0).astype(np.float32), np.stack(masks).astype(np.float32))


def _gla_body(x_ref, wlr_ref, blr_ref, ops_ref, lm_ref, o_ref, s_ref, *, tb):
    @pl.when(pl.program_id(1) == 0)
    def _():
        s_ref[...] = jnp.zeros_like(s_ref)

    hw = GLA_DV
    kw = GLA_HEADS * hw
    r1 = lax.broadcasted_iota(jnp.int32, (CHUNK, CHUNK), 0)
    c1 = lax.broadcasted_iota(jnp.int32, (CHUNK, CHUNK), 1)
    eye = r1 == c1
    qscale = GLA_DK ** -0.5
    for c in range(tb // CHUNK):
        rows = slice(c * CHUNK, (c + 1) * CHUNK)
        lr = x_ref[rows, 3 * kw:3 * kw + LANE]
        z = _dot(lr, wlr_ref[...]) + blr_ref[...]
        log_a = (jnp.minimum(z, 0.0) - jnp.log1p(jnp.exp(-jnp.abs(z)))) / GLA_TAU
        e_all = jnp.exp(_dot_exact_lhs(ops_ref[...], log_a))
        for h in range(GLA_HEADS):
            hs = slice(h * hw, (h + 1) * hw)
            qh = x_ref[rows, hs] * qscale
            kh = x_ref[rows, kw + h * hw:kw + (h + 1) * hw]
            vh = x_ref[rows, 2 * kw + h * hw:2 * kw + (h + 1) * hw]
            scores = jnp.where(eye, _dot_nt(qh, kh), 0.0)
            for lvl in range(len(GLA_LEVELS)):
                xl = e_all[(2 + lvl) * CHUNK:(3 + lvl) * CHUNK, hs]
                scores = scores + lm_ref[lvl] * _dot_nt(qh * xl, kh * xl)
            e_cum = e_all[0:CHUNK, hs]
            e_tail = e_all[CHUNK:2 * CHUNK, hs]
            state_t = s_ref[h]
            o = _dot_nt(qh * e_cum, state_t) + _dot(scores, vh)
            s_ref[h] = state_t * e_cum[CHUNK - 1:CHUNK] + _dot_tn(vh, kh * e_tail)
            o_ref[rows, hs] = o


def _gla(x, wlr, blr, ops, lmask, batch, seq, tb=256):
    nt = seq // tb
    width = x.shape[1]
    return pl.pallas_call(
        functools.partial(_gla_body, tb=tb),
        grid=(batch, nt),
        in_specs=[pl.BlockSpec((tb, width), lambda b, i: (b * nt + i, 0)),
                  pl.BlockSpec(wlr.shape, lambda b, i: (0, 0)),
                  pl.BlockSpec(blr.shape, lambda b, i: (0, 0)),
                  pl.BlockSpec(ops.shape, lambda b, i: (0, 0)),
                  pl.BlockSpec(lmask.shape, lambda b, i: (0, 0, 0))],
        out_specs=pl.BlockSpec((tb, MIX_W), lambda b, i: (b * nt + i, 0)),
        out_shape=jax.ShapeDtypeStruct((batch * seq, MIX_W), F32),
        scratch_shapes=[pltpu.VMEM((GLA_HEADS, GLA_DV, GLA_DV), F32)],
        compiler_params=_params("parallel", "arbitrary"),
        name="gla",
    )(x, wlr, blr, ops, lmask)


def _merge_body(x_ref, yn_ref, og_ref, ol_ref, wzr_ref, wmg_ref, ng_ref, nl_ref, wup_ref, wout_ref,
                lg_ref, lb_ref, o_ref, *, alpha):
    x = x_ref[...]
    xb = x.astype(BF16)
    zr = _dot(xb, wzr_ref[...])

    def head_rms(o, gain):
        parts = []
        for h in range(MIX_W // LANE):
            oh = o[:, h * LANE:(h + 1) * LANE]
            parts.append(oh * lax.rsqrt(jnp.mean(oh * oh, -1, keepdims=True) + NORM_EPS) * gain)
        return jnp.concatenate(parts, axis=1)

    z = zr[:, :MIX_W]
    r = zr[:, MIX_W:]
    y_gdn = head_rms(og_ref[...], ng_ref[...]) * (z * _sigmoid(z))
    y_gla = head_rms(ol_ref[...], nl_ref[...]) * (r * _sigmoid(r))
    ys = (yn_ref[...], y_gdn.astype(BF16), y_gla.astype(BF16))
    merged = jnp.zeros(x.shape, F32)
    for br in range(N_BRANCH):
        gate = _sigmoid(_dot(xb, wmg_ref[:, br * D_MODEL:(br + 1) * D_MODEL]))
        merged = merged + gate * _dot(ys[br], wup_ref[br])
    h = _dot(merged.astype(BF16), wout_ref[...])
    o_ref[...] = _layer_norm(alpha * x + h, lg_ref[...], lb_ref[...])


def _const_spec(shape):
    nd = len(shape)
    return pl.BlockSpec(shape, lambda i: (0,) * nd, pipeline_mode=pl.Buffered(1))


def _merge(x, y_nsa, o_gdn, o_gla, wzr, wmg, ng, nl, wup, wout, lg, lb, alpha, tm=256):
    m = x.shape[0]
    row = lambda w: pl.BlockSpec((tm, w), lambda i: (i, 0))
    return pl.pallas_call(
        functools.partial(_merge_body, alpha=alpha),
        grid=(m // tm,),
        in_specs=[row(D_MODEL), row(MIX_W), row(MIX_W), row(MIX_W),
                  _const_spec(wzr.shape), _const_spec(wmg.shape), _const_spec(ng.shape), _const_spec(nl.shape),
                  _const_spec(wup.shape), _const_spec(wout.shape), _const_spec(lg.shape), _const_spec(lb.shape)],
        out_specs=row(D_MODEL),
        out_shape=jax.ShapeDtypeStruct((m, D_MODEL), F32),
        compiler_params=_params("parallel"),
        name="mixer_merge",
    )(x, y_nsa, o_gdn, o_gla, wzr, wmg, ng, nl, wup, wout, lg, lb)


def _xattn_body(x_ref, kv_ref, wq_ref, wo_ref, lg_ref, lb_ref, o_ref, *, alpha):
    x = x_ref[...]
    q = _dot(x.astype(BF16), wq_ref[...]).astype(BF16)
    scale = XA_HD ** -0.5
    outs = []
    for h in range(XA_HEADS):
        hs = slice(h * XA_HD, (h + 1) * XA_HD)
        s = _dot_nt(q[:, hs], kv_ref[:, hs]) * scale
        m = jnp.max(s, -1, keepdims=True)
        e = jnp.exp(s - m)
        p = e / jnp.sum(e, -1, keepdims=True)
        outs.append(_dot(p.astype(BF16), kv_ref[:, D_MODEL + h * XA_HD:D_MODEL + (h + 1) * XA_HD]))
    o = jnp.concatenate(outs, axis=1)
    h_out = _dot(o.astype(BF16), wo_ref[...])
    o_ref[...] = _layer_norm(alpha * x + h_out, lg_ref[...], lb_ref[...])


def _xattn(x, kvm, wq, wo, lg, lb, alpha, batch, seq, tm=512):
    nt = seq // tm
    mt = kvm.shape[0] // batch
    cs = lambda shape: pl.BlockSpec(shape, lambda b, i: (0,) * len(shape), pipeline_mode=pl.Buffered(1))
    return pl.pallas_call(
        functools.partial(_xattn_body, alpha=alpha),
        grid=(batch, nt),
        in_specs=[pl.BlockSpec((tm, D_MODEL), lambda b, i: (b * nt + i, 0)),
                  pl.BlockSpec((mt, 2 * D_MODEL), lambda b, i: (b, 0)),
                  cs(wq.shape), cs(wo.shape), cs(lg.shape), cs(lb.shape)],
        out_specs=pl.BlockSpec((tm, D_MODEL), lambda b, i: (b * nt + i, 0)),
        out_shape=jax.ShapeDtypeStruct((batch * seq, D_MODEL), F32),
        compiler_params=_params("parallel", "parallel"),
        name="mem_xattn",
    )(x, kvm, wq, wo, lg, lb)


def _mlp_body(x_ref, w1_ref, w2_ref, lg_ref, lb_ref, o_ref, *, alpha):
    x = x_ref[...]
    h = jnp.maximum(_dot(x.astype(BF16), w1_ref[...]), 0.0)
    h = (h * h).astype(BF16)
    o_ref[...] = _layer_norm(alpha * x + _dot(h, w2_ref[...]), lg_ref[...], lb_ref[...])


def _mlp(x, w1, w2, lg, lb, alpha, tm=512):
    m = x.shape[0]
    return pl.pallas_call(
        functools.partial(_mlp_body, alpha=alpha),
        grid=(m // tm,),
        in_specs=[pl.BlockSpec((tm, D_MODEL), lambda i: (i, 0)),
                  _const_spec(w1.shape), _const_spec(w2.shape), _const_spec(lg.shape), _const_spec(lb.shape)],
        out_specs=pl.BlockSpec((tm, D_MODEL), lambda i: (i, 0)),
        out_shape=jax.ShapeDtypeStruct((m, D_MODEL), F32),
        compiler_params=_params("parallel"),
        name="sq_relu_mlp",
    )(x, w1, w2, lg, lb)


def _pad_cols(w, width):
    return jnp.pad(w, ((0, 0), (0, width - w.shape[1])))


def _nsa_weights(w_in):
    def kv(kind, g):
        o = O_NSA_KV + (kind * NSA_GROUPS + g) * NSA_HD
        return w_in[:, o:o + NSA_HD]
    cols = [w_in[:, O_NSA_Q:O_NSA_Q + NSA_HEADS * NSA_HD]]
    for g in range(NSA_GROUPS):
        cols += [kv(2, g), kv(4, g), kv(3, g), kv(5, g)]
    cols += [kv(0, 0), kv(0, 1), kv(1, 0), kv(1, 1)]
    main = jnp.concatenate(cols, axis=1).astype(BF16)
    gw = NSA_HPG * 3
    gates = [_pad_cols(w_in[:, O_NSA_G + g * gw:O_NSA_G + (g + 1) * gw], LANE) for g in range(NSA_GROUPS)]
    return main, jnp.concatenate(gates, axis=1).astype(BF16)


def _gla_weights(w_in, w_lr, b_lr):
    def pad_heads(w):
        lead = w.shape[0]
        w = w.reshape(lead, GLA_HEADS, GLA_DK)
        return jnp.pad(w, ((0, 0), (0, 0), (0, GLA_DV - GLA_DK))).reshape(lead, GLA_HEADS * GLA_DV)
    cols = [pad_heads(w_in[:, O_GLA_Q:O_GLA_Q + GLA_HEADS * GLA_DK]),
            pad_heads(w_in[:, O_GLA_K:O_GLA_K + GLA_HEADS * GLA_DK]),
            w_in[:, O_GLA_V:O_GLA_V + GLA_HEADS * GLA_DV],
            _pad_cols(w_in[:, O_GLA_LR:O_GLA_LR + GLA_RANK], LANE)]
    wlr = jnp.pad(pad_heads(w_lr), ((0, LANE - GLA_RANK), (0, 0)))
    return jnp.concatenate(cols, axis=1).astype(BF16), wlr, pad_heads(b_lr[None, :])


def _overlap_matrix(seq):
    nc = seq // CMP_STRIDE
    ns = seq // SEL_BLK
    c_start = np.arange(nc)[:, None] * CMP_STRIDE
    s_start = np.arange(ns)[None, :] * SEL_BLK
    return jnp.asarray((c_start < s_start + SEL_BLK) & (c_start + CMP_LEN > s_start), BF16)


def _nsa_branch(x, p, l, batch, seq):
    w_in = p["w_in"][l]
    w_main, w_gates = _nsa_weights(w_in)
    cmp_w = 2 * NSA_GROUPS * NSA_HD
    qkv, cmp_src, gate_logits = _mm(x, jnp.concatenate([w_main, w_gates], axis=1),
                                    [(2 * MIX_W, BF16), (cmp_w, BF16), (NSA_GROUPS * LANE, F32)])
    nc = seq // CMP_STRIDE
    half = CMP_STRIDE * NSA_HD
    cmp_in = cmp_src.reshape(batch * seq, 2 * NSA_GROUPS, NSA_HD).transpose(1, 0, 2)
    cmp_in = cmp_in.reshape(2 * NSA_GROUPS, batch * nc, half)
    w1 = p["cmp_w1"][l].reshape(2, 2, half, CMP_HID).astype(BF16)
    pe = jnp.broadcast_to(p["cmp_pe"][l].reshape(2, 2, 1, half), (2, 2, SUBLANE, half)).astype(BF16)
    cmp_tok = _compress(cmp_in, w1, p["cmp_b1"][l][:, None, :], pe, p["cmp_w2"][l].astype(BF16), batch)
    cmp_tok = cmp_tok.reshape(2 * NSA_GROUPS, batch, nc, NSA_HD)
    return _nsa_attention(qkv, gate_logits, cmp_tok, _overlap_matrix(seq), batch, seq)


def _gdn_branch(x, p, l, batch, seq):
    w_in = p["w_in"][l]
    w_gdn = jnp.concatenate([w_in[:, O_GDN_QKV:O_GDN_QKV + 3 * MIX_W],
                             _pad_cols(w_in[:, O_GDN_A:O_GDN_A + 2 * GDN_HEADS], LANE)], axis=1).astype(BF16)
    raw, ab = _mm(x, w_gdn, [(3 * MIX_W, F32), (LANE, F32)])
    alog = _pad_cols(p["gdn_a_log"][l][None, :], LANE)
    dtb = _pad_cols(p["gdn_dt_bias"][l][None, :], LANE)
    return _gdn(raw, ab, p["gdn_conv"][l], alog, dtb, batch, seq)


def _gla_branch(x, p, l, batch, seq):
    w_gla, wlr, blr = _gla_weights(p["w_in"][l], p["gla_w_lr"][l], p["gla_b_lr"][l])
    gla_in, = _mm(x, w_gla, [(w_gla.shape[1], F32)])
    ops, lmask = _gla_constants()
    return _gla(gla_in, wlr, blr, jnp.asarray(ops, BF16), jnp.asarray(lmask), batch, seq)


def _layer(x, kvm_src, p, l, batch, seq, alpha):
    w_in = p["w_in"][l]
    y_nsa = _nsa_branch(x, p, l, batch, seq)
    o_gdn = _gdn_branch(x, p, l, batch, seq)
    o_gla = _gla_branch(x, p, l, batch, seq)

    wzr = jnp.concatenate([w_in[:, O_GDN_Z:O_GDN_Z + MIX_W], w_in[:, O_GLA_R:O_GLA_R + MIX_W]], axis=1).astype(BF16)
    wmg = w_in[:, O_MERGE:O_MERGE + N_BRANCH * D_MODEL].astype(BF16)
    x = _merge(x, y_nsa, o_gdn, o_gla, wzr, wmg, p["gdn_norm"][l][None, :], p["gla_norm"][l][None, :],
               p["w_up"][l].astype(BF16), p["w_out"][l].astype(BF16),
               p["ln_g"][l, 0][None, :], p["ln_b"][l, 0][None, :], alpha)

    wkv = jnp.concatenate([p["xa_wk"][l], p["xa_wv"][l]], axis=1).astype(BF16)
    kvm, = _mm(kvm_src, wkv, [(2 * D_MODEL, BF16)])
    x = _xattn(x, kvm, p["xa_wq"][l].astype(BF16), p["xa_wo"][l].astype(BF16),
               p["ln_g"][l, 1][None, :], p["ln_b"][l, 1][None, :], alpha, batch, seq)

    return _mlp(x, p["mlp_w1"][l].astype(BF16), p["mlp_w2"][l].astype(BF16),
                p["ln_g"][l, 2][None, :], p["ln_b"][l, 2][None, :], alpha)


def kernel(x, mem, w_in, cmp_pe, cmp_w1, cmp_b1, cmp_w2, gdn_conv, gdn_a_log, gdn_dt_bias, gdn_norm, gla_w_lr,
           gla_b_lr, gla_norm, w_up, w_out, xa_wq, xa_wk, xa_wv, xa_wo, mlp_w1, mlp_w2, ln_g, ln_b):
    batch, seq, d = x.shape
    depth = w_in.shape[0]
    alpha = (2.0 * depth) ** 0.25
    p = dict(w_in=w_in, cmp_pe=cmp_pe, cmp_w1=cmp_w1, cmp_b1=cmp_b1, cmp_w2=cmp_w2, gdn_conv=gdn_conv,
             gdn_a_log=gdn_a_log, gdn_dt_bias=gdn_dt_bias, gdn_norm=gdn_norm, gla_w_lr=gla_w_lr,
             gla_b_lr=gla_b_lr, gla_norm=gla_norm, w_up=w_up, w_out=w_out, xa_wq=xa_wq, xa_wk=xa_wk,
             xa_wv=xa_wv, xa_wo=xa_wo, mlp_w1=mlp_w1, mlp_w2=mlp_w2, ln_g=ln_g, ln_b=ln_b)
    h = x.reshape(batch * seq, d)
    mem2 = mem.reshape(batch * mem.shape[1], d)
    for l in range(depth):
        h = _layer(h, mem2, p, l, batch, seq, alpha)
    return h.reshape(batch, seq, d)
```

```python
import functools

import numpy as np
import jax
import jax.numpy as jnp
from jax import lax
from jax.experimental import pallas as pl
from jax.experimental.pallas import tpu as pltpu

F32 = jnp.float32
BF16 = jnp.bfloat16

D_MODEL = 1024
MIX_W = D_MODEL // 2
N_BRANCH = 3
NSA_HEADS = 8
NSA_HD = MIX_W // NSA_HEADS
NSA_GROUPS = 2
NSA_HPG = NSA_HEADS // NSA_GROUPS
CMP_LEN = 32
CMP_STRIDE = 16
CMP_HID = D_MODEL // 4
SEL_BLK = 64
SEL_TOPK = 16
WIN = 512
Q_BLK = 256
GDN_HEADS = 4
GDN_HD = MIX_W // GDN_HEADS
CONV_K = 4
CHUNK = 64
GLA_HEADS = 4
GLA_DK = MIX_W // (2 * GLA_HEADS)
GLA_DV = MIX_W // GLA_HEADS
GLA_RANK = 16
GLA_TAU = 16.0
XA_HEADS = 4
XA_HD = D_MODEL // XA_HEADS
D_FF = 4 * D_MODEL
LN_EPS = 1e-5
NORM_EPS = 1e-6
NEG_INF = -1e30
FORCE_BONUS = 1e4

LANE = 128
SUBLANE = 8
VMEM_LIMIT_BYTES = 48 * 1024 * 1024
SEL_TILE = 512
NSA_ROWS = 16
NSA_RANGES = 4
REMOVED = -3e38
GLA_LEVELS = (32, 16, 8, 4, 2, 1)
Q_BLK_LOG2 = Q_BLK.bit_length() - 1
SEL_BLK_LOG2 = SEL_BLK.bit_length() - 1
CHUNK_LOG2 = CHUNK.bit_length() - 1

_IN_SIZES = (NSA_HEADS * NSA_HD, 6 * NSA_GROUPS * NSA_HD, NSA_HEADS * 3, 3 * MIX_W, GDN_HEADS, GDN_HEADS,
             MIX_W, GLA_HEADS * GLA_DK, GLA_HEADS * GLA_DK, GLA_HEADS * GLA_DV, GLA_RANK, MIX_W,
             N_BRANCH * D_MODEL)
_OFF = tuple(int(v) for v in np.cumsum((0,) + _IN_SIZES))
(O_NSA_Q, O_NSA_KV, O_NSA_G, O_GDN_QKV, O_GDN_A, O_GDN_B, O_GDN_Z, O_GLA_Q, O_GLA_K, O_GLA_V, O_GLA_LR,
 O_GLA_R, O_MERGE, _) = _OFF


def _params(*sem):
    return pltpu.CompilerParams(dimension_semantics=sem, vmem_limit_bytes=VMEM_LIMIT_BYTES)


def _dot(a, b):
    return jnp.dot(a, b, preferred_element_type=F32)


def _dot_nt(a, b):
    return lax.dot_general(a, b, (((1,), (1,)), ((), ())), preferred_element_type=F32)


def _dot_tn(a, b):
    return lax.dot_general(a, b, (((0,), (0,)), ((), ())), preferred_element_type=F32)


def _sigmoid(x):
    return 1.0 / (1.0 + jnp.exp(-x))


def _layer_norm(v, g, b):
    mu = jnp.mean(v, -1, keepdims=True)
    c = v - mu
    var = jnp.mean(c * c, -1, keepdims=True)
    return c * lax.rsqrt(var + LN_EPS) * g + b


def _split2(a):
    hi = a.astype(BF16)
    return hi, (a - hi.astype(F32)).astype(BF16)


def _dot_split(a, b):
    return _dot(a[0], b[0]) + (_dot(a[0], b[1]) + _dot(a[1], b[0]))


def _dot_exact_lhs(op, x):
    x1 = x.astype(BF16)
    r1 = x - x1.astype(F32)
    x2 = r1.astype(BF16)
    x3 = (r1 - x2.astype(F32)).astype(BF16)
    return _dot(op, x1) + (_dot(op, x2) + _dot(op, x3))


def _mm_body(x_ref, w_ref, *o_refs):
    res = _dot(x_ref[...].astype(BF16), w_ref[...])
    off = 0
    for o_ref in o_refs:
        o_ref[...] = res[:, off:off + o_ref.shape[1]].astype(o_ref.dtype)
        off += o_ref.shape[1]


def _mm(x, w, outs, tm=512):
    m, k = x.shape
    n = w.shape[1]
    assert n == sum(wd for wd, _ in outs) and all(wd % LANE == 0 for wd, _ in outs)
    tm = min(tm, m)
    return pl.pallas_call(
        _mm_body,
        grid=(m // tm,),
        in_specs=[pl.BlockSpec((tm, k), lambda i: (i, 0)),
                  pl.BlockSpec((k, n), lambda i: (0, 0))],
        out_specs=[pl.BlockSpec((tm, wd), lambda i: (i, 0)) for wd, _ in outs],
        out_shape=[jax.ShapeDtypeStruct((m, wd), dt) for wd, dt in outs],
        compiler_params=_params("parallel"),
        name="proj_mm",
    )(x, w)


def _compress_body(ch_ref, w1_ref, b1_ref, pe_ref, w2_ref, o_ref):
    nc = ch_ref.shape[0]
    ch = ch_ref[...]
    top = _dot(ch, w1_ref[0])
    bot = _dot(ch, w1_ref[1])
    bot_next = pltpu.roll(bot, nc - 1, axis=0)
    pe_term = _dot(pe_ref[0], w1_ref[0]) + _dot(pe_ref[1], w1_ref[1])
    h = top + bot_next + pe_term[0:1] + b1_ref[...]
    h = jax.nn.gelu(h, approximate=True)
    o_ref[...] = _dot(h.astype(BF16), w2_ref[...]).astype(o_ref.dtype)


def _compress(ch, w1, b1, pe, w2, batch):
    nc = ch.shape[1] // batch
    half = CMP_STRIDE * NSA_HD
    return pl.pallas_call(
        _compress_body,
        grid=(2, NSA_GROUPS, batch),
        in_specs=[pl.BlockSpec((None, nc, half), lambda t, g, b: (t * NSA_GROUPS + g, b, 0)),
                  pl.BlockSpec((None, 2, half, CMP_HID), lambda t, g, b: (t, 0, 0, 0)),
                  pl.BlockSpec((None, 1, CMP_HID), lambda t, g, b: (t, 0, 0)),
                  pl.BlockSpec((None, 2, SUBLANE, half), lambda t, g, b: (t, 0, 0, 0)),
                  pl.BlockSpec((None, CMP_HID, NSA_HD), lambda t, g, b: (t, 0, 0))],
        out_specs=pl.BlockSpec((None, nc, NSA_HD), lambda t, g, b: (t * NSA_GROUPS + g, b, 0)),
        out_shape=jax.ShapeDtypeStruct((2 * NSA_GROUPS, batch * nc, NSA_HD), BF16),
        compiler_params=_params("parallel", "parallel", "parallel"),
        name="nsa_compress",
    )(ch, w1, b1, pe, w2)


def _nsa_body(q_ref, kv_ref, kc_ref, vc_ref, gl_ref, ov_ref, o_ref, m_scr, acc_scr, tiles_ref, *,
              nc, ns, qb0):
    g = pl.program_id(1)
    qb = qb0 + pl.program_id(2)
    qs = qb * Q_BLK
    scale = NSA_HD ** -0.5
    n_chunks = Q_BLK // NSA_ROWS

    q4 = q_ref[...] * jnp.asarray(scale, q_ref.dtype)
    qst = jnp.concatenate([q4[:, h * NSA_HD:(h + 1) * NSA_HD] for h in range(NSA_HPG)], axis=0)
    slopes = [jnp.where(g == 0, 2.0 ** -(h + 1), 2.0 ** -(NSA_HPG + h + 1)).astype(F32) for h in range(NSA_HPG)]
    tq1 = qs + lax.broadcasted_iota(jnp.int32, (Q_BLK, 1), 0)

    def chunk_rows(h, rc):
        return slice(h * Q_BLK + rc * NSA_ROWS, h * Q_BLK + (rc + 1) * NSA_ROWS)

    order = [(h, rc) for h in range(NSA_HPG) for rc in range(n_chunks)]

    def stacked(parts):
        return jnp.concatenate([parts[i] for i in order], axis=0)

    def lane_blocks(x, op):
        out = x[:, 0:LANE]
        for j in range(1, x.shape[1] // LANE):
            out = op(out, x[:, j * LANE:(j + 1) * LANE])
        return out

    def lane_tiled(x, width):
        return jnp.concatenate([x] * (width // LANE), axis=1)

    def exp_pass(logit_parts, m_all, masks=None, want_sums=True):
        width = logit_parts[order[0]].shape[1]
        m_lanes = jnp.broadcast_to(m_all, (m_all.shape[0], LANE))
        e_parts, sums = {}, {}
        for h, rc in order:
            e = jnp.exp(logit_parts[h, rc] - lane_tiled(m_lanes[chunk_rows(h, rc)], width))
            if masks is not None:
                e = jnp.where(masks[rc], e, 0.0)
            e_parts[h, rc] = e
            if want_sums:
                sums[h, rc] = lane_blocks(e, jnp.add)
        return e_parts, (jnp.sum(stacked(sums), -1, keepdims=True) if want_sums else None)

    def masked_logits(s_all, key_pos, is_valid):
        parts, maxes, masks = {}, {}, {}
        for rc in range(n_chunks):
            dist = tq1[rc * NSA_ROWS:(rc + 1) * NSA_ROWS] - key_pos
            masks[rc] = is_valid(dist)
            d_f = dist.astype(F32)
            for h in range(NSA_HPG):
                s = jnp.where(masks[rc], s_all[chunk_rows(h, rc)] - slopes[h] * d_f, NEG_INF)
                parts[h, rc] = s
                maxes[h, rc] = lane_blocks(s, jnp.maximum)
        return parts, masks, jnp.max(stacked(maxes), -1, keepdims=True)

    c_end = lax.broadcasted_iota(jnp.int32, (1, nc), 1) * CMP_STRIDE + (CMP_LEN - 1)
    logit_c, mask_c, m_c = masked_logits(_dot_nt(qst, kc_ref[...]), c_end, lambda d: d >= 0)
    e_c, l_c = exp_pass(logit_c, m_c, mask_c)
    r_c = 1.0 / jnp.maximum(l_c, 1e-30)
    o_cmp = _dot(stacked({i: e_c[i].astype(BF16) for i in order}), vc_ref[...]) * r_c

    r_lanes = jnp.broadcast_to(r_c, (r_c.shape[0], LANE))
    p_hi, p_lo = [], []
    for rc in range(n_chunks):
        p_sum = None
        for h in range(NSA_HPG):
            p = e_c[h, rc] * lane_tiled(r_lanes[chunk_rows(h, rc)], nc)
            p_sum = p if p_sum is None else p_sum + p
        hi = p_sum.astype(BF16)
        p_hi.append(hi)
        p_lo.append((p_sum - hi.astype(F32)).astype(BF16))
    p_hi = jnp.concatenate(p_hi, axis=0)
    p_lo = jnp.concatenate(p_lo, axis=0)

    imp_t = (_dot(p_hi, ov_ref[...]) + _dot(p_lo, ov_ref[...])).T
    blk = lax.broadcasted_iota(jnp.int32, (ns, Q_BLK), 0)
    tq_row = qs + lax.broadcasted_iota(jnp.int32, (1, Q_BLK), 1)
    cur = tq_row >> SEL_BLK_LOG2
    forced = (blk == 0) | (blk == cur) | (blk == cur - 1)
    valid = blk * SEL_BLK <= tq_row
    score = jnp.where(valid, imp_t + jnp.where(forced, FORCE_BONUS, 0.0), NEG_INF)

    blk_l = blk[:, 0:LANE]

    def pick(_, carry):
        sc, sel = carry
        m = jnp.max(sc, 0, keepdims=True)
        idx = jnp.min(jnp.where(sc == m, blk_l, ns), 0, keepdims=True)
        hit = blk_l == idx
        return jnp.where(hit, REMOVED, sc), jnp.where(hit, 1.0, sel)

    sel_t = jnp.concatenate(
        [lax.fori_loop(0, min(SEL_TOPK, ns), pick,
                       (score[:, j * LANE:(j + 1) * LANE], jnp.zeros((ns, LANE), F32)))[1]
         for j in range(Q_BLK // LANE)], axis=1)
    sel = sel_t.T
    sel_bias = ((sel - 1.0) * -NEG_INF).astype(BF16)
    blk_any = jnp.max(sel, 0, keepdims=True)

    tile_blks = SEL_TILE // SEL_BLK
    blk_row = lax.broadcasted_iota(jnp.int32, (ns, SEL_TILE), 0)
    blk_of_key = lax.broadcasted_iota(jnp.int32, (ns, SEL_TILE), 1) >> SEL_BLK_LOG2
    key_in_tile = lax.broadcasted_iota(jnp.int32, (1, SEL_TILE), 1)
    m_scr[...] = jnp.full(m_scr.shape, NEG_INF, F32)
    acc_scr[...] = jnp.zeros(acc_scr.shape, F32)
    v_lane = lax.broadcasted_iota(jnp.int32, (1, 2 * NSA_HD), 1)
    one_b = jnp.ones((), kv_ref.dtype)

    tile_of_blk = lax.broadcasted_iota(jnp.int32, (ns, LANE), 0) >> (SEL_TILE.bit_length() - 1 - SEL_BLK_LOG2)
    in_tile = jnp.where(tile_of_blk == lax.broadcasted_iota(jnp.int32, (ns, LANE), 1), 1.0, 0.0).astype(BF16)
    tile_cnt = _dot(jnp.broadcast_to(blk_any, (SUBLANE, ns)).astype(BF16), in_tile)
    n_visible = (qs + Q_BLK + SEL_TILE - 1) // SEL_TILE
    n_active = jnp.int32(0)
    for kt in range(ns // tile_blks):
        tiles_ref[n_active] = kt
        n_active = n_active + ((tile_cnt[0, kt] > 0.0) & (kt < n_visible)).astype(jnp.int32)

    def sel_tile(j, carry):
        kt = tiles_ref[j]
        k0 = pl.multiple_of(kt * SEL_TILE, SEL_TILE)
        st_all = _dot_nt(qst, kv_ref[pl.ds(k0, SEL_TILE), 0:NSA_HD])
        expand = jnp.where(blk_row == blk_of_key + kt * tile_blks, 1.0, 0.0).astype(BF16)
        key_pos = k0 + key_in_tile
        bias_all = jnp.where(tq1 - key_pos >= 0, _dot(sel_bias, expand), NEG_INF)
        logits, maxes = {}, {}
        for rc in range(n_chunks):
            qr = slice(rc * NSA_ROWS, (rc + 1) * NSA_ROWS)
            d_f = (tq1[qr] - key_pos).astype(F32)
            for h in range(NSA_HPG):
                s = st_all[chunk_rows(h, rc)] + (bias_all[qr] - slopes[h] * d_f)
                logits[h, rc] = s
                maxes[h, rc] = lane_blocks(s, jnp.maximum)
        m_old = m_scr[...]
        m_new = jnp.maximum(m_old, jnp.max(stacked(maxes), -1, keepdims=True))
        p_parts, _ = exp_pass(logits, m_new, want_sums=False)
        alpha = jnp.exp(m_old - m_new)
        m_scr[...] = m_new
        v_one = jnp.where(v_lane < NSA_HD, kv_ref[pl.ds(k0, SEL_TILE), 2 * NSA_HD:4 * NSA_HD], one_b)
        p_all = stacked({i: p_parts[i].astype(BF16) for i in order})
        acc_scr[...] = alpha * acc_scr[...] + _dot(p_all, v_one)
        return carry

    lax.fori_loop(0, n_active, sel_tile, 0)
    acc = acc_scr[...]
    o_slc = acc[:, 0:NSA_HD] * (1.0 / jnp.maximum(acc[:, NSA_HD:NSA_HD + 1], 1e-30))

    wlen = WIN + Q_BLK
    w0 = pl.multiple_of(jnp.maximum(qs - WIN, 0), Q_BLK)
    pos_w = w0 + lax.broadcasted_iota(jnp.int32, (1, wlen), 1)
    logit_w, _, m_w = masked_logits(_dot_nt(qst, kv_ref[pl.ds(w0, wlen), NSA_HD:2 * NSA_HD]), pos_w,
                                    lambda d: (d >= 0) & (d < WIN))
    e_w, l_w = exp_pass(logit_w, m_w)
    o_win = (_dot(stacked({i: e_w[i].astype(BF16) for i in order}), kv_ref[pl.ds(w0, wlen), 3 * NSA_HD:4 * NSA_HD])
             * (1.0 / jnp.maximum(l_w, 1e-30)))

    gates = _sigmoid(gl_ref[...])
    outs = []
    for h in range(NSA_HPG):
        r = slice(h * Q_BLK, (h + 1) * Q_BLK)
        outs.append(gates[:, 3 * h:3 * h + 1] * o_cmp[r] + gates[:, 3 * h + 1:3 * h + 2] * o_slc[r]
                    + gates[:, 3 * h + 2:3 * h + 3] * o_win[r])
    o_ref[...] = jnp.concatenate(outs, axis=1).astype(o_ref.dtype)


def _round_up(v, m):
    return (v + m - 1) // m * m


def _nsa_attention(main, gate_logits, cmp_tok, overlap, batch, seq):
    nq = seq // Q_BLK
    gw = NSA_HPG * NSA_HD
    hq = NSA_HPG * Q_BLK
    n_ranges = min(NSA_RANGES, nq)
    per = nq // n_ranges
    outs = []
    for r in range(n_ranges):
        q0 = r * per
        seen = (q0 + per) * Q_BLK
        nc = min(seq // CMP_STRIDE, _round_up(seen // CMP_STRIDE, LANE))
        ns = min(seq // SEL_BLK, _round_up(seen // SEL_BLK, LANE))
        out = pl.pallas_call(
            functools.partial(_nsa_body, nc=nc, ns=ns, qb0=q0),
            grid=(batch, NSA_GROUPS, per),
            in_specs=[pl.BlockSpec((Q_BLK, gw), lambda b, g, i, q0=q0: (b * nq + q0 + i, g)),
                      pl.BlockSpec((seq, gw), lambda b, g, i: (b, NSA_GROUPS + g)),
                      pl.BlockSpec((None, None, nc, NSA_HD), lambda b, g, i: (g, b, 0, 0)),
                      pl.BlockSpec((None, None, nc, NSA_HD), lambda b, g, i: (NSA_GROUPS + g, b, 0, 0)),
                      pl.BlockSpec((Q_BLK, LANE), lambda b, g, i, q0=q0: (b * nq + q0 + i, g)),
                      pl.BlockSpec((nc, ns), lambda b, g, i: (0, 0))],
            out_specs=pl.BlockSpec((Q_BLK, gw), lambda b, g, i: (b * per + i, g)),
            out_shape=jax.ShapeDtypeStruct((batch * per * Q_BLK, MIX_W), BF16),
            scratch_shapes=[pltpu.VMEM((hq, 1), F32), pltpu.VMEM((hq, 2 * NSA_HD), F32),
                            pltpu.SMEM((ns * SEL_BLK // SEL_TILE,), jnp.int32)],
            compiler_params=_params("parallel", "parallel", "arbitrary"),
            name="nsa_attention",
        )(main, main, cmp_tok, cmp_tok, gate_logits, overlap)
        outs.append(out.reshape(batch, per * Q_BLK, MIX_W))
    return jnp.concatenate(outs, axis=1).reshape(batch * seq, MIX_W)


def _unit_lower_inverse(a):
    n = a.shape[0]
    eye = (lax.broadcasted_iota(jnp.int32, (n, n), 0) == lax.broadcasted_iota(jnp.int32, (n, n), 1)).astype(F32)
    x = _split2(eye - a)
    p = _split2(a)
    power = 1
    while 2 * power < CHUNK:
        p = _split2(_dot_split(p, p))
        x = _split2((x[0].astype(F32) + x[1].astype(F32)) + _dot_split(x, p))
        power *= 2
    return x


def _gdn_body(raw_ref, ab_ref, conv_ref, alog_ref, dtb_ref, o_ref, s_ref, tail_ref, buf_ref, *, tb):
    @pl.when(pl.program_id(1) == 0)
    def _():
        s_ref[...] = jnp.zeros_like(s_ref)
        tail_ref[...] = jnp.zeros_like(tail_ref)

    raw = raw_ref[...]
    buf_ref[0:SUBLANE] = tail_ref[...]
    buf_ref[SUBLANE:SUBLANE + tb] = raw
    tail_ref[...] = raw[tb - SUBLANE:tb]
    x = jnp.zeros_like(raw)
    for k in range(CONV_K):
        x = x + buf_ref[pl.ds(SUBLANE - (CONV_K - 1) + k, tb)] * conv_ref[k:k + 1]
    x = x * _sigmoid(x)

    ab = ab_ref[...]
    zed = ab + dtb_ref[...]
    softplus = jnp.maximum(zed, 0.0) + jnp.log1p(jnp.exp(-jnp.abs(zed)))
    g_all = -jnp.exp(alog_ref[...]) * softplus
    beta_all = _sigmoid(ab)

    c2 = 2 * CHUNK
    ri = lax.broadcasted_iota(jnp.int32, (c2, c2), 0)
    ci = lax.broadcasted_iota(jnp.int32, (c2, c2), 1)
    tri2 = jnp.where((ri >= ci) & ((ri >> CHUNK_LOG2) == (ci >> CHUNK_LOG2)), 1.0, 0.0).astype(BF16)
    hc = GDN_HEADS * CHUNK
    rs = lax.broadcasted_iota(jnp.int32, (hc, hc), 0)
    cs = lax.broadcasted_iota(jnp.int32, (hc, hc), 1)
    same_head = (rs >> CHUNK_LOG2) == (cs >> CHUNK_LOG2)
    tri = same_head & (rs >= cs)
    strict = same_head & (rs > cs)
    qscale = GDN_HD ** -0.5

    def l2n(v):
        return v * lax.rsqrt(jnp.sum(v * v, -1, keepdims=True) + NORM_EPS)

    def stack(fn):
        return jnp.concatenate([fn(h) for h in range(GDN_HEADS)], axis=0)

    for pair in range(tb // c2):
        pr = slice(pair * c2, (pair + 1) * c2)
        gc2 = _dot_exact_lhs(tri2, g_all[pr])
        gc2t = gc2.T
        for half in range(2):
            rows = slice(pair * c2 + half * CHUNK, pair * c2 + (half + 1) * CHUNK)
            hr = slice(half * CHUNK, (half + 1) * CHUNK)
            gcol = stack(lambda h: gc2[hr, h:h + 1])
            grow = jnp.concatenate([gc2t[h:h + 1, hr] for h in range(GDN_HEADS)], axis=1)
            g_last = stack(lambda h: jnp.broadcast_to(gc2[hr, h:h + 1][CHUNK - 1:CHUNK], (CHUNK, 1)))
            beta = stack(lambda h: beta_all[rows, GDN_HEADS + h:GDN_HEADS + h + 1])
            qs_ = stack(lambda h: l2n(x[rows, h * GDN_HD:(h + 1) * GDN_HD])) * qscale
            ks_ = stack(lambda h: l2n(x[rows, MIX_W + h * GDN_HD:MIX_W + (h + 1) * GDN_HD]))
            vs_ = stack(lambda h: x[rows, 2 * MIX_W + h * GDN_HD:2 * MIX_W + (h + 1) * GDN_HD])
            decay = jnp.where(tri, jnp.exp(jnp.where(tri, gcol - grow, 0.0)), 0.0)
            kb = ks_ * beta
            ks_b = ks_.astype(BF16)
            a_mat = jnp.where(strict, _dot_nt(kb.astype(BF16), ks_b) * decay, 0.0)
            t_inv = _unit_lower_inverse(a_mat)
            egc = jnp.exp(gcol)
            sol = _dot_split(t_inv, _split2(jnp.concatenate([vs_ * beta, kb * egc], axis=1)))
            qk = jnp.where(tri, _dot_nt(qs_.astype(BF16), ks_b) * decay, 0.0).astype(BF16)
            q_dec = (qs_ * egc).astype(BF16)
            k_tail = (ks_ * jnp.exp(g_last - gcol)).astype(BF16)
            a_last = jnp.exp(g_last)
            states = [s_ref[h] for h in range(GDN_HEADS)]
            states_b = [st.astype(BF16) for st in states]
            hrows = [slice(h * CHUNK, (h + 1) * CHUNK) for h in range(GDN_HEADS)]
            v_new = stack(lambda h: sol[hrows[h], :GDN_HD]
                          - _dot(sol[hrows[h], GDN_HD:].astype(BF16), states_b[h]))
            v_new_b = v_new.astype(BF16)
            o = _dot(qk, v_new_b) + stack(lambda h: _dot(q_dec[hrows[h]], states_b[h]))
            for h in range(GDN_HEADS):
                s_ref[h] = (states[h] * a_last[hrows[h]][0:1] + _dot_tn(k_tail[hrows[h]], v_new_b[hrows[h]]))
                o_ref[rows, h * GDN_HD:(h + 1) * GDN_HD] = o[hrows[h]]


def _gdn(raw, ab, conv, alog, dtb, batch, seq, tb=256):
    nt = seq // tb
    return pl.pallas_call(
        functools.partial(_gdn_body, tb=tb),
        grid=(batch, nt),
        in_specs=[pl.BlockSpec((tb, 3 * MIX_W), lambda b, i: (b * nt + i, 0)),
                  pl.BlockSpec((tb, LANE), lambda b, i: (b * nt + i, 0)),
                  pl.BlockSpec((CONV_K, 3 * MIX_W), lambda b, i: (0, 0)),
                  pl.BlockSpec((1, LANE), lambda b, i: (0, 0)),
                  pl.BlockSpec((1, LANE), lambda b, i: (0, 0))],
        out_specs=pl.BlockSpec((tb, MIX_W), lambda b, i: (b * nt + i, 0)),
        out_shape=jax.ShapeDtypeStruct((batch * seq, MIX_W), F32),
        scratch_shapes=[pltpu.VMEM((GDN_HEADS, GDN_HD, GDN_HD), F32),
                        pltpu.VMEM((SUBLANE, 3 * MIX_W), F32),
                        pltpu.VMEM((SUBLANE + tb, 3 * MIX_W), F32)],
        compiler_params=_params("parallel", "arbitrary"),
        name="gated_delta_net",
    )(raw, ab, conv, alog, dtb)


def _gla_constants():
    c = CHUNK
    t = np.arange(c)[:, None]
    s = np.arange(c)[None, :]
    ops = [(s <= t), (s > t)]
    masks = []
    for m in GLA_LEVELS:
        r = (t // (2 * m)) * (2 * m) + m
        upper = (t % (2 * m)) >= m
        ops.append(np.where(upper, (s > r) & (s <= t), (s > t) & (s <= r)))
        i, j = t, s
        masks.append((i // (2 * m) == j // (2 * m)) & ((i % (2 * m)) >= m) & ((j % (2 * m)) < m))
    return (np.concatenate(ops, 0).astype(np.float32), np.stack(masks).astype(np.float32))


def _gla_body(x_ref, wlr_ref, blr_ref, ops_ref, lm_ref, o_ref, s_ref, *, tb):
    @pl.when(pl.program_id(1) == 0)
    def _():
        s_ref[...] = jnp.zeros_like(s_ref)

    hw = GLA_DV
    kw = GLA_HEADS * hw
    r1 = lax.broadcasted_iota(jnp.int32, (CHUNK, CHUNK), 0)
    c1 = lax.broadcasted_iota(jnp.int32, (CHUNK, CHUNK), 1)
    eye = r1 == c1
    qscale = GLA_DK ** -0.5
    for c in range(tb // CHUNK):
        rows = slice(c * CHUNK, (c + 1) * CHUNK)
        lr = x_ref[rows, 3 * kw:3 * kw + LANE]
        z = _dot(lr, wlr_ref[...]) + blr_ref[...]
        log_a = (jnp.minimum(z, 0.0) - jnp.log1p(jnp.exp(-jnp.abs(z)))) / GLA_TAU
        e_all = jnp.exp(_dot_exact_lhs(ops_ref[...], log_a))
        for h in range(GLA_HEADS):
            hs = slice(h * hw, (h + 1) * hw)
            qh = x_ref[rows, hs] * qscale
            kh = x_ref[rows, kw + h * hw:kw + (h + 1) * hw]
            vh = x_ref[rows, 2 * kw + h * hw:2 * kw + (h + 1) * hw]
            scores = jnp.where(eye, _dot_nt(qh.astype(BF16), kh.astype(BF16)), 0.0)
            for lvl in range(len(GLA_LEVELS)):
                xl = e_all[(2 + lvl) * CHUNK:(3 + lvl) * CHUNK, hs]
                scores = scores + lm_ref[lvl] * _dot_nt((qh * xl).astype(BF16), (kh * xl).astype(BF16))
            e_cum = e_all[0:CHUNK, hs]
            e_tail = e_all[CHUNK:2 * CHUNK, hs]
            state_t = s_ref[h]
            vb = vh.astype(BF16)
            o = _dot_nt((qh * e_cum).astype(BF16), state_t.astype(BF16)) + _dot(scores.astype(BF16), vb)
            s_ref[h] = state_t * e_cum[CHUNK - 1:CHUNK] + _dot_tn(vb, (kh * e_tail).astype(BF16))
            o_ref[rows, hs] = o


def _gla(x, wlr, blr, ops, lmask, batch, seq, tb=256):
    nt = seq // tb
    width = x.shape[1]
    return pl.pallas_call(
        functools.partial(_gla_body, tb=tb),
        grid=(batch, nt),
        in_specs=[pl.BlockSpec((tb, width), lambda b, i: (b * nt + i, 0)),
                  pl.BlockSpec(wlr.shape, lambda b, i: (0, 0)),
                  pl.BlockSpec(blr.shape, lambda b, i: (0, 0)),
                  pl.BlockSpec(ops.shape, lambda b, i: (0, 0)),
                  pl.BlockSpec(lmask.shape, lambda b, i: (0, 0, 0))],
        out_specs=pl.BlockSpec((tb, MIX_W), lambda b, i: (b * nt + i, 0)),
        out_shape=jax.ShapeDtypeStruct((batch * seq, MIX_W), F32),
        scratch_shapes=[pltpu.VMEM((GLA_HEADS, GLA_DV, GLA_DV), F32)],
        compiler_params=_params("parallel", "arbitrary"),
        name="gla",
    )(x, wlr, blr, ops, lmask)


def _merge_body(x_ref, yn_ref, og_ref, ol_ref, wzr_ref, wmg_ref, ng_ref, nl_ref, wup_ref, wout_ref,
                lg_ref, lb_ref, o_ref, *, alpha):
    x = x_ref[...]
    xb = x.astype(BF16)
    zr = _dot(xb, wzr_ref[...])

    def head_rms(o, gain):
        parts = []
        for h in range(MIX_W // LANE):
            oh = o[:, h * LANE:(h + 1) * LANE]
            parts.append(oh * lax.rsqrt(jnp.mean(oh * oh, -1, keepdims=True) + NORM_EPS) * gain)
        return jnp.concatenate(parts, axis=1)

    z = zr[:, :MIX_W]
    r = zr[:, MIX_W:]
    y_gdn = head_rms(og_ref[...], ng_ref[...]) * (z * _sigmoid(z))
    y_gla = head_rms(ol_ref[...], nl_ref[...]) * (r * _sigmoid(r))
    ys = (yn_ref[...], y_gdn.astype(BF16), y_gla.astype(BF16))
    merged = jnp.zeros(x.shape, F32)
    for br in range(N_BRANCH):
        gate = _sigmoid(_dot(xb, wmg_ref[:, br * D_MODEL:(br + 1) * D_MODEL]))
        merged = merged + gate * _dot(ys[br], wup_ref[br])
    h = _dot(merged.astype(BF16), wout_ref[...])
    o_ref[...] = _layer_norm(alpha * x + h, lg_ref[...], lb_ref[...])


def _const_spec(shape):
    nd = len(shape)
    return pl.BlockSpec(shape, lambda i: (0,) * nd, pipeline_mode=pl.Buffered(1))


def _merge(x, y_nsa, o_gdn, o_gla, wzr, wmg, ng, nl, wup, wout, lg, lb, alpha, tm=256):
    m = x.shape[0]
    row = lambda w: pl.BlockSpec((tm, w), lambda i: (i, 0))
    return pl.pallas_call(
        functools.partial(_merge_body, alpha=alpha),
        grid=(m // tm,),
        in_specs=[row(D_MODEL), row(MIX_W), row(MIX_W), row(MIX_W),
                  _const_spec(wzr.shape), _const_spec(wmg.shape), _const_spec(ng.shape), _const_spec(nl.shape),
                  _const_spec(wup.shape), _const_spec(wout.shape), _const_spec(lg.shape), _const_spec(lb.shape)],
        out_specs=row(D_MODEL),
        out_shape=jax.ShapeDtypeStruct((m, D_MODEL), F32),
        compiler_params=_params("parallel"),
        name="mixer_merge",
    )(x, y_nsa, o_gdn, o_gla, wzr, wmg, ng, nl, wup, wout, lg, lb)


def _xattn_body(x_ref, kv_ref, wq_ref, wo_ref, lg_ref, lb_ref, o_ref, *, alpha):
    x = x_ref[...]
    q = _dot(x.astype(BF16), wq_ref[...]).astype(BF16)
    scale = XA_HD ** -0.5
    outs = []
    for h in range(XA_HEADS):
        hs = slice(h * XA_HD, (h + 1) * XA_HD)
        s = _dot_nt(q[:, hs], kv_ref[:, hs]) * scale
        m = jnp.max(s, -1, keepdims=True)
        e = jnp.exp(s - m)
        p = e / jnp.sum(e, -1, keepdims=True)
        outs.append(_dot(p.astype(BF16), kv_ref[:, D_MODEL + h * XA_HD:D_MODEL + (h + 1) * XA_HD]))
    o = jnp.concatenate(outs, axis=1)
    h_out = _dot(o.astype(BF16), wo_ref[...])
    o_ref[...] = _layer_norm(alpha * x + h_out, lg_ref[...], lb_ref[...])


def _xattn(x, kvm, wq, wo, lg, lb, alpha, batch, seq, tm=512):
    nt = seq // tm
    mt = kvm.shape[0] // batch
    cs = lambda shape: pl.BlockSpec(shape, lambda b, i: (0,) * len(shape), pipeline_mode=pl.Buffered(1))
    return pl.pallas_call(
        functools.partial(_xattn_body, alpha=alpha),
        grid=(batch, nt),
        in_specs=[pl.BlockSpec((tm, D_MODEL), lambda b, i: (b * nt + i, 0)),
                  pl.BlockSpec((mt, 2 * D_MODEL), lambda b, i: (b, 0)),
                  cs(wq.shape), cs(wo.shape), cs(lg.shape), cs(lb.shape)],
        out_specs=pl.BlockSpec((tm, D_MODEL), lambda b, i: (b * nt + i, 0)),
        out_shape=jax.ShapeDtypeStruct((batch * seq, D_MODEL), F32),
        compiler_params=_params("parallel", "parallel"),
        name="mem_xattn",
    )(x, kvm, wq, wo, lg, lb)


def _mlp_body(x_ref, w1_ref, w2_ref, lg_ref, lb_ref, o_ref, *, alpha):
    x = x_ref[...]
    h = jnp.maximum(_dot(x.astype(BF16), w1_ref[...]), 0.0)
    h = (h * h).astype(BF16)
    o_ref[...] = _layer_norm(alpha * x + _dot(h, w2_ref[...]), lg_ref[...], lb_ref[...])


def _mlp(x, w1, w2, lg, lb, alpha, tm=512):
    m = x.shape[0]
    return pl.pallas_call(
        functools.partial(_mlp_body, alpha=alpha),
        grid=(m // tm,),
        in_specs=[pl.BlockSpec((tm, D_MODEL), lambda i: (i, 0)),
                  _const_spec(w1.shape), _const_spec(w2.shape), _const_spec(lg.shape), _const_spec(lb.shape)],
        out_specs=pl.BlockSpec((tm, D_MODEL), lambda i: (i, 0)),
        out_shape=jax.ShapeDtypeStruct((m, D_MODEL), F32),
        compiler_params=_params("parallel"),
        name="sq_relu_mlp",
    )(x, w1, w2, lg, lb)


def _pad_cols(w, width):
    return jnp.pad(w, ((0, 0), (0, width - w.shape[1])))


def _nsa_weights(w_in):
    def kv(kind, g):
        o = O_NSA_KV + (kind * NSA_GROUPS + g) * NSA_HD
        return w_in[:, o:o + NSA_HD]
    cols = [w_in[:, O_NSA_Q:O_NSA_Q + NSA_HEADS * NSA_HD]]
    for g in range(NSA_GROUPS):
        cols += [kv(2, g), kv(4, g), kv(3, g), kv(5, g)]
    cols += [kv(0, 0), kv(0, 1), kv(1, 0), kv(1, 1)]
    main = jnp.concatenate(cols, axis=1).astype(BF16)
    gw = NSA_HPG * 3
    gates = [_pad_cols(w_in[:, O_NSA_G + g * gw:O_NSA_G + (g + 1) * gw], LANE) for g in range(NSA_GROUPS)]
    return main, jnp.concatenate(gates, axis=1).astype(BF16)


def _gla_weights(w_in, w_lr, b_lr):
    def pad_heads(w):
        lead = w.shape[0]
        w = w.reshape(lead, GLA_HEADS, GLA_DK)
        return jnp.pad(w, ((0, 0), (0, 0), (0, GLA_DV - GLA_DK))).reshape(lead, GLA_HEADS * GLA_DV)
    cols = [pad_heads(w_in[:, O_GLA_Q:O_GLA_Q + GLA_HEADS * GLA_DK]),
            pad_heads(w_in[:, O_GLA_K:O_GLA_K + GLA_HEADS * GLA_DK]),
            w_in[:, O_GLA_V:O_GLA_V + GLA_HEADS * GLA_DV],
            _pad_cols(w_in[:, O_GLA_LR:O_GLA_LR + GLA_RANK], LANE)]
    wlr = jnp.pad(pad_heads(w_lr), ((0, LANE - GLA_RANK), (0, 0)))
    return jnp.concatenate(cols, axis=1).astype(BF16), wlr, pad_heads(b_lr[None, :])


def _overlap_matrix(seq):
    nc = seq // CMP_STRIDE
    ns = seq // SEL_BLK
    c_start = np.arange(nc)[:, None] * CMP_STRIDE
    s_start = np.arange(ns)[None, :] * SEL_BLK
    return jnp.asarray((c_start < s_start + SEL_BLK) & (c_start + CMP_LEN > s_start), BF16)


def _nsa_branch(x, p, l, batch, seq):
    w_in = p["w_in"][l]
    w_main, w_gates = _nsa_weights(w_in)
    cmp_w = 2 * NSA_GROUPS * NSA_HD
    qkv, cmp_src, gate_logits = _mm(x, jnp.concatenate([w_main, w_gates], axis=1),
                                    [(2 * MIX_W, BF16), (cmp_w, BF16), (NSA_GROUPS * LANE, F32)])
    nc = seq // CMP_STRIDE
    half = CMP_STRIDE * NSA_HD
    cmp_in = cmp_src.reshape(batch * seq, 2 * NSA_GROUPS, NSA_HD).transpose(1, 0, 2)
    cmp_in = cmp_in.reshape(2 * NSA_GROUPS, batch * nc, half)
    w1 = p["cmp_w1"][l].reshape(2, 2, half, CMP_HID).astype(BF16)
    pe = jnp.broadcast_to(p["cmp_pe"][l].reshape(2, 2, 1, half), (2, 2, SUBLANE, half)).astype(BF16)
    cmp_tok = _compress(cmp_in, w1, p["cmp_b1"][l][:, None, :], pe, p["cmp_w2"][l].astype(BF16), batch)
    cmp_tok = cmp_tok.reshape(2 * NSA_GROUPS, batch, nc, NSA_HD)
    return _nsa_attention(qkv, gate_logits, cmp_tok, _overlap_matrix(seq), batch, seq)


def _gdn_branch(x, p, l, batch, seq):
    w_in = p["w_in"][l]
    w_gdn = jnp.concatenate([w_in[:, O_GDN_QKV:O_GDN_QKV + 3 * MIX_W],
                             _pad_cols(w_in[:, O_GDN_A:O_GDN_A + 2 * GDN_HEADS], LANE)], axis=1).astype(BF16)
    raw, ab = _mm(x, w_gdn, [(3 * MIX_W, F32), (LANE, F32)])
    alog = _pad_cols(p["gdn_a_log"][l][None, :], LANE)
    dtb = _pad_cols(p["gdn_dt_bias"][l][None, :], LANE)
    return _gdn(raw, ab, p["gdn_conv"][l], alog, dtb, batch, seq)


def _gla_branch(x, p, l, batch, seq):
    w_gla, wlr, blr = _gla_weights(p["w_in"][l], p["gla_w_lr"][l], p["gla_b_lr"][l])
    gla_in, = _mm(x, w_gla, [(w_gla.shape[1], F32)])
    ops, lmask = _gla_constants()
    return _gla(gla_in, wlr, blr, jnp.asarray(ops, BF16), jnp.asarray(lmask), batch, seq)


def _layer(x, kvm_src, p, l, batch, seq, alpha):
    w_in = p["w_in"][l]
    y_nsa = _nsa_branch(x, p, l, batch, seq)
    o_gdn = _gdn_branch(x, p, l, batch, seq)
    o_gla = _gla_branch(x, p, l, batch, seq)

    wzr = jnp.concatenate([w_in[:, O_GDN_Z:O_GDN_Z + MIX_W], w_in[:, O_GLA_R:O_GLA_R + MIX_W]], axis=1).astype(BF16)
    wmg = w_in[:, O_MERGE:O_MERGE + N_BRANCH * D_MODEL].astype(BF16)
    x = _merge(x, y_nsa, o_gdn, o_gla, wzr, wmg, p["gdn_norm"][l][None, :], p["gla_norm"][l][None, :],
               p["w_up"][l].astype(BF16), p["w_out"][l].astype(BF16),
               p["ln_g"][l, 0][None, :], p["ln_b"][l, 0][None, :], alpha)

    wkv = jnp.concatenate([p["xa_wk"][l], p["xa_wv"][l]], axis=1).astype(BF16)
    kvm, = _mm(kvm_src, wkv, [(2 * D_MODEL, BF16)])
    x = _xattn(x, kvm, p["xa_wq"][l].astype(BF16), p["xa_wo"][l].astype(BF16),
               p["ln_g"][l, 1][None, :], p["ln_b"][l, 1][None, :], alpha, batch, seq)

    return _mlp(x, p["mlp_w1"][l].astype(BF16), p["mlp_w2"][l].astype(BF16),
                p["ln_g"][l, 2][None, :], p["ln_b"][l, 2][None, :], alpha)


def kernel(x, mem, w_in, cmp_pe, cmp_w1, cmp_b1, cmp_w2, gdn_conv, gdn_a_log, gdn_dt_bias, gdn_norm, gla_w_lr,
           gla_b_lr, gla_norm, w_up, w_out, xa_wq, xa_wk, xa_wv, xa_wo, mlp_w1, mlp_w2, ln_g, ln_b):
    batch, seq, d = x.shape
    depth = w_in.shape[0]
    alpha = (2.0 * depth) ** 0.25
    p = dict(w_in=w_in, cmp_pe=cmp_pe, cmp_w1=cmp_w1, cmp_b1=cmp_b1, cmp_w2=cmp_w2, gdn_conv=gdn_conv,
             gdn_a_log=gdn_a_log, gdn_dt_bias=gdn_dt_bias, gdn_norm=gdn_norm, gla_w_lr=gla_w_lr,
             gla_b_lr=gla_b_lr, gla_norm=gla_norm, w_up=w_up, w_out=w_out, xa_wq=xa_wq, xa_wk=xa_wk,
             xa_wv=xa_wv, xa_wo=xa_wo, mlp_w1=mlp_w1, mlp_w2=mlp_w2, ln_g=ln_g, ln_b=ln_b)
    h = x.reshape(batch * seq, d)
    mem2 = mem.reshape(batch * mem.shape[1], d)
    for l in range(depth):
        h = _layer(h, mem2, p, l, batch, seq, alpha)
    return h.reshape(batch, seq, d)
```

```python
import functools

import numpy as np
import jax
import jax.numpy as jnp
from jax import lax
from jax.experimental import pallas as pl
from jax.experimental.pallas import tpu as pltpu

F32 = jnp.float32
BF16 = jnp.bfloat16

D_MODEL = 1024
MIX_W = D_MODEL // 2
N_BRANCH = 3
NSA_HEADS = 8
NSA_HD = MIX_W // NSA_HEADS
NSA_GROUPS = 2
NSA_HPG = NSA_HEADS // NSA_GROUPS
CMP_LEN = 32
CMP_STRIDE = 16
CMP_HID = D_MODEL // 4
SEL_BLK = 64
SEL_TOPK = 16
WIN = 512
Q_BLK = 256
GDN_HEADS = 4
GDN_HD = MIX_W // GDN_HEADS
CONV_K = 4
CHUNK = 64
GLA_HEADS = 4
GLA_DK = MIX_W // (2 * GLA_HEADS)
GLA_DV = MIX_W // GLA_HEADS
GLA_RANK = 16
GLA_TAU = 16.0
XA_HEADS = 4
XA_HD = D_MODEL // XA_HEADS
D_FF = 4 * D_MODEL
LN_EPS = 1e-5
NORM_EPS = 1e-6
NEG_INF = -1e30
FORCE_BONUS = 1e4

LANE = 128
SUBLANE = 8
VMEM_LIMIT_BYTES = 48 * 1024 * 1024
SEL_TILE = 512
NSA_ROWS = 16
NSA_RANGES = 4
REMOVED = -3e38
GLA_LEVELS = (32, 16, 8, 4, 2, 1)
Q_BLK_LOG2 = Q_BLK.bit_length() - 1
SEL_BLK_LOG2 = SEL_BLK.bit_length() - 1
CHUNK_LOG2 = CHUNK.bit_length() - 1

_IN_SIZES = (NSA_HEADS * NSA_HD, 6 * NSA_GROUPS * NSA_HD, NSA_HEADS * 3, 3 * MIX_W, GDN_HEADS, GDN_HEADS,
             MIX_W, GLA_HEADS * GLA_DK, GLA_HEADS * GLA_DK, GLA_HEADS * GLA_DV, GLA_RANK, MIX_W,
             N_BRANCH * D_MODEL)
_OFF = tuple(int(v) for v in np.cumsum((0,) + _IN_SIZES))
(O_NSA_Q, O_NSA_KV, O_NSA_G, O_GDN_QKV, O_GDN_A, O_GDN_B, O_GDN_Z, O_GLA_Q, O_GLA_K, O_GLA_V, O_GLA_LR,
 O_GLA_R, O_MERGE, _) = _OFF


def _params(*sem):
    return pltpu.CompilerParams(dimension_semantics=sem, vmem_limit_bytes=VMEM_LIMIT_BYTES)


def _dot(a, b):
    return jnp.dot(a, b, preferred_element_type=F32)


def _dot_nt(a, b):
    return lax.dot_general(a, b, (((1,), (1,)), ((), ())), preferred_element_type=F32)


def _dot_tn(a, b):
    return lax.dot_general(a, b, (((0,), (0,)), ((), ())), preferred_element_type=F32)


def _sigmoid(x):
    return 1.0 / (1.0 + jnp.exp(-x))


def _layer_norm(v, g, b):
    mu = jnp.mean(v, -1, keepdims=True)
    c = v - mu
    var = jnp.mean(c * c, -1, keepdims=True)
    return c * lax.rsqrt(var + LN_EPS) * g + b


def _split2(a):
    hi = a.astype(BF16)
    return hi, (a - hi.astype(F32)).astype(BF16)


def _dot_split(a, b):
    return _dot(a[0], b[0]) + (_dot(a[0], b[1]) + _dot(a[1], b[0]))


def _dot_exact_lhs(op, x):
    x1 = x.astype(BF16)
    r1 = x - x1.astype(F32)
    x2 = r1.astype(BF16)
    x3 = (r1 - x2.astype(F32)).astype(BF16)
    return _dot(op, x1) + (_dot(op, x2) + _dot(op, x3))


def _mm_body(x_ref, w_ref, *o_refs):
    res = _dot(x_ref[...].astype(BF16), w_ref[...])
    off = 0
    for o_ref in o_refs:
        o_ref[...] = res[:, off:off + o_ref.shape[1]].astype(o_ref.dtype)
        off += o_ref.shape[1]


def _mm(x, w, outs, tm=512):
    m, k = x.shape
    n = w.shape[1]
    assert n == sum(wd for wd, _ in outs) and all(wd % LANE == 0 for wd, _ in outs)
    tm = min(tm, m)
    return pl.pallas_call(
        _mm_body,
        grid=(m // tm,),
        in_specs=[pl.BlockSpec((tm, k), lambda i: (i, 0)),
                  pl.BlockSpec((k, n), lambda i: (0, 0))],
        out_specs=[pl.BlockSpec((tm, wd), lambda i: (i, 0)) for wd, _ in outs],
        out_shape=[jax.ShapeDtypeStruct((m, wd), dt) for wd, dt in outs],
        compiler_params=_params("parallel"),
        name="proj_mm",
    )(x, w)


def _compress_body(ch_ref, w1_ref, b1_ref, pe_ref, w2_ref, o_ref):
    nc = ch_ref.shape[0]
    ch = ch_ref[...]
    top = _dot(ch, w1_ref[0])
    bot = _dot(ch, w1_ref[1])
    bot_next = pltpu.roll(bot, nc - 1, axis=0)
    pe_term = _dot(pe_ref[0], w1_ref[0]) + _dot(pe_ref[1], w1_ref[1])
    h = top + bot_next + pe_term[0:1] + b1_ref[...]
    h = jax.nn.gelu(h, approximate=True)
    o_ref[...] = _dot(h.astype(BF16), w2_ref[...]).astype(o_ref.dtype)


def _compress(ch, w1, b1, pe, w2, batch):
    nc = ch.shape[1] // batch
    half = CMP_STRIDE * NSA_HD
    return pl.pallas_call(
        _compress_body,
        grid=(2, NSA_GROUPS, batch),
        in_specs=[pl.BlockSpec((None, nc, half), lambda t, g, b: (t * NSA_GROUPS + g, b, 0)),
                  pl.BlockSpec((None, 2, half, CMP_HID), lambda t, g, b: (t, 0, 0, 0)),
                  pl.BlockSpec((None, 1, CMP_HID), lambda t, g, b: (t, 0, 0)),
                  pl.BlockSpec((None, 2, SUBLANE, half), lambda t, g, b: (t, 0, 0, 0)),
                  pl.BlockSpec((None, CMP_HID, NSA_HD), lambda t, g, b: (t, 0, 0))],
        out_specs=pl.BlockSpec((None, nc, NSA_HD), lambda t, g, b: (t * NSA_GROUPS + g, b, 0)),
        out_shape=jax.ShapeDtypeStruct((2 * NSA_GROUPS, batch * nc, NSA_HD), BF16),
        compiler_params=_params("parallel", "parallel", "parallel"),
        name="nsa_compress",
    )(ch, w1, b1, pe, w2)


def _nsa_body(q_ref, kv_ref, kc_ref, vc_ref, gl_ref, ov_ref, o_ref, m_scr, acc_scr, tiles_ref, *,
              nc, ns, qb0):
    g = pl.program_id(1)
    qb = qb0 + pl.program_id(2)
    qs = qb * Q_BLK
    scale = NSA_HD ** -0.5
    n_chunks = Q_BLK // NSA_ROWS

    q4 = q_ref[...] * jnp.asarray(scale, q_ref.dtype)
    qst = jnp.concatenate([q4[:, h * NSA_HD:(h + 1) * NSA_HD] for h in range(NSA_HPG)], axis=0)
    slopes = [jnp.where(g == 0, 2.0 ** -(h + 1), 2.0 ** -(NSA_HPG + h + 1)).astype(F32) for h in range(NSA_HPG)]
    tq1 = qs + lax.broadcasted_iota(jnp.int32, (Q_BLK, 1), 0)

    def chunk_rows(h, rc):
        return slice(h * Q_BLK + rc * NSA_ROWS, h * Q_BLK + (rc + 1) * NSA_ROWS)

    order = [(h, rc) for h in range(NSA_HPG) for rc in range(n_chunks)]

    def stacked(parts):
        return jnp.concatenate([parts[i] for i in order], axis=0)

    def lane_blocks(x, op):
        out = x[:, 0:LANE]
        for j in range(1, x.shape[1] // LANE):
            out = op(out, x[:, j * LANE:(j + 1) * LANE])
        return out

    def lane_tiled(x, width):
        return jnp.concatenate([x] * (width // LANE), axis=1)

    def exp_pass(logit_parts, m_all, masks=None, want_sums=True):
        width = logit_parts[order[0]].shape[1]
        m_lanes = jnp.broadcast_to(m_all, (m_all.shape[0], LANE))
        e_parts, sums = {}, {}
        for h, rc in order:
            e = jnp.exp(logit_parts[h, rc] - lane_tiled(m_lanes[chunk_rows(h, rc)], width))
            if masks is not None:
                e = jnp.where(masks[rc], e, 0.0)
            e_parts[h, rc] = e
            if want_sums:
                sums[h, rc] = lane_blocks(e, jnp.add)
        return e_parts, (jnp.sum(stacked(sums), -1, keepdims=True) if want_sums else None)

    def masked_logits(s_all, key_pos, is_valid):
        parts, maxes, masks = {}, {}, {}
        for rc in range(n_chunks):
            dist = tq1[rc * NSA_ROWS:(rc + 1) * NSA_ROWS] - key_pos
            masks[rc] = is_valid(dist)
            d_f = dist.astype(F32)
            for h in range(NSA_HPG):
                s = jnp.where(masks[rc], s_all[chunk_rows(h, rc)] - slopes[h] * d_f, NEG_INF)
                parts[h, rc] = s
                maxes[h, rc] = lane_blocks(s, jnp.maximum)
        return parts, masks, jnp.max(stacked(maxes), -1, keepdims=True)

    c_end = lax.broadcasted_iota(jnp.int32, (1, nc), 1) * CMP_STRIDE + (CMP_LEN - 1)
    logit_c, mask_c, m_c = masked_logits(_dot_nt(qst, kc_ref[...]), c_end, lambda d: d >= 0)
    e_c, l_c = exp_pass(logit_c, m_c, mask_c)
    r_c = 1.0 / jnp.maximum(l_c, 1e-30)
    o_cmp = _dot(stacked({i: e_c[i].astype(BF16) for i in order}), vc_ref[...]) * r_c

    r_lanes = jnp.broadcast_to(r_c, (r_c.shape[0], LANE))
    p_hi, p_lo = [], []
    for rc in range(n_chunks):
        p_sum = None
        for h in range(NSA_HPG):
            p = e_c[h, rc] * lane_tiled(r_lanes[chunk_rows(h, rc)], nc)
            p_sum = p if p_sum is None else p_sum + p
        hi = p_sum.astype(BF16)
        p_hi.append(hi)
        p_lo.append((p_sum - hi.astype(F32)).astype(BF16))
    p_hi = jnp.concatenate(p_hi, axis=0)
    p_lo = jnp.concatenate(p_lo, axis=0)

    imp_t = (_dot(p_hi, ov_ref[...]) + _dot(p_lo, ov_ref[...])).T
    blk = lax.broadcasted_iota(jnp.int32, (ns, Q_BLK), 0)
    tq_row = qs + lax.broadcasted_iota(jnp.int32, (1, Q_BLK), 1)
    cur = tq_row >> SEL_BLK_LOG2
    forced = (blk == 0) | (blk == cur) | (blk == cur - 1)
    valid = blk * SEL_BLK <= tq_row
    score = jnp.where(valid, imp_t + jnp.where(forced, FORCE_BONUS, 0.0), NEG_INF)

    blk_l = blk[:, 0:LANE]

    def pick(_, carry):
        sc, sel = carry
        m = jnp.max(sc, 0, keepdims=True)
        idx = jnp.min(jnp.where(sc == m, blk_l, ns), 0, keepdims=True)
        hit = blk_l == idx
        return jnp.where(hit, REMOVED, sc), jnp.where(hit, 1.0, sel)

    sel_t = jnp.concatenate(
        [lax.fori_loop(0, min(SEL_TOPK, ns), pick,
                       (score[:, j * LANE:(j + 1) * LANE], jnp.zeros((ns, LANE), F32)))[1]
         for j in range(Q_BLK // LANE)], axis=1)
    sel = sel_t.T
    sel_bias = ((sel - 1.0) * -NEG_INF).astype(BF16)
    blk_any = jnp.max(sel, 0, keepdims=True)

    tile_blks = SEL_TILE // SEL_BLK
    blk_row = lax.broadcasted_iota(jnp.int32, (ns, SEL_TILE), 0)
    blk_of_key = lax.broadcasted_iota(jnp.int32, (ns, SEL_TILE), 1) >> SEL_BLK_LOG2
    key_in_tile = lax.broadcasted_iota(jnp.int32, (1, SEL_TILE), 1)
    m_scr[...] = jnp.full(m_scr.shape, NEG_INF, F32)
    acc_scr[...] = jnp.zeros(acc_scr.shape, F32)
    v_lane = lax.broadcasted_iota(jnp.int32, (1, 2 * NSA_HD), 1)
    one_b = jnp.ones((), kv_ref.dtype)

    tile_of_blk = lax.broadcasted_iota(jnp.int32, (ns, LANE), 0) >> (SEL_TILE.bit_length() - 1 - SEL_BLK_LOG2)
    in_tile = jnp.where(tile_of_blk == lax.broadcasted_iota(jnp.int32, (ns, LANE), 1), 1.0, 0.0).astype(BF16)
    tile_cnt = _dot(jnp.broadcast_to(blk_any, (SUBLANE, ns)).astype(BF16), in_tile)
    n_visible = (qs + Q_BLK + SEL_TILE - 1) // SEL_TILE
    n_active = jnp.int32(0)
    for kt in range(ns // tile_blks):
        tiles_ref[n_active] = kt
        n_active = n_active + ((tile_cnt[0, kt] > 0.0) & (kt < n_visible)).astype(jnp.int32)

    def sel_tile(j, carry):
        kt = tiles_ref[j]
        k0 = pl.multiple_of(kt * SEL_TILE, SEL_TILE)
        st_all = _dot_nt(qst, kv_ref[pl.ds(k0, SEL_TILE), 0:NSA_HD])
        expand = jnp.where(blk_row == blk_of_key + kt * tile_blks, 1.0, 0.0).astype(BF16)
        key_pos = k0 + key_in_tile
        bias_all = jnp.where(tq1 - key_pos >= 0, _dot(sel_bias, expand), NEG_INF)
        logits, maxes = {}, {}
        for rc in range(n_chunks):
            qr = slice(rc * NSA_ROWS, (rc + 1) * NSA_ROWS)
            d_f = (tq1[qr] - key_pos).astype(F32)
            for h in range(NSA_HPG):
                s = st_all[chunk_rows(h, rc)] + (bias_all[qr] - slopes[h] * d_f)
                logits[h, rc] = s
                maxes[h, rc] = lane_blocks(s, jnp.maximum)
        m_old = m_scr[...]
        m_new = jnp.maximum(m_old, jnp.max(stacked(maxes), -1, keepdims=True))
        p_parts, _ = exp_pass(logits, m_new, want_sums=False)
        alpha = jnp.exp(m_old - m_new)
        m_scr[...] = m_new
        v_one = jnp.where(v_lane < NSA_HD, kv_ref[pl.ds(k0, SEL_TILE), 2 * NSA_HD:4 * NSA_HD], one_b)
        p_all = stacked({i: p_parts[i].astype(BF16) for i in order})
        acc_scr[...] = alpha * acc_scr[...] + _dot(p_all, v_one)
        return carry

    lax.fori_loop(0, n_active, sel_tile, 0)
    acc = acc_scr[...]
    o_slc = acc[:, 0:NSA_HD] * (1.0 / jnp.maximum(acc[:, NSA_HD:NSA_HD + 1], 1e-30))

    wlen = WIN + Q_BLK
    w0 = pl.multiple_of(jnp.maximum(qs - WIN, 0), Q_BLK)
    pos_w = w0 + lax.broadcasted_iota(jnp.int32, (1, wlen), 1)
    logit_w, _, m_w = masked_logits(_dot_nt(qst, kv_ref[pl.ds(w0, wlen), NSA_HD:2 * NSA_HD]), pos_w,
                                    lambda d: (d >= 0) & (d < WIN))
    e_w, l_w = exp_pass(logit_w, m_w)
    o_win = (_dot(stacked({i: e_w[i].astype(BF16) for i in order}), kv_ref[pl.ds(w0, wlen), 3 * NSA_HD:4 * NSA_HD])
             * (1.0 / jnp.maximum(l_w, 1e-30)))

    gates = _sigmoid(gl_ref[...])
    outs = []
    for h in range(NSA_HPG):
        r = slice(h * Q_BLK, (h + 1) * Q_BLK)
        outs.append(gates[:, 3 * h:3 * h + 1] * o_cmp[r] + gates[:, 3 * h + 1:3 * h + 2] * o_slc[r]
                    + gates[:, 3 * h + 2:3 * h + 3] * o_win[r])
    o_ref[...] = jnp.concatenate(outs, axis=1).astype(o_ref.dtype)


def _round_up(v, m):
    return (v + m - 1) // m * m


def _nsa_attention(main, gate_logits, cmp_tok, overlap, batch, seq):
    nq = seq // Q_BLK
    gw = NSA_HPG * NSA_HD
    hq = NSA_HPG * Q_BLK
    n_ranges = min(NSA_RANGES, nq)
    per = nq // n_ranges
    outs = []
    for r in range(n_ranges):
        q0 = r * per
        seen = (q0 + per) * Q_BLK
        nc = min(seq // CMP_STRIDE, _round_up(seen // CMP_STRIDE, LANE))
        ns = min(seq // SEL_BLK, _round_up(seen // SEL_BLK, LANE))
        out = pl.pallas_call(
            functools.partial(_nsa_body, nc=nc, ns=ns, qb0=q0),
            grid=(batch, NSA_GROUPS, per),
            in_specs=[pl.BlockSpec((Q_BLK, gw), lambda b, g, i, q0=q0: (b * nq + q0 + i, g)),
                      pl.BlockSpec((seq, gw), lambda b, g, i: (b, NSA_GROUPS + g)),
                      pl.BlockSpec((None, None, nc, NSA_HD), lambda b, g, i: (g, b, 0, 0)),
                      pl.BlockSpec((None, None, nc, NSA_HD), lambda b, g, i: (NSA_GROUPS + g, b, 0, 0)),
                      pl.BlockSpec((Q_BLK, LANE), lambda b, g, i, q0=q0: (b * nq + q0 + i, g)),
                      pl.BlockSpec((nc, ns), lambda b, g, i: (0, 0))],
            out_specs=pl.BlockSpec((Q_BLK, gw), lambda b, g, i: (b * per + i, g)),
            out_shape=jax.ShapeDtypeStruct((batch * per * Q_BLK, MIX_W), BF16),
            scratch_shapes=[pltpu.VMEM((hq, 1), F32), pltpu.VMEM((hq, 2 * NSA_HD), F32),
                            pltpu.SMEM((ns * SEL_BLK // SEL_TILE,), jnp.int32)],
            compiler_params=_params("parallel", "parallel", "arbitrary"),
            name="nsa_attention",
        )(main, main, cmp_tok, cmp_tok, gate_logits, overlap)
        outs.append(out.reshape(batch, per * Q_BLK, MIX_W))
    return jnp.concatenate(outs, axis=1).reshape(batch * seq, MIX_W)


def _unit_lower_inverses(mats):
    n = mats[0].shape[0]
    eye = (lax.broadcasted_iota(jnp.int32, (n, n), 0) == lax.broadcasted_iota(jnp.int32, (n, n), 1)).astype(F32)
    xs = [_split2(eye - a) for a in mats]
    ps = [_split2(a) for a in mats]
    power = 1
    while 2 * power < CHUNK:
        ps = [_split2(_dot_split(p, p)) for p in ps]
        xs = [_split2((x[0].astype(F32) + x[1].astype(F32)) + _dot_split(x, p)) for x, p in zip(xs, ps)]
        power *= 2
    return xs


def _gdn_body(raw_ref, ab_ref, conv_ref, alog_ref, dtb_ref, o_ref, s_ref, tail_ref, buf_ref, *, tb):
    @pl.when(pl.program_id(1) == 0)
    def _():
        s_ref[...] = jnp.zeros_like(s_ref)
        tail_ref[...] = jnp.zeros_like(tail_ref)

    raw = raw_ref[...]
    buf_ref[0:SUBLANE] = tail_ref[...]
    buf_ref[SUBLANE:SUBLANE + tb] = raw
    tail_ref[...] = raw[tb - SUBLANE:tb]
    x = jnp.zeros_like(raw)
    for k in range(CONV_K):
        x = x + buf_ref[pl.ds(SUBLANE - (CONV_K - 1) + k, tb)] * conv_ref[k:k + 1]
    x = x * _sigmoid(x)

    ab = ab_ref[...]
    zed = ab + dtb_ref[...]
    softplus = jnp.maximum(zed, 0.0) + jnp.log1p(jnp.exp(-jnp.abs(zed)))
    g_all = -jnp.exp(alog_ref[...]) * softplus
    beta_all = _sigmoid(ab)

    c2 = 2 * CHUNK
    ri = lax.broadcasted_iota(jnp.int32, (c2, c2), 0)
    ci = lax.broadcasted_iota(jnp.int32, (c2, c2), 1)
    tri2 = jnp.where((ri >= ci) & ((ri >> CHUNK_LOG2) == (ci >> CHUNK_LOG2)), 1.0, 0.0).astype(BF16)
    hc = GDN_HEADS * CHUNK
    rs = lax.broadcasted_iota(jnp.int32, (hc, hc), 0)
    cs = lax.broadcasted_iota(jnp.int32, (hc, hc), 1)
    same_head = (rs >> CHUNK_LOG2) == (cs >> CHUNK_LOG2)
    tri = same_head & (rs >= cs)
    strict = same_head & (rs > cs)
    qscale = GDN_HD ** -0.5

    def l2n(v):
        return v * lax.rsqrt(jnp.sum(v * v, -1, keepdims=True) + NORM_EPS)

    def stack(fn):
        return jnp.concatenate([fn(h) for h in range(GDN_HEADS)], axis=0)

    chunks = []
    for pair in range(tb // c2):
        pr = slice(pair * c2, (pair + 1) * c2)
        gc2 = _dot_exact_lhs(tri2, g_all[pr])
        gc2t = gc2.T
        for half in range(2):
            rows = slice(pair * c2 + half * CHUNK, pair * c2 + (half + 1) * CHUNK)
            hr = slice(half * CHUNK, (half + 1) * CHUNK)
            gcol = stack(lambda h: gc2[hr, h:h + 1])
            grow = jnp.concatenate([gc2t[h:h + 1, hr] for h in range(GDN_HEADS)], axis=1)
            g_last = stack(lambda h: jnp.broadcast_to(gc2[hr, h:h + 1][CHUNK - 1:CHUNK], (CHUNK, 1)))
            beta = stack(lambda h: beta_all[rows, GDN_HEADS + h:GDN_HEADS + h + 1])
            qs_ = stack(lambda h: l2n(x[rows, h * GDN_HD:(h + 1) * GDN_HD])) * qscale
            ks_ = stack(lambda h: l2n(x[rows, MIX_W + h * GDN_HD:MIX_W + (h + 1) * GDN_HD]))
            vs_ = stack(lambda h: x[rows, 2 * MIX_W + h * GDN_HD:2 * MIX_W + (h + 1) * GDN_HD])
            decay = jnp.where(tri, jnp.exp(jnp.where(tri, gcol - grow, 0.0)), 0.0)
            kb = ks_ * beta
            ks_b = ks_.astype(BF16)
            egc = jnp.exp(gcol)
            chunks.append(dict(
                rows=rows,
                a_mat=jnp.where(strict, _dot_nt(kb.astype(BF16), ks_b) * decay, 0.0),
                rhs=_split2(jnp.concatenate([vs_ * beta, kb * egc], axis=1)),
                qk=jnp.where(tri, _dot_nt(qs_.astype(BF16), ks_b) * decay, 0.0).astype(BF16),
                q_dec=(qs_ * egc).astype(BF16),
                k_tail=(ks_ * jnp.exp(g_last - gcol)).astype(BF16),
                a_last=jnp.exp(g_last)))

    t_invs = _unit_lower_inverses([c["a_mat"] for c in chunks])
    sols = [_dot_split(t, c["rhs"]) for t, c in zip(t_invs, chunks)]

    hrows = [slice(h * CHUNK, (h + 1) * CHUNK) for h in range(GDN_HEADS)]
    for c, sol in zip(chunks, sols):
        states = [s_ref[h] for h in range(GDN_HEADS)]
        states_b = [st.astype(BF16) for st in states]
        v_new = stack(lambda h: sol[hrows[h], :GDN_HD] - _dot(sol[hrows[h], GDN_HD:].astype(BF16), states_b[h]))
        v_new_b = v_new.astype(BF16)
        o = _dot(c["qk"], v_new_b) + stack(lambda h: _dot(c["q_dec"][hrows[h]], states_b[h]))
        for h in range(GDN_HEADS):
            s_ref[h] = (states[h] * c["a_last"][hrows[h]][0:1]
                        + _dot_tn(c["k_tail"][hrows[h]], v_new_b[hrows[h]]))
            o_ref[c["rows"], h * GDN_HD:(h + 1) * GDN_HD] = o[hrows[h]]


def _gdn(raw, ab, conv, alog, dtb, batch, seq, tb=256):
    nt = seq // tb
    return pl.pallas_call(
        functools.partial(_gdn_body, tb=tb),
        grid=(batch, nt),
        in_specs=[pl.BlockSpec((tb, 3 * MIX_W), lambda b, i: (b * nt + i, 0)),
                  pl.BlockSpec((tb, LANE), lambda b, i: (b * nt + i, 0)),
                  pl.BlockSpec((CONV_K, 3 * MIX_W), lambda b, i: (0, 0)),
                  pl.BlockSpec((1, LANE), lambda b, i: (0, 0)),
                  pl.BlockSpec((1, LANE), lambda b, i: (0, 0))],
        out_specs=pl.BlockSpec((tb, MIX_W), lambda b, i: (b * nt + i, 0)),
        out_shape=jax.ShapeDtypeStruct((batch * seq, MIX_W), F32),
        scratch_shapes=[pltpu.VMEM((GDN_HEADS, GDN_HD, GDN_HD), F32),
                        pltpu.VMEM((SUBLANE, 3 * MIX_W), F32),
                        pltpu.VMEM((SUBLANE + tb, 3 * MIX_W), F32)],
        compiler_params=_params("parallel", "arbitrary"),
        name="gated_delta_net",
    )(raw, ab, conv, alog, dtb)


def _gla_constants():
    c = CHUNK
    t = np.arange(c)[:, None]
    s = np.arange(c)[None, :]
    ops = [(s <= t), (s > t)]
    masks = []
    for m in GLA_LEVELS:
        r = (t // (2 * m)) * (2 * m) + m
        upper = (t % (2 * m)) >= m
        ops.append(np.where(upper, (s > r) & (s <= t), (s > t) & (s <= r)))
        i, j = t, s
        masks.append((i // (2 * m) == j // (2 * m)) & ((i % (2 * m)) >= m) & ((j % (2 * m)) < m))
    masks.append(t == s)
    heads = np.eye(GLA_HEADS)
    masks = np.stack([np.kron(heads, m) for m in masks])
    return np.concatenate(ops, 0).astype(np.float32), masks.astype(np.float32)


def _gla_body(x_ref, wlr_ref, blr_ref, ops_ref, lm_ref, o_ref, s_ref, *, tb):
    @pl.when(pl.program_id(1) == 0)
    def _():
        s_ref[...] = jnp.zeros_like(s_ref)

    hw = GLA_DV
    kw = GLA_HEADS * hw
    qscale = GLA_DK ** -0.5
    n_lvl = len(GLA_LEVELS)
    hrows = [slice(h * CHUNK, (h + 1) * CHUNK) for h in range(GLA_HEADS)]

    def stack(fn):
        return jnp.concatenate([fn(h) for h in range(GLA_HEADS)], axis=0)

    for c in range(tb // CHUNK):
        rows = slice(c * CHUNK, (c + 1) * CHUNK)
        lr = x_ref[rows, 3 * kw:3 * kw + LANE]
        z = _dot(lr, wlr_ref[...]) + blr_ref[...]
        log_a = (jnp.minimum(z, 0.0) - jnp.log1p(jnp.exp(-jnp.abs(z)))) / GLA_TAU
        e_all = jnp.exp(_dot_exact_lhs(ops_ref[...], log_a))

        def factor(i):
            return stack(lambda h: e_all[i * CHUNK:(i + 1) * CHUNK, h * hw:(h + 1) * hw])

        qs_ = stack(lambda h: x_ref[rows, h * hw:(h + 1) * hw]) * qscale
        ks_ = stack(lambda h: x_ref[rows, kw + h * hw:kw + (h + 1) * hw])
        vs_b = stack(lambda h: x_ref[rows, 2 * kw + h * hw:2 * kw + (h + 1) * hw]).astype(BF16)
        scores = lm_ref[n_lvl] * _dot_nt(qs_.astype(BF16), ks_.astype(BF16))
        for lvl in range(n_lvl):
            xl = factor(2 + lvl)
            scores = scores + lm_ref[lvl] * _dot_nt((qs_ * xl).astype(BF16), (ks_ * xl).astype(BF16))
        e_cum = factor(0)
        q_dec = (qs_ * e_cum).astype(BF16)
        k_tail = (ks_ * factor(1)).astype(BF16)
        states = [s_ref[h] for h in range(GLA_HEADS)]
        o = (_dot(scores.astype(BF16), vs_b)
             + stack(lambda h: _dot_nt(q_dec[hrows[h]], states[h].astype(BF16))))
        for h in range(GLA_HEADS):
            s_ref[h] = (states[h] * e_cum[hrows[h]][CHUNK - 1:CHUNK]
                        + _dot_tn(vs_b[hrows[h]], k_tail[hrows[h]]))
            o_ref[rows, h * hw:(h + 1) * hw] = o[hrows[h]]


def _gla(x, wlr, blr, ops, lmask, batch, seq, tb=256):
    nt = seq // tb
    width = x.shape[1]
    return pl.pallas_call(
        functools.partial(_gla_body, tb=tb),
        grid=(batch, nt),
        in_specs=[pl.BlockSpec((tb, width), lambda b, i: (b * nt + i, 0)),
                  pl.BlockSpec(wlr.shape, lambda b, i: (0, 0)),
                  pl.BlockSpec(blr.shape, lambda b, i: (0, 0)),
                  pl.BlockSpec(ops.shape, lambda b, i: (0, 0)),
                  pl.BlockSpec(lmask.shape, lambda b, i: (0, 0, 0))],
        out_specs=pl.BlockSpec((tb, MIX_W), lambda b, i: (b * nt + i, 0)),
        out_shape=jax.ShapeDtypeStruct((batch * seq, MIX_W), F32),
        scratch_shapes=[pltpu.VMEM((GLA_HEADS, GLA_DV, GLA_DV), F32)],
        compiler_params=_params("parallel", "arbitrary"),
        name="gla",
    )(x, wlr, blr, ops, lmask)


def _merge_body(x_ref, yn_ref, og_ref, ol_ref, wzr_ref, wmg_ref, ng_ref, nl_ref, wup_ref, wout_ref,
                lg_ref, lb_ref, o_ref, *, alpha):
    x = x_ref[...]
    xb = x.astype(BF16)
    zr = _dot(xb, wzr_ref[...])

    def head_rms(o, gain):
        parts = []
        for h in range(MIX_W // LANE):
            oh = o[:, h * LANE:(h + 1) * LANE]
            parts.append(oh * lax.rsqrt(jnp.mean(oh * oh, -1, keepdims=True) + NORM_EPS) * gain)
        return jnp.concatenate(parts, axis=1)

    z = zr[:, :MIX_W]
    r = zr[:, MIX_W:]
    y_gdn = head_rms(og_ref[...], ng_ref[...]) * (z * _sigmoid(z))
    y_gla = head_rms(ol_ref[...], nl_ref[...]) * (r * _sigmoid(r))
    ys = (yn_ref[...], y_gdn.astype(BF16), y_gla.astype(BF16))
    merged = jnp.zeros(x.shape, F32)
    for br in range(N_BRANCH):
        gate = _sigmoid(_dot(xb, wmg_ref[:, br * D_MODEL:(br + 1) * D_MODEL]))
        merged = merged + gate * _dot(ys[br], wup_ref[br])
    h = _dot(merged.astype(BF16), wout_ref[...])
    o_ref[...] = _layer_norm(alpha * x + h, lg_ref[...], lb_ref[...])


def _const_spec(shape):
    nd = len(shape)
    return pl.BlockSpec(shape, lambda i: (0,) * nd, pipeline_mode=pl.Buffered(1))


def _merge(x, y_nsa, o_gdn, o_gla, wzr, wmg, ng, nl, wup, wout, lg, lb, alpha, tm=256):
    m = x.shape[0]
    row = lambda w: pl.BlockSpec((tm, w), lambda i: (i, 0))
    return pl.pallas_call(
        functools.partial(_merge_body, alpha=alpha),
        grid=(m // tm,),
        in_specs=[row(D_MODEL), row(MIX_W), row(MIX_W), row(MIX_W),
                  _const_spec(wzr.shape), _const_spec(wmg.shape), _const_spec(ng.shape), _const_spec(nl.shape),
                  _const_spec(wup.shape), _const_spec(wout.shape), _const_spec(lg.shape), _const_spec(lb.shape)],
        out_specs=row(D_MODEL),
        out_shape=jax.ShapeDtypeStruct((m, D_MODEL), F32),
        compiler_params=_params("parallel"),
        name="mixer_merge",
    )(x, y_nsa, o_gdn, o_gla, wzr, wmg, ng, nl, wup, wout, lg, lb)


def _xattn_body(x_ref, kv_ref, wq_ref, wo_ref, lg_ref, lb_ref, o_ref, *, alpha):
    x = x_ref[...]
    q = _dot(x.astype(BF16), wq_ref[...]).astype(BF16)
    scale = XA_HD ** -0.5
    outs = []
    for h in range(XA_HEADS):
        hs = slice(h * XA_HD, (h + 1) * XA_HD)
        s = _dot_nt(q[:, hs], kv_ref[:, hs]) * scale
        m = jnp.max(s, -1, keepdims=True)
        e = jnp.exp(s - m)
        p = e / jnp.sum(e, -1, keepdims=True)
        outs.append(_dot(p.astype(BF16), kv_ref[:, D_MODEL + h * XA_HD:D_MODEL + (h + 1) * XA_HD]))
    o = jnp.concatenate(outs, axis=1)
    h_out = _dot(o.astype(BF16), wo_ref[...])
    o_ref[...] = _layer_norm(alpha * x + h_out, lg_ref[...], lb_ref[...])


def _xattn(x, kvm, wq, wo, lg, lb, alpha, batch, seq, tm=512):
    nt = seq // tm
    mt = kvm.shape[0] // batch
    cs = lambda shape: pl.BlockSpec(shape, lambda b, i: (0,) * len(shape), pipeline_mode=pl.Buffered(1))
    return pl.pallas_call(
        functools.partial(_xattn_body, alpha=alpha),
        grid=(batch, nt),
        in_specs=[pl.BlockSpec((tm, D_MODEL), lambda b, i: (b * nt + i, 0)),
                  pl.BlockSpec((mt, 2 * D_MODEL), lambda b, i: (b, 0)),
                  cs(wq.shape), cs(wo.shape), cs(lg.shape), cs(lb.shape)],
        out_specs=pl.BlockSpec((tm, D_MODEL), lambda b, i: (b * nt + i, 0)),
        out_shape=jax.ShapeDtypeStruct((batch * seq, D_MODEL), F32),
        compiler_params=_params("parallel", "parallel"),
        name="mem_xattn",
    )(x, kvm, wq, wo, lg, lb)


def _mlp_body(x_ref, w1_ref, w2_ref, lg_ref, lb_ref, o_ref, *, alpha):
    x = x_ref[...]
    h = jnp.maximum(_dot(x.astype(BF16), w1_ref[...]), 0.0)
    h = (h * h).astype(BF16)
    o_ref[...] = _layer_norm(alpha * x + _dot(h, w2_ref[...]), lg_ref[...], lb_ref[...])


def _mlp(x, w1, w2, lg, lb, alpha, tm=512):
    m = x.shape[0]
    return pl.pallas_call(
        functools.partial(_mlp_body, alpha=alpha),
        grid=(m // tm,),
        in_specs=[pl.BlockSpec((tm, D_MODEL), lambda i: (i, 0)),
                  _const_spec(w1.shape), _const_spec(w2.shape), _const_spec(lg.shape), _const_spec(lb.shape)],
        out_specs=pl.BlockSpec((tm, D_MODEL), lambda i: (i, 0)),
        out_shape=jax.ShapeDtypeStruct((m, D_MODEL), F32),
        compiler_params=_params("parallel"),
        name="sq_relu_mlp",
    )(x, w1, w2, lg, lb)


def _pad_cols(w, width):
    return jnp.pad(w, ((0, 0), (0, width - w.shape[1])))


def _nsa_weights(w_in):
    def kv(kind, g):
        o = O_NSA_KV + (kind * NSA_GROUPS + g) * NSA_HD
        return w_in[:, o:o + NSA_HD]
    cols = [w_in[:, O_NSA_Q:O_NSA_Q + NSA_HEADS * NSA_HD]]
    for g in range(NSA_GROUPS):
        cols += [kv(2, g), kv(4, g), kv(3, g), kv(5, g)]
    cols += [kv(0, 0), kv(0, 1), kv(1, 0), kv(1, 1)]
    main = jnp.concatenate(cols, axis=1).astype(BF16)
    gw = NSA_HPG * 3
    gates = [_pad_cols(w_in[:, O_NSA_G + g * gw:O_NSA_G + (g + 1) * gw], LANE) for g in range(NSA_GROUPS)]
    return main, jnp.concatenate(gates, axis=1).astype(BF16)


def _gla_weights(w_in, w_lr, b_lr):
    def pad_heads(w):
        lead = w.shape[0]
        w = w.reshape(lead, GLA_HEADS, GLA_DK)
        return jnp.pad(w, ((0, 0), (0, 0), (0, GLA_DV - GLA_DK))).reshape(lead, GLA_HEADS * GLA_DV)
    cols = [pad_heads(w_in[:, O_GLA_Q:O_GLA_Q + GLA_HEADS * GLA_DK]),
            pad_heads(w_in[:, O_GLA_K:O_GLA_K + GLA_HEADS * GLA_DK]),
            w_in[:, O_GLA_V:O_GLA_V + GLA_HEADS * GLA_DV],
            _pad_cols(w_in[:, O_GLA_LR:O_GLA_LR + GLA_RANK], LANE)]
    wlr = jnp.pad(pad_heads(w_lr), ((0, LANE - GLA_RANK), (0, 0)))
    return jnp.concatenate(cols, axis=1).astype(BF16), wlr, pad_heads(b_lr[None, :])


def _overlap_matrix(seq):
    nc = seq // CMP_STRIDE
    ns = seq // SEL_BLK
    c_start = np.arange(nc)[:, None] * CMP_STRIDE
    s_start = np.arange(ns)[None, :] * SEL_BLK
    return jnp.asarray((c_start < s_start + SEL_BLK) & (c_start + CMP_LEN > s_start), BF16)


def _nsa_branch(x, p, l, batch, seq):
    w_in = p["w_in"][l]
    w_main, w_gates = _nsa_weights(w_in)
    cmp_w = 2 * NSA_GROUPS * NSA_HD
    qkv, cmp_src, gate_logits = _mm(x, jnp.concatenate([w_main, w_gates], axis=1),
                                    [(2 * MIX_W, BF16), (cmp_w, BF16), (NSA_GROUPS * LANE, F32)])
    nc = seq // CMP_STRIDE
    half = CMP_STRIDE * NSA_HD
    cmp_in = cmp_src.reshape(batch * seq, 2 * NSA_GROUPS, NSA_HD).transpose(1, 0, 2)
    cmp_in = cmp_in.reshape(2 * NSA_GROUPS, batch * nc, half)
    w1 = p["cmp_w1"][l].reshape(2, 2, half, CMP_HID).astype(BF16)
    pe = jnp.broadcast_to(p["cmp_pe"][l].reshape(2, 2, 1, half), (2, 2, SUBLANE, half)).astype(BF16)
    cmp_tok = _compress(cmp_in, w1, p["cmp_b1"][l][:, None, :], pe, p["cmp_w2"][l].astype(BF16), batch)
    cmp_tok = cmp_tok.reshape(2 * NSA_GROUPS, batch, nc, NSA_HD)
    return _nsa_attention(qkv, gate_logits, cmp_tok, _overlap_matrix(seq), batch, seq)


def _gdn_branch(x, p, l, batch, seq):
    w_in = p["w_in"][l]
    w_gdn = jnp.concatenate([w_in[:, O_GDN_QKV:O_GDN_QKV + 3 * MIX_W],
                             _pad_cols(w_in[:, O_GDN_A:O_GDN_A + 2 * GDN_HEADS], LANE)], axis=1).astype(BF16)
    raw, ab = _mm(x, w_gdn, [(3 * MIX_W, F32), (LANE, F32)])
    alog = _pad_cols(p["gdn_a_log"][l][None, :], LANE)
    dtb = _pad_cols(p["gdn_dt_bias"][l][None, :], LANE)
    return _gdn(raw, ab, p["gdn_conv"][l], alog, dtb, batch, seq)


def _gla_branch(x, p, l, batch, seq):
    w_gla, wlr, blr = _gla_weights(p["w_in"][l], p["gla_w_lr"][l], p["gla_b_lr"][l])
    gla_in, = _mm(x, w_gla, [(w_gla.shape[1], F32)])
    ops, lmask = _gla_constants()
    return _gla(gla_in, wlr, blr, jnp.asarray(ops, BF16), jnp.asarray(lmask), batch, seq)


def _layer(x, kvm_src, p, l, batch, seq, alpha):
    w_in = p["w_in"][l]
    y_nsa = _nsa_branch(x, p, l, batch, seq)
    o_gdn = _gdn_branch(x, p, l, batch, seq)
    o_gla = _gla_branch(x, p, l, batch, seq)

    wzr = jnp.concatenate([w_in[:, O_GDN_Z:O_GDN_Z + MIX_W], w_in[:, O_GLA_R:O_GLA_R + MIX_W]], axis=1).astype(BF16)
    wmg = w_in[:, O_MERGE:O_MERGE + N_BRANCH * D_MODEL].astype(BF16)
    x = _merge(x, y_nsa, o_gdn, o_gla, wzr, wmg, p["gdn_norm"][l][None, :], p["gla_norm"][l][None, :],
               p["w_up"][l].astype(BF16), p["w_out"][l].astype(BF16),
               p["ln_g"][l, 0][None, :], p["ln_b"][l, 0][None, :], alpha)

    wkv = jnp.concatenate([p["xa_wk"][l], p["xa_wv"][l]], axis=1).astype(BF16)
    kvm, = _mm(kvm_src, wkv, [(2 * D_MODEL, BF16)])
    x = _xattn(x, kvm, p["xa_wq"][l].astype(BF16), p["xa_wo"][l].astype(BF16),
               p["ln_g"][l, 1][None, :], p["ln_b"][l, 1][None, :], alpha, batch, seq)

    return _mlp(x, p["mlp_w1"][l].astype(BF16), p["mlp_w2"][l].astype(BF16),
                p["ln_g"][l, 2][None, :], p["ln_b"][l, 2][None, :], alpha)


def kernel(x, mem, w_in, cmp_pe, cmp_w1, cmp_b1, cmp_w2, gdn_conv, gdn_a_log, gdn_dt_bias, gdn_norm, gla_w_lr,
           gla_b_lr, gla_norm, w_up, w_out, xa_wq, xa_wk, xa_wv, xa_wo, mlp_w1, mlp_w2, ln_g, ln_b):
    batch, seq, d = x.shape
    depth = w_in.shape[0]
    alpha = (2.0 * depth) ** 0.25
    p = dict(w_in=w_in, cmp_pe=cmp_pe, cmp_w1=cmp_w1, cmp_b1=cmp_b1, cmp_w2=cmp_w2, gdn_conv=gdn_conv,
             gdn_a_log=gdn_a_log, gdn_dt_bias=gdn_dt_bias, gdn_norm=gdn_norm, gla_w_lr=gla_w_lr,
             gla_b_lr=gla_b_lr, gla_norm=gla_norm, w_up=w_up, w_out=w_out, xa_wq=xa_wq, xa_wk=xa_wk,
             xa_wv=xa_wv, xa_wo=xa_wo, mlp_w1=mlp_w1, mlp_w2=mlp_w2, ln_g=ln_g, ln_b=ln_b)
    h = x.reshape(batch * seq, d)
    mem2 = mem.reshape(batch * mem.shape[1], d)
    for l in range(depth):
        h = _layer(h, mem2, p, l, batch, seq, alpha)
    return h.reshape(batch, seq, d)
```

```python
import functools

import numpy as np
import jax
import jax.numpy as jnp
from jax import lax
from jax.experimental import pallas as pl
from jax.experimental.pallas import tpu as pltpu

F32 = jnp.float32
BF16 = jnp.bfloat16

D_MODEL = 1024
MIX_W = D_MODEL // 2
N_BRANCH = 3
NSA_HEADS = 8
NSA_HD = MIX_W // NSA_HEADS
NSA_GROUPS = 2
NSA_HPG = NSA_HEADS // NSA_GROUPS
CMP_LEN = 32
CMP_STRIDE = 16
CMP_HID = D_MODEL // 4
SEL_BLK = 64
SEL_TOPK = 16
WIN = 512
Q_BLK = 256
GDN_HEADS = 4
GDN_HD = MIX_W // GDN_HEADS
CONV_K = 4
CHUNK = 64
GLA_HEADS = 4
GLA_DK = MIX_W // (2 * GLA_HEADS)
GLA_DV = MIX_W // GLA_HEADS
GLA_RANK = 16
GLA_TAU = 16.0
XA_HEADS = 4
XA_HD = D_MODEL // XA_HEADS
D_FF = 4 * D_MODEL
LN_EPS = 1e-5
NORM_EPS = 1e-6
NEG_INF = -1e30
FORCE_BONUS = 1e4

LANE = 128
SUBLANE = 8
VMEM_LIMIT_BYTES = 48 * 1024 * 1024
SEL_TILE = 512
SEL_HEAD = 128
NSA_ROWS = 16
NSA_RANGES = 4
REMOVED = -3e38
GLA_LEVELS = (32, 16, 8, 4, 2, 1)
Q_BLK_LOG2 = Q_BLK.bit_length() - 1
SEL_BLK_LOG2 = SEL_BLK.bit_length() - 1
CHUNK_LOG2 = CHUNK.bit_length() - 1

_IN_SIZES = (NSA_HEADS * NSA_HD, 6 * NSA_GROUPS * NSA_HD, NSA_HEADS * 3, 3 * MIX_W, GDN_HEADS, GDN_HEADS,
             MIX_W, GLA_HEADS * GLA_DK, GLA_HEADS * GLA_DK, GLA_HEADS * GLA_DV, GLA_RANK, MIX_W,
             N_BRANCH * D_MODEL)
_OFF = tuple(int(v) for v in np.cumsum((0,) + _IN_SIZES))
(O_NSA_Q, O_NSA_KV, O_NSA_G, O_GDN_QKV, O_GDN_A, O_GDN_B, O_GDN_Z, O_GLA_Q, O_GLA_K, O_GLA_V, O_GLA_LR,
 O_GLA_R, O_MERGE, _) = _OFF


def _params(*sem):
    return pltpu.CompilerParams(dimension_semantics=sem, vmem_limit_bytes=VMEM_LIMIT_BYTES)


def _dot(a, b):
    return jnp.dot(a, b, preferred_element_type=F32)


def _dot_nt(a, b):
    return lax.dot_general(a, b, (((1,), (1,)), ((), ())), preferred_element_type=F32)


def _dot_tn(a, b):
    return lax.dot_general(a, b, (((0,), (0,)), ((), ())), preferred_element_type=F32)


def _sigmoid(x):
    return 1.0 / (1.0 + jnp.exp(-x))


def _layer_norm(v, g, b):
    mu = jnp.mean(v, -1, keepdims=True)
    c = v - mu
    var = jnp.mean(c * c, -1, keepdims=True)
    return c * lax.rsqrt(var + LN_EPS) * g + b


def _split2(a):
    hi = a.astype(BF16)
    return hi, (a - hi.astype(F32)).astype(BF16)


def _dot_split(a, b):
    return _dot(a[0], b[0]) + (_dot(a[0], b[1]) + _dot(a[1], b[0]))


def _dot_exact_lhs(op, x):
    x1 = x.astype(BF16)
    r1 = x - x1.astype(F32)
    x2 = r1.astype(BF16)
    x3 = (r1 - x2.astype(F32)).astype(BF16)
    return _dot(op, x1) + (_dot(op, x2) + _dot(op, x3))


def _mm_body(x_ref, w_ref, *o_refs):
    res = _dot(x_ref[...].astype(BF16), w_ref[...])
    off = 0
    for o_ref in o_refs:
        o_ref[...] = res[:, off:off + o_ref.shape[1]].astype(o_ref.dtype)
        off += o_ref.shape[1]


def _mm(x, w, outs, tm=512):
    m, k = x.shape
    n = w.shape[1]
    assert n == sum(wd for wd, _ in outs) and all(wd % LANE == 0 for wd, _ in outs)
    tm = min(tm, m)
    return pl.pallas_call(
        _mm_body,
        grid=(m // tm,),
        in_specs=[pl.BlockSpec((tm, k), lambda i: (i, 0)),
                  pl.BlockSpec((k, n), lambda i: (0, 0))],
        out_specs=[pl.BlockSpec((tm, wd), lambda i: (i, 0)) for wd, _ in outs],
        out_shape=[jax.ShapeDtypeStruct((m, wd), dt) for wd, dt in outs],
        compiler_params=_params("parallel"),
        name="proj_mm",
    )(x, w)


def _compress_body(ch_ref, w1_ref, b1_ref, pe_ref, w2_ref, o_ref):
    nc = ch_ref.shape[0]
    ch = ch_ref[...]
    top = _dot(ch, w1_ref[0])
    bot = _dot(ch, w1_ref[1])
    bot_next = pltpu.roll(bot, nc - 1, axis=0)
    pe_term = _dot(pe_ref[0], w1_ref[0]) + _dot(pe_ref[1], w1_ref[1])
    h = top + bot_next + pe_term[0:1] + b1_ref[...]
    h = jax.nn.gelu(h, approximate=True)
    o_ref[...] = _dot(h.astype(BF16), w2_ref[...]).astype(o_ref.dtype)


def _compress(ch, w1, b1, pe, w2, batch):
    nc = ch.shape[1] // batch
    half = CMP_STRIDE * NSA_HD
    return pl.pallas_call(
        _compress_body,
        grid=(2, NSA_GROUPS, batch),
        in_specs=[pl.BlockSpec((None, nc, half), lambda t, g, b: (t * NSA_GROUPS + g, b, 0)),
                  pl.BlockSpec((None, 2, half, CMP_HID), lambda t, g, b: (t, 0, 0, 0)),
                  pl.BlockSpec((None, 1, CMP_HID), lambda t, g, b: (t, 0, 0)),
                  pl.BlockSpec((None, 2, SUBLANE, half), lambda t, g, b: (t, 0, 0, 0)),
                  pl.BlockSpec((None, CMP_HID, NSA_HD), lambda t, g, b: (t, 0, 0))],
        out_specs=pl.BlockSpec((None, nc, NSA_HD), lambda t, g, b: (t * NSA_GROUPS + g, b, 0)),
        out_shape=jax.ShapeDtypeStruct((2 * NSA_GROUPS, batch * nc, NSA_HD), BF16),
        compiler_params=_params("parallel", "parallel", "parallel"),
        name="nsa_compress",
    )(ch, w1, b1, pe, w2)


def _nsa_body(q_ref, kv_ref, kc_ref, vc_ref, gl_ref, ov_ref, o_ref, m_scr, acc_scr, tiles_ref, *,
              nc, ns, qb0):
    g = pl.program_id(1)
    qb = qb0 + pl.program_id(2)
    qs = qb * Q_BLK
    scale = NSA_HD ** -0.5
    n_chunks = Q_BLK // NSA_ROWS

    q4 = q_ref[...] * jnp.asarray(scale, q_ref.dtype)
    qst = jnp.concatenate([q4[:, h * NSA_HD:(h + 1) * NSA_HD] for h in range(NSA_HPG)], axis=0)
    slopes = [jnp.where(g == 0, 2.0 ** -(h + 1), 2.0 ** -(NSA_HPG + h + 1)).astype(F32) for h in range(NSA_HPG)]
    tq1 = qs + lax.broadcasted_iota(jnp.int32, (Q_BLK, 1), 0)

    def chunk_rows(h, rc):
        return slice(h * Q_BLK + rc * NSA_ROWS, h * Q_BLK + (rc + 1) * NSA_ROWS)

    order = [(h, rc) for h in range(NSA_HPG) for rc in range(n_chunks)]

    def stacked(parts):
        return jnp.concatenate([parts[i] for i in order], axis=0)

    def lane_blocks(x, op):
        out = x[:, 0:LANE]
        for j in range(1, x.shape[1] // LANE):
            out = op(out, x[:, j * LANE:(j + 1) * LANE])
        return out

    def lane_tiled(x, width):
        return jnp.concatenate([x] * (width // LANE), axis=1)

    def exp_pass(logit_parts, m_all, masks=None, want_sums=True):
        width = logit_parts[order[0]].shape[1]
        m_lanes = jnp.broadcast_to(m_all, (m_all.shape[0], LANE))
        e_parts, sums = {}, {}
        for h, rc in order:
            e = jnp.exp(logit_parts[h, rc] - lane_tiled(m_lanes[chunk_rows(h, rc)], width))
            if masks is not None:
                e = jnp.where(masks[rc], e, 0.0)
            e_parts[h, rc] = e
            if want_sums:
                sums[h, rc] = lane_blocks(e, jnp.add)
        return e_parts, (jnp.sum(stacked(sums), -1, keepdims=True) if want_sums else None)

    def masked_logits(s_all, key_pos, is_valid):
        parts, maxes, masks = {}, {}, {}
        for rc in range(n_chunks):
            dist = tq1[rc * NSA_ROWS:(rc + 1) * NSA_ROWS] - key_pos
            masks[rc] = is_valid(dist)
            d_f = dist.astype(F32)
            for h in range(NSA_HPG):
                s = jnp.where(masks[rc], s_all[chunk_rows(h, rc)] - slopes[h] * d_f, NEG_INF)
                parts[h, rc] = s
                maxes[h, rc] = lane_blocks(s, jnp.maximum)
        return parts, masks, jnp.max(stacked(maxes), -1, keepdims=True)

    c_end = lax.broadcasted_iota(jnp.int32, (1, nc), 1) * CMP_STRIDE + (CMP_LEN - 1)
    logit_c, mask_c, m_c = masked_logits(_dot_nt(qst, kc_ref[...]), c_end, lambda d: d >= 0)
    e_c, l_c = exp_pass(logit_c, m_c, mask_c)
    r_c = 1.0 / jnp.maximum(l_c, 1e-30)
    o_cmp = _dot(stacked({i: e_c[i].astype(BF16) for i in order}), vc_ref[...]) * r_c

    r_lanes = jnp.broadcast_to(r_c, (r_c.shape[0], LANE))
    p_hi, p_lo = [], []
    for rc in range(n_chunks):
        p_sum = None
        for h in range(NSA_HPG):
            p = e_c[h, rc] * lane_tiled(r_lanes[chunk_rows(h, rc)], nc)
            p_sum = p if p_sum is None else p_sum + p
        hi = p_sum.astype(BF16)
        p_hi.append(hi)
        p_lo.append((p_sum - hi.astype(F32)).astype(BF16))
    p_hi = jnp.concatenate(p_hi, axis=0)
    p_lo = jnp.concatenate(p_lo, axis=0)

    imp_t = (_dot(p_hi, ov_ref[...]) + _dot(p_lo, ov_ref[...])).T
    blk = lax.broadcasted_iota(jnp.int32, (ns, Q_BLK), 0)
    tq_row = qs + lax.broadcasted_iota(jnp.int32, (1, Q_BLK), 1)
    cur = tq_row >> SEL_BLK_LOG2
    forced = (blk == 0) | (blk == cur) | (blk == cur - 1)
    valid = blk * SEL_BLK <= tq_row
    score = jnp.where(valid, imp_t + jnp.where(forced, FORCE_BONUS, 0.0), NEG_INF)

    blk_l = blk[:, 0:LANE]

    def pick(_, carry):
        sc, sel = carry
        m = jnp.max(sc, 0, keepdims=True)
        idx = jnp.min(jnp.where(sc == m, blk_l, ns), 0, keepdims=True)
        hit = blk_l == idx
        return jnp.where(hit, REMOVED, sc), jnp.where(hit, 1.0, sel)

    sel_t = jnp.concatenate(
        [lax.fori_loop(0, min(SEL_TOPK, ns), pick,
                       (score[:, j * LANE:(j + 1) * LANE], jnp.zeros((ns, LANE), F32)))[1]
         for j in range(Q_BLK // LANE)], axis=1)
    sel = sel_t.T
    sel_bias = ((sel - 1.0) * -NEG_INF).astype(BF16)
    blk_any = jnp.max(sel, 0, keepdims=True)

    tile_blks = SEL_TILE // SEL_BLK
    head_blks = SEL_HEAD // SEL_BLK
    m_scr[...] = jnp.full(m_scr.shape, NEG_INF, F32)
    acc_scr[...] = jnp.zeros(acc_scr.shape, F32)
    v_lane = lax.broadcasted_iota(jnp.int32, (1, 2 * NSA_HD), 1)
    one_b = jnp.ones((), kv_ref.dtype)
    in_head = lax.broadcasted_iota(jnp.int32, (Q_BLK, ns), 1) < head_blks
    sel_bias_rest = jnp.where(in_head, NEG_INF, (sel - 1.0) * -NEG_INF).astype(BF16)
    blk_any_rest = jnp.where(in_head[0:1], 0.0, blk_any)

    def attend(k0, blk0, width, bias_src):
        st_all = _dot_nt(qst, kv_ref[pl.ds(k0, width), 0:NSA_HD])
        blk_row = lax.broadcasted_iota(jnp.int32, (ns, width), 0)
        blk_of_key = lax.broadcasted_iota(jnp.int32, (ns, width), 1) >> SEL_BLK_LOG2
        expand = jnp.where(blk_row == blk_of_key + blk0, 1.0, 0.0).astype(BF16)
        key_pos = k0 + lax.broadcasted_iota(jnp.int32, (1, width), 1)
        bias_all = jnp.where(tq1 - key_pos >= 0, _dot(bias_src, expand), NEG_INF)
        logits, maxes = {}, {}
        for rc in range(n_chunks):
            qr = slice(rc * NSA_ROWS, (rc + 1) * NSA_ROWS)
            d_f = (tq1[qr] - key_pos).astype(F32)
            for h in range(NSA_HPG):
                s = st_all[chunk_rows(h, rc)] + (bias_all[qr] - slopes[h] * d_f)
                logits[h, rc] = s
                maxes[h, rc] = lane_blocks(s, jnp.maximum)
        m_old = m_scr[...]
        m_new = jnp.maximum(m_old, jnp.max(stacked(maxes), -1, keepdims=True))
        p_parts, _ = exp_pass(logits, m_new, want_sums=False)
        alpha = jnp.exp(m_old - m_new)
        m_scr[...] = m_new
        v_one = jnp.where(v_lane < NSA_HD, kv_ref[pl.ds(k0, width), 2 * NSA_HD:4 * NSA_HD], one_b)
        p_all = stacked({i: p_parts[i].astype(BF16) for i in order})
        acc_scr[...] = alpha * acc_scr[...] + _dot(p_all, v_one)

    attend(0, 0, SEL_HEAD, sel_bias)

    tile_of_blk = lax.broadcasted_iota(jnp.int32, (ns, LANE), 0) >> (SEL_TILE.bit_length() - 1 - SEL_BLK_LOG2)
    in_tile = jnp.where(tile_of_blk == lax.broadcasted_iota(jnp.int32, (ns, LANE), 1), 1.0, 0.0).astype(BF16)
    tile_cnt = _dot(jnp.broadcast_to(blk_any_rest, (SUBLANE, ns)).astype(BF16), in_tile)
    n_visible = (qs + Q_BLK + SEL_TILE - 1) // SEL_TILE
    n_active = jnp.int32(0)
    for kt in range(ns // tile_blks):
        tiles_ref[n_active] = kt
        n_active = n_active + ((tile_cnt[0, kt] > 0.0) & (kt < n_visible)).astype(jnp.int32)

    def sel_tile(j, carry):
        kt = tiles_ref[j]
        attend(pl.multiple_of(kt * SEL_TILE, SEL_TILE), kt * tile_blks, SEL_TILE, sel_bias_rest)
        return carry

    lax.fori_loop(0, n_active, sel_tile, 0)
    acc = acc_scr[...]
    o_slc = acc[:, 0:NSA_HD] * (1.0 / jnp.maximum(acc[:, NSA_HD:NSA_HD + 1], 1e-30))

    wlen = WIN + Q_BLK
    w0 = pl.multiple_of(jnp.maximum(qs - WIN, 0), Q_BLK)
    pos_w = w0 + lax.broadcasted_iota(jnp.int32, (1, wlen), 1)
    logit_w, _, m_w = masked_logits(_dot_nt(qst, kv_ref[pl.ds(w0, wlen), NSA_HD:2 * NSA_HD]), pos_w,
                                    lambda d: (d >= 0) & (d < WIN))
    e_w, l_w = exp_pass(logit_w, m_w)
    o_win = (_dot(stacked({i: e_w[i].astype(BF16) for i in order}), kv_ref[pl.ds(w0, wlen), 3 * NSA_HD:4 * NSA_HD])
             * (1.0 / jnp.maximum(l_w, 1e-30)))

    gates = _sigmoid(gl_ref[...])
    outs = []
    for h in range(NSA_HPG):
        r = slice(h * Q_BLK, (h + 1) * Q_BLK)
        outs.append(gates[:, 3 * h:3 * h + 1] * o_cmp[r] + gates[:, 3 * h + 1:3 * h + 2] * o_slc[r]
                    + gates[:, 3 * h + 2:3 * h + 3] * o_win[r])
    o_ref[...] = jnp.concatenate(outs, axis=1).astype(o_ref.dtype)


def _round_up(v, m):
    return (v + m - 1) // m * m


def _nsa_attention(main, gate_logits, cmp_tok, overlap, batch, seq):
    nq = seq // Q_BLK
    gw = NSA_HPG * NSA_HD
    hq = NSA_HPG * Q_BLK
    n_ranges = min(NSA_RANGES, nq)
    per = nq // n_ranges
    outs = []
    for r in range(n_ranges):
        q0 = r * per
        seen = (q0 + per) * Q_BLK
        nc = min(seq // CMP_STRIDE, _round_up(seen // CMP_STRIDE, LANE))
        ns = min(seq // SEL_BLK, _round_up(seen // SEL_BLK, LANE))
        out = pl.pallas_call(
            functools.partial(_nsa_body, nc=nc, ns=ns, qb0=q0),
            grid=(batch, NSA_GROUPS, per),
            in_specs=[pl.BlockSpec((Q_BLK, gw), lambda b, g, i, q0=q0: (b * nq + q0 + i, g)),
                      pl.BlockSpec((seq, gw), lambda b, g, i: (b, NSA_GROUPS + g)),
                      pl.BlockSpec((None, None, nc, NSA_HD), lambda b, g, i: (g, b, 0, 0)),
                      pl.BlockSpec((None, None, nc, NSA_HD), lambda b, g, i: (NSA_GROUPS + g, b, 0, 0)),
                      pl.BlockSpec((Q_BLK, LANE), lambda b, g, i, q0=q0: (b * nq + q0 + i, g)),
                      pl.BlockSpec((nc, ns), lambda b, g, i: (0, 0))],
            out_specs=pl.BlockSpec((Q_BLK, gw), lambda b, g, i: (b * per + i, g)),
            out_shape=jax.ShapeDtypeStruct((batch * per * Q_BLK, MIX_W), BF16),
            scratch_shapes=[pltpu.VMEM((hq, 1), F32), pltpu.VMEM((hq, 2 * NSA_HD), F32),
                            pltpu.SMEM((ns * SEL_BLK // SEL_TILE,), jnp.int32)],
            compiler_params=_params("parallel", "parallel", "arbitrary"),
            name="nsa_attention",
        )(main, main, cmp_tok, cmp_tok, gate_logits, overlap)
        outs.append(out.reshape(batch, per * Q_BLK, MIX_W))
    return jnp.concatenate(outs, axis=1).reshape(batch * seq, MIX_W)


def _unit_lower_inverses(mats):
    n = mats[0].shape[0]
    eye = (lax.broadcasted_iota(jnp.int32, (n, n), 0) == lax.broadcasted_iota(jnp.int32, (n, n), 1)).astype(F32)
    xs = [_split2(eye - a) for a in mats]
    ps = [_split2(a) for a in mats]
    power = 1
    while 2 * power < CHUNK:
        ps = [_split2(_dot_split(p, p)) for p in ps]
        xs = [_split2((x[0].astype(F32) + x[1].astype(F32)) + _dot_split(x, p)) for x, p in zip(xs, ps)]
        power *= 2
    return xs


def _gdn_body(raw_ref, ab_ref, conv_ref, alog_ref, dtb_ref, o_ref, s_ref, tail_ref, buf_ref, *, tb):
    @pl.when(pl.program_id(1) == 0)
    def _():
        s_ref[...] = jnp.zeros_like(s_ref)
        tail_ref[...] = jnp.zeros_like(tail_ref)

    raw = raw_ref[...]
    buf_ref[0:SUBLANE] = tail_ref[...]
    buf_ref[SUBLANE:SUBLANE + tb] = raw
    tail_ref[...] = raw[tb - SUBLANE:tb]
    x = jnp.zeros_like(raw)
    for k in range(CONV_K):
        x = x + buf_ref[pl.ds(SUBLANE - (CONV_K - 1) + k, tb)] * conv_ref[k:k + 1]
    x = x * _sigmoid(x)

    ab = ab_ref[...]
    zed = ab + dtb_ref[...]
    softplus = jnp.maximum(zed, 0.0) + jnp.log1p(jnp.exp(-jnp.abs(zed)))
    g_all = -jnp.exp(alog_ref[...]) * softplus
    beta_all = _sigmoid(ab)

    c2 = 2 * CHUNK
    ri = lax.broadcasted_iota(jnp.int32, (c2, c2), 0)
    ci = lax.broadcasted_iota(jnp.int32, (c2, c2), 1)
    tri2 = jnp.where((ri >= ci) & ((ri >> CHUNK_LOG2) == (ci >> CHUNK_LOG2)), 1.0, 0.0).astype(BF16)
    hc = GDN_HEADS * CHUNK
    rs = lax.broadcasted_iota(jnp.int32, (hc, hc), 0)
    cs = lax.broadcasted_iota(jnp.int32, (hc, hc), 1)
    same_head = (rs >> CHUNK_LOG2) == (cs >> CHUNK_LOG2)
    tri = same_head & (rs >= cs)
    strict = same_head & (rs > cs)
    qscale = GDN_HD ** -0.5

    def l2n(v):
        return v * lax.rsqrt(jnp.sum(v * v, -1, keepdims=True) + NORM_EPS)

    def stack(fn):
        return jnp.concatenate([fn(h) for h in range(GDN_HEADS)], axis=0)

    chunks = []
    for pair in range(tb // c2):
        pr = slice(pair * c2, (pair + 1) * c2)
        gc2 = _dot_exact_lhs(tri2, g_all[pr])
        gc2t = gc2.T
        for half in range(2):
            rows = slice(pair * c2 + half * CHUNK, pair * c2 + (half + 1) * CHUNK)
            hr = slice(half * CHUNK, (half + 1) * CHUNK)
            gcol = stack(lambda h: gc2[hr, h:h + 1])
            grow = jnp.concatenate([gc2t[h:h + 1, hr] for h in range(GDN_HEADS)], axis=1)
            g_last = stack(lambda h: jnp.broadcast_to(gc2[hr, h:h + 1][CHUNK - 1:CHUNK], (CHUNK, 1)))
            beta = stack(lambda h: beta_all[rows, GDN_HEADS + h:GDN_HEADS + h + 1])
            qs_ = stack(lambda h: l2n(x[rows, h * GDN_HD:(h + 1) * GDN_HD])) * qscale
            ks_ = stack(lambda h: l2n(x[rows, MIX_W + h * GDN_HD:MIX_W + (h + 1) * GDN_HD]))
            vs_ = stack(lambda h: x[rows, 2 * MIX_W + h * GDN_HD:2 * MIX_W + (h + 1) * GDN_HD])
            decay = jnp.where(tri, jnp.exp(jnp.where(tri, gcol - grow, 0.0)), 0.0)
            kb = ks_ * beta
            ks_b = ks_.astype(BF16)
            egc = jnp.exp(gcol)
            chunks.append(dict(
                rows=rows,
                a_mat=jnp.where(strict, _dot_nt(kb.astype(BF16), ks_b) * decay, 0.0),
                rhs=_split2(jnp.concatenate([vs_ * beta, kb * egc], axis=1)),
                qk=jnp.where(tri, _dot_nt(qs_.astype(BF16), ks_b) * decay, 0.0).astype(BF16),
                q_dec=(qs_ * egc).astype(BF16),
                k_tail=(ks_ * jnp.exp(g_last - gcol)).astype(BF16),
                a_last=jnp.exp(g_last)))

    t_invs = _unit_lower_inverses([c["a_mat"] for c in chunks])
    sols = [_dot_split(t, c["rhs"]) for t, c in zip(t_invs, chunks)]

    hrows = [slice(h * CHUNK, (h + 1) * CHUNK) for h in range(GDN_HEADS)]
    for c, sol in zip(chunks, sols):
        states = [s_ref[h] for h in range(GDN_HEADS)]
        states_b = [st.astype(BF16) for st in states]
        v_new = stack(lambda h: sol[hrows[h], :GDN_HD] - _dot(sol[hrows[h], GDN_HD:].astype(BF16), states_b[h]))
        v_new_b = v_new.astype(BF16)
        o = _dot(c["qk"], v_new_b) + stack(lambda h: _dot(c["q_dec"][hrows[h]], states_b[h]))
        for h in range(GDN_HEADS):
            s_ref[h] = (states[h] * c["a_last"][hrows[h]][0:1]
                        + _dot_tn(c["k_tail"][hrows[h]], v_new_b[hrows[h]]))
            o_ref[c["rows"], h * GDN_HD:(h + 1) * GDN_HD] = o[hrows[h]]


def _gdn(raw, ab, conv, alog, dtb, batch, seq, tb=256):
    nt = seq // tb
    return pl.pallas_call(
        functools.partial(_gdn_body, tb=tb),
        grid=(batch, nt),
        in_specs=[pl.BlockSpec((tb, 3 * MIX_W), lambda b, i: (b * nt + i, 0)),
                  pl.BlockSpec((tb, LANE), lambda b, i: (b * nt + i, 0)),
                  pl.BlockSpec((CONV_K, 3 * MIX_W), lambda b, i: (0, 0)),
                  pl.BlockSpec((1, LANE), lambda b, i: (0, 0)),
                  pl.BlockSpec((1, LANE), lambda b, i: (0, 0))],
        out_specs=pl.BlockSpec((tb, MIX_W), lambda b, i: (b * nt + i, 0)),
        out_shape=jax.ShapeDtypeStruct((batch * seq, MIX_W), F32),
        scratch_shapes=[pltpu.VMEM((GDN_HEADS, GDN_HD, GDN_HD), F32),
                        pltpu.VMEM((SUBLANE, 3 * MIX_W), F32),
                        pltpu.VMEM((SUBLANE + tb, 3 * MIX_W), F32)],
        compiler_params=_params("parallel", "arbitrary"),
        name="gated_delta_net",
    )(raw, ab, conv, alog, dtb)


def _gla_constants():
    c = CHUNK
    t = np.arange(c)[:, None]
    s = np.arange(c)[None, :]
    ops = [(s <= t), (s > t)]
    masks = []
    for m in GLA_LEVELS:
        r = (t // (2 * m)) * (2 * m) + m
        upper = (t % (2 * m)) >= m
        ops.append(np.where(upper, (s > r) & (s <= t), (s > t) & (s <= r)))
        i, j = t, s
        masks.append((i // (2 * m) == j // (2 * m)) & ((i % (2 * m)) >= m) & ((j % (2 * m)) < m))
    masks.append(t == s)
    heads = np.eye(GLA_HEADS)
    masks = np.stack([np.kron(heads, m) for m in masks])
    return np.concatenate(ops, 0).astype(np.float32), masks.astype(np.float32)


def _gla_body(x_ref, wlr_ref, blr_ref, ops_ref, lm_ref, o_ref, s_ref, *, tb):
    @pl.when(pl.program_id(1) == 0)
    def _():
        s_ref[...] = jnp.zeros_like(s_ref)

    hw = GLA_DV
    kw = GLA_HEADS * hw
    qscale = GLA_DK ** -0.5
    n_lvl = len(GLA_LEVELS)
    hrows = [slice(h * CHUNK, (h + 1) * CHUNK) for h in range(GLA_HEADS)]

    def stack(fn):
        return jnp.concatenate([fn(h) for h in range(GLA_HEADS)], axis=0)

    for c in range(tb // CHUNK):
        rows = slice(c * CHUNK, (c + 1) * CHUNK)
        lr = x_ref[rows, 3 * kw:3 * kw + LANE]
        z = _dot(lr, wlr_ref[...]) + blr_ref[...]
        log_a = (jnp.minimum(z, 0.0) - jnp.log1p(jnp.exp(-jnp.abs(z)))) / GLA_TAU
        e_all = jnp.exp(_dot_exact_lhs(ops_ref[...], log_a))

        def factor(i):
            return stack(lambda h: e_all[i * CHUNK:(i + 1) * CHUNK, h * hw:(h + 1) * hw])

        qs_ = stack(lambda h: x_ref[rows, h * hw:(h + 1) * hw]) * qscale
        ks_ = stack(lambda h: x_ref[rows, kw + h * hw:kw + (h + 1) * hw])
        vs_b = stack(lambda h: x_ref[rows, 2 * kw + h * hw:2 * kw + (h + 1) * hw]).astype(BF16)
        scores = lm_ref[n_lvl] * _dot_nt(qs_.astype(BF16), ks_.astype(BF16))
        for lvl in range(n_lvl):
            xl = factor(2 + lvl)
            scores = scores + lm_ref[lvl] * _dot_nt((qs_ * xl).astype(BF16), (ks_ * xl).astype(BF16))
        e_cum = factor(0)
        q_dec = (qs_ * e_cum).astype(BF16)
        k_tail = (ks_ * factor(1)).astype(BF16)
        states = [s_ref[h] for h in range(GLA_HEADS)]
        o = (_dot(scores.astype(BF16), vs_b)
             + stack(lambda h: _dot_nt(q_dec[hrows[h]], states[h].astype(BF16))))
        for h in range(GLA_HEADS):
            s_ref[h] = (states[h] * e_cum[hrows[h]][CHUNK - 1:CHUNK]
                        + _dot_tn(vs_b[hrows[h]], k_tail[hrows[h]]))
            o_ref[rows, h * hw:(h + 1) * hw] = o[hrows[h]]


def _gla(x, wlr, blr, ops, lmask, batch, seq, tb=256):
    nt = seq // tb
    width = x.shape[1]
    return pl.pallas_call(
        functools.partial(_gla_body, tb=tb),
        grid=(batch, nt),
        in_specs=[pl.BlockSpec((tb, width), lambda b, i: (b * nt + i, 0)),
                  pl.BlockSpec(wlr.shape, lambda b, i: (0, 0)),
                  pl.BlockSpec(blr.shape, lambda b, i: (0, 0)),
                  pl.BlockSpec(ops.shape, lambda b, i: (0, 0)),
                  pl.BlockSpec(lmask.shape, lambda b, i: (0, 0, 0))],
        out_specs=pl.BlockSpec((tb, MIX_W), lambda b, i: (b * nt + i, 0)),
        out_shape=jax.ShapeDtypeStruct((batch * seq, MIX_W), F32),
        scratch_shapes=[pltpu.VMEM((GLA_HEADS, GLA_DV, GLA_DV), F32)],
        compiler_params=_params("parallel", "arbitrary"),
        name="gla",
    )(x, wlr, blr, ops, lmask)


def _merge_body(x_ref, yn_ref, og_ref, ol_ref, wzr_ref, wmg_ref, ng_ref, nl_ref, wup_ref, wout_ref,
                lg_ref, lb_ref, o_ref, *, alpha):
    x = x_ref[...]
    xb = x.astype(BF16)
    zr = _dot(xb, wzr_ref[...])

    def head_rms(o, gain):
        parts = []
        for h in range(MIX_W // LANE):
            oh = o[:, h * LANE:(h + 1) * LANE]
            parts.append(oh * lax.rsqrt(jnp.mean(oh * oh, -1, keepdims=True) + NORM_EPS) * gain)
        return jnp.concatenate(parts, axis=1)

    z = zr[:, :MIX_W]
    r = zr[:, MIX_W:]
    y_gdn = head_rms(og_ref[...], ng_ref[...]) * (z * _sigmoid(z))
    y_gla = head_rms(ol_ref[...], nl_ref[...]) * (r * _sigmoid(r))
    ys = (yn_ref[...], y_gdn.astype(BF16), y_gla.astype(BF16))
    merged = jnp.zeros(x.shape, F32)
    for br in range(N_BRANCH):
        gate = _sigmoid(_dot(xb, wmg_ref[:, br * D_MODEL:(br + 1) * D_MODEL]))
        merged = merged + gate * _dot(ys[br], wup_ref[br])
    h = _dot(merged.astype(BF16), wout_ref[...])
    o_ref[...] = _layer_norm(alpha * x + h, lg_ref[...], lb_ref[...])


def _const_spec(shape):
    nd = len(shape)
    return pl.BlockSpec(shape, lambda i: (0,) * nd, pipeline_mode=pl.Buffered(1))


def _merge(x, y_nsa, o_gdn, o_gla, wzr, wmg, ng, nl, wup, wout, lg, lb, alpha, tm=256):
    m = x.shape[0]
    row = lambda w: pl.BlockSpec((tm, w), lambda i: (i, 0))
    return pl.pallas_call(
        functools.partial(_merge_body, alpha=alpha),
        grid=(m // tm,),
        in_specs=[row(D_MODEL), row(MIX_W), row(MIX_W), row(MIX_W),
                  _const_spec(wzr.shape), _const_spec(wmg.shape), _const_spec(ng.shape), _const_spec(nl.shape),
                  _const_spec(wup.shape), _const_spec(wout.shape), _const_spec(lg.shape), _const_spec(lb.shape)],
        out_specs=row(D_MODEL),
        out_shape=jax.ShapeDtypeStruct((m, D_MODEL), F32),
        compiler_params=_params("parallel"),
        name="mixer_merge",
    )(x, y_nsa, o_gdn, o_gla, wzr, wmg, ng, nl, wup, wout, lg, lb)


def _xattn_body(x_ref, kv_ref, wq_ref, wo_ref, lg_ref, lb_ref, o_ref, *, alpha):
    x = x_ref[...]
    q = _dot(x.astype(BF16), wq_ref[...]).astype(BF16)
    scale = XA_HD ** -0.5
    outs = []
    for h in range(XA_HEADS):
        hs = slice(h * XA_HD, (h + 1) * XA_HD)
        s = _dot_nt(q[:, hs], kv_ref[:, hs]) * scale
        m = jnp.max(s, -1, keepdims=True)
        e = jnp.exp(s - m)
        p = e / jnp.sum(e, -1, keepdims=True)
        outs.append(_dot(p.astype(BF16), kv_ref[:, D_MODEL + h * XA_HD:D_MODEL + (h + 1) * XA_HD]))
    o = jnp.concatenate(outs, axis=1)
    h_out = _dot(o.astype(BF16), wo_ref[...])
    o_ref[...] = _layer_norm(alpha * x + h_out, lg_ref[...], lb_ref[...])


def _xattn(x, kvm, wq, wo, lg, lb, alpha, batch, seq, tm=512):
    nt = seq // tm
    mt = kvm.shape[0] // batch
    cs = lambda shape: pl.BlockSpec(shape, lambda b, i: (0,) * len(shape), pipeline_mode=pl.Buffered(1))
    return pl.pallas_call(
        functools.partial(_xattn_body, alpha=alpha),
        grid=(batch, nt),
        in_specs=[pl.BlockSpec((tm, D_MODEL), lambda b, i: (b * nt + i, 0)),
                  pl.BlockSpec((mt, 2 * D_MODEL), lambda b, i: (b, 0)),
                  cs(wq.shape), cs(wo.shape), cs(lg.shape), cs(lb.shape)],
        out_specs=pl.BlockSpec((tm, D_MODEL), lambda b, i: (b * nt + i, 0)),
        out_shape=jax.ShapeDtypeStruct((batch * seq, D_MODEL), F32),
        compiler_params=_params("parallel", "parallel"),
        name="mem_xattn",
    )(x, kvm, wq, wo, lg, lb)


def _mlp_body(x_ref, w1_ref, w2_ref, lg_ref, lb_ref, o_ref, *, alpha):
    x = x_ref[...]
    h = jnp.maximum(_dot(x.astype(BF16), w1_ref[...]), 0.0)
    h = (h * h).astype(BF16)
    o_ref[...] = _layer_norm(alpha * x + _dot(h, w2_ref[...]), lg_ref[...], lb_ref[...])


def _mlp(x, w1, w2, lg, lb, alpha, tm=512):
    m = x.shape[0]
    return pl.pallas_call(
        functools.partial(_mlp_body, alpha=alpha),
        grid=(m // tm,),
        in_specs=[pl.BlockSpec((tm, D_MODEL), lambda i: (i, 0)),
                  _const_spec(w1.shape), _const_spec(w2.shape), _const_spec(lg.shape), _const_spec(lb.shape)],
        out_specs=pl.BlockSpec((tm, D_MODEL), lambda i: (i, 0)),
        out_shape=jax.ShapeDtypeStruct((m, D_MODEL), F32),
        compiler_params=_params("parallel"),
        name="sq_relu_mlp",
    )(x, w1, w2, lg, lb)


def _pad_cols(w, width):
    return jnp.pad(w, ((0, 0), (0, width - w.shape[1])))


def _nsa_weights(w_in):
    def kv(kind, g):
        o = O_NSA_KV + (kind * NSA_GROUPS + g) * NSA_HD
        return w_in[:, o:o + NSA_HD]
    cols = [w_in[:, O_NSA_Q:O_NSA_Q + NSA_HEADS * NSA_HD]]
    for g in range(NSA_GROUPS):
        cols += [kv(2, g), kv(4, g), kv(3, g), kv(5, g)]
    cols += [kv(0, 0), kv(0, 1), kv(1, 0), kv(1, 1)]
    main = jnp.concatenate(cols, axis=1).astype(BF16)
    gw = NSA_HPG * 3
    gates = [_pad_cols(w_in[:, O_NSA_G + g * gw:O_NSA_G + (g + 1) * gw], LANE) for g in range(NSA_GROUPS)]
    return main, jnp.concatenate(gates, axis=1).astype(BF16)


def _gla_weights(w_in, w_lr, b_lr):
    def pad_heads(w):
        lead = w.shape[0]
        w = w.reshape(lead, GLA_HEADS, GLA_DK)
        return jnp.pad(w, ((0, 0), (0, 0), (0, GLA_DV - GLA_DK))).reshape(lead, GLA_HEADS * GLA_DV)
    cols = [pad_heads(w_in[:, O_GLA_Q:O_GLA_Q + GLA_HEADS * GLA_DK]),
            pad_heads(w_in[:, O_GLA_K:O_GLA_K + GLA_HEADS * GLA_DK]),
            w_in[:, O_GLA_V:O_GLA_V + GLA_HEADS * GLA_DV],
            _pad_cols(w_in[:, O_GLA_LR:O_GLA_LR + GLA_RANK], LANE)]
    wlr = jnp.pad(pad_heads(w_lr), ((0, LANE - GLA_RANK), (0, 0)))
    return jnp.concatenate(cols, axis=1).astype(BF16), wlr, pad_heads(b_lr[None, :])


def _overlap_matrix(seq):
    nc = seq // CMP_STRIDE
    ns = seq // SEL_BLK
    c_start = np.arange(nc)[:, None] * CMP_STRIDE
    s_start = np.arange(ns)[None, :] * SEL_BLK
    return jnp.asarray((c_start < s_start + SEL_BLK) & (c_start + CMP_LEN > s_start), BF16)


def _nsa_branch(x, p, l, batch, seq):
    w_in = p["w_in"][l]
    w_main, w_gates = _nsa_weights(w_in)
    cmp_w = 2 * NSA_GROUPS * NSA_HD
    qkv, cmp_src, gate_logits = _mm(x, jnp.concatenate([w_main, w_gates], axis=1),
                                    [(2 * MIX_W, BF16), (cmp_w, BF16), (NSA_GROUPS * LANE, F32)])
    nc = seq // CMP_STRIDE
    half = CMP_STRIDE * NSA_HD
    cmp_in = cmp_src.reshape(batch * seq, 2 * NSA_GROUPS, NSA_HD).transpose(1, 0, 2)
    cmp_in = cmp_in.reshape(2 * NSA_GROUPS, batch * nc, half)
    w1 = p["cmp_w1"][l].reshape(2, 2, half, CMP_HID).astype(BF16)
    pe = jnp.broadcast_to(p["cmp_pe"][l].reshape(2, 2, 1, half), (2, 2, SUBLANE, half)).astype(BF16)
    cmp_tok = _compress(cmp_in, w1, p["cmp_b1"][l][:, None, :], pe, p["cmp_w2"][l].astype(BF16), batch)
    cmp_tok = cmp_tok.reshape(2 * NSA_GROUPS, batch, nc, NSA_HD)
    return _nsa_attention(qkv, gate_logits, cmp_tok, _overlap_matrix(seq), batch, seq)


def _gdn_branch(x, p, l, batch, seq):
    w_in = p["w_in"][l]
    w_gdn = jnp.concatenate([w_in[:, O_GDN_QKV:O_GDN_QKV + 3 * MIX_W],
                             _pad_cols(w_in[:, O_GDN_A:O_GDN_A + 2 * GDN_HEADS], LANE)], axis=1).astype(BF16)
    raw, ab = _mm(x, w_gdn, [(3 * MIX_W, F32), (LANE, F32)])
    alog = _pad_cols(p["gdn_a_log"][l][None, :], LANE)
    dtb = _pad_cols(p["gdn_dt_bias"][l][None, :], LANE)
    return _gdn(raw, ab, p["gdn_conv"][l], alog, dtb, batch, seq)


def _gla_branch(x, p, l, batch, seq):
    w_gla, wlr, blr = _gla_weights(p["w_in"][l], p["gla_w_lr"][l], p["gla_b_lr"][l])
    gla_in, = _mm(x, w_gla, [(w_gla.shape[1], F32)])
    ops, lmask = _gla_constants()
    return _gla(gla_in, wlr, blr, jnp.asarray(ops, BF16), jnp.asarray(lmask), batch, seq)


def _layer(x, kvm_src, p, l, batch, seq, alpha):
    w_in = p["w_in"][l]
    y_nsa = _nsa_branch(x, p, l, batch, seq)
    o_gdn = _gdn_branch(x, p, l, batch, seq)
    o_gla = _gla_branch(x, p, l, batch, seq)

    wzr = jnp.concatenate([w_in[:, O_GDN_Z:O_GDN_Z + MIX_W], w_in[:, O_GLA_R:O_GLA_R + MIX_W]], axis=1).astype(BF16)
    wmg = w_in[:, O_MERGE:O_MERGE + N_BRANCH * D_MODEL].astype(BF16)
    x = _merge(x, y_nsa, o_gdn, o_gla, wzr, wmg, p["gdn_norm"][l][None, :], p["gla_norm"][l][None, :],
               p["w_up"][l].astype(BF16), p["w_out"][l].astype(BF16),
               p["ln_g"][l, 0][None, :], p["ln_b"][l, 0][None, :], alpha)

    wkv = jnp.concatenate([p["xa_wk"][l], p["xa_wv"][l]], axis=1).astype(BF16)
    kvm, = _mm(kvm_src, wkv, [(2 * D_MODEL, BF16)])
    x = _xattn(x, kvm, p["xa_wq"][l].astype(BF16), p["xa_wo"][l].astype(BF16),
               p["ln_g"][l, 1][None, :], p["ln_b"][l, 1][None, :], alpha, batch, seq)

    return _mlp(x, p["mlp_w1"][l].astype(BF16), p["mlp_w2"][l].astype(BF16),
                p["ln_g"][l, 2][None, :], p["ln_b"][l, 2][None, :], alpha)


def kernel(x, mem, w_in, cmp_pe, cmp_w1, cmp_b1, cmp_w2, gdn_conv, gdn_a_log, gdn_dt_bias, gdn_norm, gla_w_lr,
           gla_b_lr, gla_norm, w_up, w_out, xa_wq, xa_wk, xa_wv, xa_wo, mlp_w1, mlp_w2, ln_g, ln_b):
    batch, seq, d = x.shape
    depth = w_in.shape[0]
    alpha = (2.0 * depth) ** 0.25
    p = dict(w_in=w_in, cmp_pe=cmp_pe, cmp_w1=cmp_w1, cmp_b1=cmp_b1, cmp_w2=cmp_w2, gdn_conv=gdn_conv,
             gdn_a_log=gdn_a_log, gdn_dt_bias=gdn_dt_bias, gdn_norm=gdn_norm, gla_w_lr=gla_w_lr,
             gla_b_lr=gla_b_lr, gla_norm=gla_norm, w_up=w_up, w_out=w_out, xa_wq=xa_wq, xa_wk=xa_wk,
             xa_wv=xa_wv, xa_wo=xa_wo, mlp_w1=mlp_w1, mlp_w2=mlp_w2, ln_g=ln_g, ln_b=ln_b)
    h = x.reshape(batch * seq, d)
    mem2 = mem.reshape(batch * mem.shape[1], d)
    for l in range(depth):
        h = _layer(h, mem2, p, l, batch, seq, alpha)
    return h.reshape(batch, seq, d)
```

```python
import functools

import numpy as np
import jax
import jax.numpy as jnp
from jax import lax
from jax.experimental import pallas as pl
from jax.experimental.pallas import tpu as pltpu

F32 = jnp.float32
BF16 = jnp.bfloat16

D_MODEL = 1024
MIX_W = D_MODEL // 2
N_BRANCH = 3
NSA_HEADS = 8
NSA_HD = MIX_W // NSA_HEADS
NSA_GROUPS = 2
NSA_HPG = NSA_HEADS // NSA_GROUPS
CMP_LEN = 32
CMP_STRIDE = 16
CMP_HID = D_MODEL // 4
SEL_BLK = 64
SEL_TOPK = 16
WIN = 512
Q_BLK = 256
GDN_HEADS = 4
GDN_HD = MIX_W // GDN_HEADS
CONV_K = 4
CHUNK = 64
GLA_HEADS = 4
GLA_DK = MIX_W // (2 * GLA_HEADS)
GLA_DV = MIX_W // GLA_HEADS
GLA_RANK = 16
GLA_TAU = 16.0
XA_HEADS = 4
XA_HD = D_MODEL // XA_HEADS
D_FF = 4 * D_MODEL
LN_EPS = 1e-5
NORM_EPS = 1e-6
NEG_INF = -1e30
FORCE_BONUS = 1e4

LANE = 128
SUBLANE = 8
VMEM_LIMIT_BYTES = 48 * 1024 * 1024
SEL_TILE = 512
SEL_HEAD = 128
NSA_ROWS = 16
NSA_RANGES = 4
REMOVED = -3e38
GLA_LEVELS = (32, 16, 8, 4, 2, 1)
Q_BLK_LOG2 = Q_BLK.bit_length() - 1
SEL_BLK_LOG2 = SEL_BLK.bit_length() - 1
CHUNK_LOG2 = CHUNK.bit_length() - 1

_IN_SIZES = (NSA_HEADS * NSA_HD, 6 * NSA_GROUPS * NSA_HD, NSA_HEADS * 3, 3 * MIX_W, GDN_HEADS, GDN_HEADS,
             MIX_W, GLA_HEADS * GLA_DK, GLA_HEADS * GLA_DK, GLA_HEADS * GLA_DV, GLA_RANK, MIX_W,
             N_BRANCH * D_MODEL)
_OFF = tuple(int(v) for v in np.cumsum((0,) + _IN_SIZES))
(O_NSA_Q, O_NSA_KV, O_NSA_G, O_GDN_QKV, O_GDN_A, O_GDN_B, O_GDN_Z, O_GLA_Q, O_GLA_K, O_GLA_V, O_GLA_LR,
 O_GLA_R, O_MERGE, _) = _OFF


def _params(*sem):
    return pltpu.CompilerParams(dimension_semantics=sem, vmem_limit_bytes=VMEM_LIMIT_BYTES)


def _dot(a, b):
    return jnp.dot(a, b, preferred_element_type=F32)


def _dot_nt(a, b):
    return lax.dot_general(a, b, (((1,), (1,)), ((), ())), preferred_element_type=F32)


def _dot_tn(a, b):
    return lax.dot_general(a, b, (((0,), (0,)), ((), ())), preferred_element_type=F32)


def _sigmoid(x):
    return 1.0 / (1.0 + jnp.exp(-x))


def _layer_norm(v, g, b):
    mu = jnp.mean(v, -1, keepdims=True)
    c = v - mu
    var = jnp.mean(c * c, -1, keepdims=True)
    return c * lax.rsqrt(var + LN_EPS) * g + b


def _split2(a):
    hi = a.astype(BF16)
    return hi, (a - hi.astype(F32)).astype(BF16)


def _dot_split(a, b):
    return _dot(a[0], b[0]) + (_dot(a[0], b[1]) + _dot(a[1], b[0]))


def _dot_exact_lhs(op, x):
    x1 = x.astype(BF16)
    r1 = x - x1.astype(F32)
    x2 = r1.astype(BF16)
    x3 = (r1 - x2.astype(F32)).astype(BF16)
    return _dot(op, x1) + (_dot(op, x2) + _dot(op, x3))


def _mm_body(x_ref, w_ref, *o_refs):
    res = _dot(x_ref[...].astype(BF16), w_ref[...])
    off = 0
    for o_ref in o_refs:
        o_ref[...] = res[:, off:off + o_ref.shape[1]].astype(o_ref.dtype)
        off += o_ref.shape[1]


def _mm(x, w, outs, tm=512):
    m, k = x.shape
    n = w.shape[1]
    assert n == sum(wd for wd, _ in outs) and all(wd % LANE == 0 for wd, _ in outs)
    tm = min(tm, m)
    return pl.pallas_call(
        _mm_body,
        grid=(m // tm,),
        in_specs=[pl.BlockSpec((tm, k), lambda i: (i, 0)),
                  pl.BlockSpec((k, n), lambda i: (0, 0))],
        out_specs=[pl.BlockSpec((tm, wd), lambda i: (i, 0)) for wd, _ in outs],
        out_shape=[jax.ShapeDtypeStruct((m, wd), dt) for wd, dt in outs],
        compiler_params=_params("parallel"),
        name="proj_mm",
    )(x, w)


def _compress_body(ch_ref, w1_ref, b1_ref, pe_ref, w2_ref, o_ref):
    nc = ch_ref.shape[0]
    ch = ch_ref[...]
    top = _dot(ch, w1_ref[0])
    bot = _dot(ch, w1_ref[1])
    bot_next = pltpu.roll(bot, nc - 1, axis=0)
    pe_term = _dot(pe_ref[0], w1_ref[0]) + _dot(pe_ref[1], w1_ref[1])
    h = top + bot_next + pe_term[0:1] + b1_ref[...]
    h = jax.nn.gelu(h, approximate=True)
    o_ref[...] = _dot(h.astype(BF16), w2_ref[...]).astype(o_ref.dtype)


def _compress(ch, w1, b1, pe, w2, batch):
    nc = ch.shape[1] // batch
    half = CMP_STRIDE * NSA_HD
    return pl.pallas_call(
        _compress_body,
        grid=(2, NSA_GROUPS, batch),
        in_specs=[pl.BlockSpec((None, nc, half), lambda t, g, b: (t * NSA_GROUPS + g, b, 0)),
                  pl.BlockSpec((None, 2, half, CMP_HID), lambda t, g, b: (t, 0, 0, 0)),
                  pl.BlockSpec((None, 1, CMP_HID), lambda t, g, b: (t, 0, 0)),
                  pl.BlockSpec((None, 2, SUBLANE, half), lambda t, g, b: (t, 0, 0, 0)),
                  pl.BlockSpec((None, CMP_HID, NSA_HD), lambda t, g, b: (t, 0, 0))],
        out_specs=pl.BlockSpec((None, nc, NSA_HD), lambda t, g, b: (t * NSA_GROUPS + g, b, 0)),
        out_shape=jax.ShapeDtypeStruct((2 * NSA_GROUPS, batch * nc, NSA_HD), BF16),
        compiler_params=_params("parallel", "parallel", "parallel"),
        name="nsa_compress",
    )(ch, w1, b1, pe, w2)


def _nsa_body(q_ref, kv_ref, kc_ref, vc_ref, gl_ref, ov_ref, o_ref, m_scr, acc_scr, tiles_ref, *,
              nc, ns, qb0):
    g = pl.program_id(1)
    qb = qb0 + pl.program_id(2)
    qs = qb * Q_BLK
    scale = NSA_HD ** -0.5
    n_chunks = Q_BLK // NSA_ROWS

    q4 = q_ref[...] * jnp.asarray(scale, q_ref.dtype)
    qst = jnp.concatenate([q4[:, h * NSA_HD:(h + 1) * NSA_HD] for h in range(NSA_HPG)], axis=0)
    slopes = [jnp.where(g == 0, 2.0 ** -(h + 1), 2.0 ** -(NSA_HPG + h + 1)).astype(F32) for h in range(NSA_HPG)]
    tq1 = qs + lax.broadcasted_iota(jnp.int32, (Q_BLK, 1), 0)

    def chunk_rows(h, rc):
        return slice(h * Q_BLK + rc * NSA_ROWS, h * Q_BLK + (rc + 1) * NSA_ROWS)

    order = [(h, rc) for h in range(NSA_HPG) for rc in range(n_chunks)]

    def stacked(parts):
        return jnp.concatenate([parts[i] for i in order], axis=0)

    def lane_blocks(x, op):
        out = x[:, 0:LANE]
        for j in range(1, x.shape[1] // LANE):
            out = op(out, x[:, j * LANE:(j + 1) * LANE])
        return out

    def lane_tiled(x, width):
        return jnp.concatenate([x] * (width // LANE), axis=1)

    def exp_pass(logit_parts, m_all, masks=None, want_sums=True):
        width = logit_parts[order[0]].shape[1]
        m_lanes = jnp.broadcast_to(m_all, (m_all.shape[0], LANE))
        e_parts, sums = {}, {}
        for h, rc in order:
            e = jnp.exp(logit_parts[h, rc] - lane_tiled(m_lanes[chunk_rows(h, rc)], width))
            if masks is not None:
                e = jnp.where(masks[rc], e, 0.0)
            e_parts[h, rc] = e
            if want_sums:
                sums[h, rc] = lane_blocks(e, jnp.add)
        return e_parts, (jnp.sum(stacked(sums), -1, keepdims=True) if want_sums else None)

    def masked_logits(s_all, key_pos, is_valid):
        parts, maxes, masks = {}, {}, {}
        for rc in range(n_chunks):
            dist = tq1[rc * NSA_ROWS:(rc + 1) * NSA_ROWS] - key_pos
            masks[rc] = is_valid(dist)
            d_f = dist.astype(F32)
            for h in range(NSA_HPG):
                s = jnp.where(masks[rc], s_all[chunk_rows(h, rc)] - slopes[h] * d_f, NEG_INF)
                parts[h, rc] = s
                maxes[h, rc] = lane_blocks(s, jnp.maximum)
        return parts, masks, jnp.max(stacked(maxes), -1, keepdims=True)

    c_end = lax.broadcasted_iota(jnp.int32, (1, nc), 1) * CMP_STRIDE + (CMP_LEN - 1)
    logit_c, mask_c, m_c = masked_logits(_dot_nt(qst, kc_ref[...]), c_end, lambda d: d >= 0)
    e_c, l_c = exp_pass(logit_c, m_c, mask_c)
    r_c = 1.0 / jnp.maximum(l_c, 1e-30)
    o_cmp = _dot(stacked({i: e_c[i].astype(BF16) for i in order}), vc_ref[...]) * r_c

    r_lanes = jnp.broadcast_to(r_c, (r_c.shape[0], LANE))
    p_hi, p_lo = [], []
    for rc in range(n_chunks):
        p_sum = None
        for h in range(NSA_HPG):
            p = e_c[h, rc] * lane_tiled(r_lanes[chunk_rows(h, rc)], nc)
            p_sum = p if p_sum is None else p_sum + p
        hi = p_sum.astype(BF16)
        p_hi.append(hi)
        p_lo.append((p_sum - hi.astype(F32)).astype(BF16))
    p_hi = jnp.concatenate(p_hi, axis=0)
    p_lo = jnp.concatenate(p_lo, axis=0)

    imp_t = (_dot(p_hi, ov_ref[...]) + _dot(p_lo, ov_ref[...])).T
    blk = lax.broadcasted_iota(jnp.int32, (ns, Q_BLK), 0)
    tq_row = qs + lax.broadcasted_iota(jnp.int32, (1, Q_BLK), 1)
    cur = tq_row >> SEL_BLK_LOG2
    forced = (blk == 0) | (blk == cur) | (blk == cur - 1)
    valid = blk * SEL_BLK <= tq_row
    score = jnp.where(forced, REMOVED, jnp.where(valid, imp_t, NEG_INF))
    assert FORCE_BONUS > NSA_HPG
    n_top = min(SEL_TOPK, ns)
    n_pick = (n_top - 1) - jnp.where(cur >= 1, 1, 0) - jnp.where(cur >= 2, 1, 0)
    blk_l = blk[:, 0:LANE]

    def pick_lanes(j):
        lanes = slice(j * LANE, (j + 1) * LANE)
        n_pick_l = n_pick[:, lanes]

        def pick(it, carry):
            sc, sel = carry
            m = jnp.max(sc, 0, keepdims=True)
            idx = jnp.min(jnp.where(sc == m, blk_l, ns), 0, keepdims=True)
            hit = blk_l == jnp.where(it < n_pick_l, idx, -1)
            return jnp.where(hit, REMOVED, sc), jnp.where(hit, 1.0, sel)

        trips = jnp.where(qs + j * LANE >= 2 * SEL_BLK, n_top - 3, n_top - 1)
        return lax.fori_loop(0, trips, pick, (score[:, lanes], jnp.where(forced[:, lanes], 1.0, 0.0)))[1]

    sel_t = jnp.concatenate([pick_lanes(j) for j in range(Q_BLK // LANE)], axis=1)
    sel = sel_t.T
    sel_bias = ((sel - 1.0) * -NEG_INF).astype(BF16)
    blk_any = jnp.max(sel, 0, keepdims=True)

    tile_blks = SEL_TILE // SEL_BLK
    head_blks = SEL_HEAD // SEL_BLK
    m_scr[...] = jnp.full(m_scr.shape, NEG_INF, F32)
    acc_scr[...] = jnp.zeros(acc_scr.shape, F32)
    v_lane = lax.broadcasted_iota(jnp.int32, (1, 2 * NSA_HD), 1)
    one_b = jnp.ones((), kv_ref.dtype)
    in_head = lax.broadcasted_iota(jnp.int32, (Q_BLK, ns), 1) < head_blks
    sel_bias_rest = jnp.where(in_head, NEG_INF, (sel - 1.0) * -NEG_INF).astype(BF16)
    blk_any_rest = jnp.where(in_head[0:1], 0.0, blk_any)

    def attend(k0, blk0, width, bias_src):
        st_all = _dot_nt(qst, kv_ref[pl.ds(k0, width), 0:NSA_HD])
        blk_row = lax.broadcasted_iota(jnp.int32, (ns, width), 0)
        blk_of_key = lax.broadcasted_iota(jnp.int32, (ns, width), 1) >> SEL_BLK_LOG2
        expand = jnp.where(blk_row == blk_of_key + blk0, 1.0, 0.0).astype(BF16)
        key_pos = k0 + lax.broadcasted_iota(jnp.int32, (1, width), 1)
        bias_all = jnp.where(tq1 - key_pos >= 0, _dot(bias_src, expand), NEG_INF)
        logits, maxes = {}, {}
        for rc in range(n_chunks):
            qr = slice(rc * NSA_ROWS, (rc + 1) * NSA_ROWS)
            d_f = (tq1[qr] - key_pos).astype(F32)
            for h in range(NSA_HPG):
                s = st_all[chunk_rows(h, rc)] + (bias_all[qr] - slopes[h] * d_f)
                logits[h, rc] = s
                maxes[h, rc] = lane_blocks(s, jnp.maximum)
        m_old = m_scr[...]
        m_new = jnp.maximum(m_old, jnp.max(stacked(maxes), -1, keepdims=True))
        p_parts, _ = exp_pass(logits, m_new, want_sums=False)
        alpha = jnp.exp(m_old - m_new)
        m_scr[...] = m_new
        v_one = jnp.where(v_lane < NSA_HD, kv_ref[pl.ds(k0, width), 2 * NSA_HD:4 * NSA_HD], one_b)
        p_all = stacked({i: p_parts[i].astype(BF16) for i in order})
        acc_scr[...] = alpha * acc_scr[...] + _dot(p_all, v_one)

    attend(0, 0, SEL_HEAD, sel_bias)

    tile_of_blk = lax.broadcasted_iota(jnp.int32, (ns, LANE), 0) >> (SEL_TILE.bit_length() - 1 - SEL_BLK_LOG2)
    in_tile = jnp.where(tile_of_blk == lax.broadcasted_iota(jnp.int32, (ns, LANE), 1), 1.0, 0.0).astype(BF16)
    tile_cnt = _dot(jnp.broadcast_to(blk_any_rest, (SUBLANE, ns)).astype(BF16), in_tile)
    n_visible = (qs + Q_BLK + SEL_TILE - 1) // SEL_TILE
    n_active = jnp.int32(0)
    for kt in range(ns // tile_blks):
        tiles_ref[n_active] = kt
        n_active = n_active + ((tile_cnt[0, kt] > 0.0) & (kt < n_visible)).astype(jnp.int32)

    def sel_tile(j, carry):
        kt = tiles_ref[j]
        attend(pl.multiple_of(kt * SEL_TILE, SEL_TILE), kt * tile_blks, SEL_TILE, sel_bias_rest)
        return carry

    lax.fori_loop(0, n_active, sel_tile, 0)
    acc = acc_scr[...]
    o_slc = acc[:, 0:NSA_HD] * (1.0 / jnp.maximum(acc[:, NSA_HD:NSA_HD + 1], 1e-30))

    wlen = WIN + Q_BLK
    w0 = pl.multiple_of(jnp.maximum(qs - WIN, 0), Q_BLK)
    pos_w = w0 + lax.broadcasted_iota(jnp.int32, (1, wlen), 1)
    logit_w, _, m_w = masked_logits(_dot_nt(qst, kv_ref[pl.ds(w0, wlen), NSA_HD:2 * NSA_HD]), pos_w,
                                    lambda d: (d >= 0) & (d < WIN))
    e_w, l_w = exp_pass(logit_w, m_w)
    o_win = (_dot(stacked({i: e_w[i].astype(BF16) for i in order}), kv_ref[pl.ds(w0, wlen), 3 * NSA_HD:4 * NSA_HD])
             * (1.0 / jnp.maximum(l_w, 1e-30)))

    gates = _sigmoid(gl_ref[...])
    outs = []
    for h in range(NSA_HPG):
        r = slice(h * Q_BLK, (h + 1) * Q_BLK)
        outs.append(gates[:, 3 * h:3 * h + 1] * o_cmp[r] + gates[:, 3 * h + 1:3 * h + 2] * o_slc[r]
                    + gates[:, 3 * h + 2:3 * h + 3] * o_win[r])
    o_ref[...] = jnp.concatenate(outs, axis=1).astype(o_ref.dtype)


def _round_up(v, m):
    return (v + m - 1) // m * m


def _nsa_attention(main, gate_logits, cmp_tok, overlap, batch, seq):
    nq = seq // Q_BLK
    gw = NSA_HPG * NSA_HD
    hq = NSA_HPG * Q_BLK
    n_ranges = min(NSA_RANGES, nq)
    per = nq // n_ranges
    outs = []
    for r in range(n_ranges):
        q0 = r * per
        seen = (q0 + per) * Q_BLK
        nc = min(seq // CMP_STRIDE, _round_up(seen // CMP_STRIDE, LANE))
        ns = min(seq // SEL_BLK, _round_up(seen // SEL_BLK, LANE))
        out = pl.pallas_call(
            functools.partial(_nsa_body, nc=nc, ns=ns, qb0=q0),
            grid=(batch, NSA_GROUPS, per),
            in_specs=[pl.BlockSpec((Q_BLK, gw), lambda b, g, i, q0=q0: (b * nq + q0 + i, g)),
                      pl.BlockSpec((seq, gw), lambda b, g, i: (b, NSA_GROUPS + g)),
                      pl.BlockSpec((None, None, nc, NSA_HD), lambda b, g, i: (g, b, 0, 0)),
                      pl.BlockSpec((None, None, nc, NSA_HD), lambda b, g, i: (NSA_GROUPS + g, b, 0, 0)),
                      pl.BlockSpec((Q_BLK, LANE), lambda b, g, i, q0=q0: (b * nq + q0 + i, g)),
                      pl.BlockSpec((nc, ns), lambda b, g, i: (0, 0))],
            out_specs=pl.BlockSpec((Q_BLK, gw), lambda b, g, i: (b * per + i, g)),
            out_shape=jax.ShapeDtypeStruct((batch * per * Q_BLK, MIX_W), BF16),
            scratch_shapes=[pltpu.VMEM((hq, 1), F32), pltpu.VMEM((hq, 2 * NSA_HD), F32),
                            pltpu.SMEM((ns * SEL_BLK // SEL_TILE,), jnp.int32)],
            compiler_params=_params("parallel", "parallel", "arbitrary"),
            name="nsa_attention",
        )(main, main, cmp_tok, cmp_tok, gate_logits, overlap)
        outs.append(out.reshape(batch, per * Q_BLK, MIX_W))
    return jnp.concatenate(outs, axis=1).reshape(batch * seq, MIX_W)


def _unit_lower_inverses(mats):
    n = mats[0].shape[0]
    eye = (lax.broadcasted_iota(jnp.int32, (n, n), 0) == lax.broadcasted_iota(jnp.int32, (n, n), 1)).astype(F32)
    xs = [_split2(eye - a) for a in mats]
    ps = [_split2(a) for a in mats]
    power = 1
    while 2 * power < CHUNK:
        ps = [_split2(_dot_split(p, p)) for p in ps]
        xs = [_split2((x[0].astype(F32) + x[1].astype(F32)) + _dot_split(x, p)) for x, p in zip(xs, ps)]
        power *= 2
    return xs


def _gdn_body(raw_ref, ab_ref, conv_ref, alog_ref, dtb_ref, o_ref, s_ref, tail_ref, buf_ref, *, tb):
    @pl.when(pl.program_id(1) == 0)
    def _():
        s_ref[...] = jnp.zeros_like(s_ref)
        tail_ref[...] = jnp.zeros_like(tail_ref)

    raw = raw_ref[...]
    buf_ref[0:SUBLANE] = tail_ref[...]
    buf_ref[SUBLANE:SUBLANE + tb] = raw
    tail_ref[...] = raw[tb - SUBLANE:tb]
    x = jnp.zeros_like(raw)
    for k in range(CONV_K):
        x = x + buf_ref[pl.ds(SUBLANE - (CONV_K - 1) + k, tb)] * conv_ref[k:k + 1]
    x = x * _sigmoid(x)

    ab = ab_ref[...]
    zed = ab + dtb_ref[...]
    softplus = jnp.maximum(zed, 0.0) + jnp.log1p(jnp.exp(-jnp.abs(zed)))
    g_all = -jnp.exp(alog_ref[...]) * softplus
    beta_all = _sigmoid(ab)

    c2 = 2 * CHUNK
    ri = lax.broadcasted_iota(jnp.int32, (c2, c2), 0)
    ci = lax.broadcasted_iota(jnp.int32, (c2, c2), 1)
    tri2 = jnp.where((ri >= ci) & ((ri >> CHUNK_LOG2) == (ci >> CHUNK_LOG2)), 1.0, 0.0).astype(BF16)
    hc = GDN_HEADS * CHUNK
    rs = lax.broadcasted_iota(jnp.int32, (hc, hc), 0)
    cs = lax.broadcasted_iota(jnp.int32, (hc, hc), 1)
    same_head = (rs >> CHUNK_LOG2) == (cs >> CHUNK_LOG2)
    tri = same_head & (rs >= cs)
    strict = same_head & (rs > cs)
    qscale = GDN_HD ** -0.5

    def l2n(v):
        return v * lax.rsqrt(jnp.sum(v * v, -1, keepdims=True) + NORM_EPS)

    def stack(fn):
        return jnp.concatenate([fn(h) for h in range(GDN_HEADS)], axis=0)

    chunks = []
    for pair in range(tb // c2):
        pr = slice(pair * c2, (pair + 1) * c2)
        gc2 = _dot_exact_lhs(tri2, g_all[pr])
        gc2t = gc2.T
        for half in range(2):
            rows = slice(pair * c2 + half * CHUNK, pair * c2 + (half + 1) * CHUNK)
            hr = slice(half * CHUNK, (half + 1) * CHUNK)
            gcol = stack(lambda h: gc2[hr, h:h + 1])
            grow = jnp.concatenate([gc2t[h:h + 1, hr] for h in range(GDN_HEADS)], axis=1)
            g_last = stack(lambda h: jnp.broadcast_to(gc2[hr, h:h + 1][CHUNK - 1:CHUNK], (CHUNK, 1)))
            beta = stack(lambda h: beta_all[rows, GDN_HEADS + h:GDN_HEADS + h + 1])
            qs_ = stack(lambda h: l2n(x[rows, h * GDN_HD:(h + 1) * GDN_HD])) * qscale
            ks_ = stack(lambda h: l2n(x[rows, MIX_W + h * GDN_HD:MIX_W + (h + 1) * GDN_HD]))
            vs_ = stack(lambda h: x[rows, 2 * MIX_W + h * GDN_HD:2 * MIX_W + (h + 1) * GDN_HD])
            decay = jnp.where(tri, jnp.exp(jnp.where(tri, gcol - grow, 0.0)), 0.0)
            kb = ks_ * beta
            ks_b = ks_.astype(BF16)
            egc = jnp.exp(gcol)
            chunks.append(dict(
                rows=rows,
                a_mat=jnp.where(strict, _dot_nt(kb.astype(BF16), ks_b) * decay, 0.0),
                rhs=_split2(jnp.concatenate([vs_ * beta, kb * egc], axis=1)),
                qk=jnp.where(tri, _dot_nt(qs_.astype(BF16), ks_b) * decay, 0.0).astype(BF16),
                q_dec=(qs_ * egc).astype(BF16),
                k_tail=(ks_ * jnp.exp(g_last - gcol)).astype(BF16),
                a_last=jnp.exp(g_last)))

    t_invs = _unit_lower_inverses([c["a_mat"] for c in chunks])
    sols = [_dot_split(t, c["rhs"]) for t, c in zip(t_invs, chunks)]

    hrows = [slice(h * CHUNK, (h + 1) * CHUNK) for h in range(GDN_HEADS)]
    for c, sol in zip(chunks, sols):
        states = [s_ref[h] for h in range(GDN_HEADS)]
        states_b = [st.astype(BF16) for st in states]
        v_new = stack(lambda h: sol[hrows[h], :GDN_HD] - _dot(sol[hrows[h], GDN_HD:].astype(BF16), states_b[h]))
        v_new_b = v_new.astype(BF16)
        o = _dot(c["qk"], v_new_b) + stack(lambda h: _dot(c["q_dec"][hrows[h]], states_b[h]))
        for h in range(GDN_HEADS):
            s_ref[h] = (states[h] * c["a_last"][hrows[h]][0:1]
                        + _dot_tn(c["k_tail"][hrows[h]], v_new_b[hrows[h]]))
            o_ref[c["rows"], h * GDN_HD:(h + 1) * GDN_HD] = o[hrows[h]]


def _gdn(raw, ab, conv, alog, dtb, batch, seq, tb=512):
    nt = seq // tb
    return pl.pallas_call(
        functools.partial(_gdn_body, tb=tb),
        grid=(batch, nt),
        in_specs=[pl.BlockSpec((tb, 3 * MIX_W), lambda b, i: (b * nt + i, 0)),
                  pl.BlockSpec((tb, LANE), lambda b, i: (b * nt + i, 0)),
                  pl.BlockSpec((CONV_K, 3 * MIX_W), lambda b, i: (0, 0)),
                  pl.BlockSpec((1, LANE), lambda b, i: (0, 0)),
                  pl.BlockSpec((1, LANE), lambda b, i: (0, 0))],
        out_specs=pl.BlockSpec((tb, MIX_W), lambda b, i: (b * nt + i, 0)),
        out_shape=jax.ShapeDtypeStruct((batch * seq, MIX_W), F32),
        scratch_shapes=[pltpu.VMEM((GDN_HEADS, GDN_HD, GDN_HD), F32),
                        pltpu.VMEM((SUBLANE, 3 * MIX_W), F32),
                        pltpu.VMEM((SUBLANE + tb, 3 * MIX_W), F32)],
        compiler_params=_params("parallel", "arbitrary"),
        name="gated_delta_net",
    )(raw, ab, conv, alog, dtb)


def _gla_constants():
    c = CHUNK
    t = np.arange(c)[:, None]
    s = np.arange(c)[None, :]
    ops = [(s <= t), (s > t)]
    masks = []
    for m in GLA_LEVELS:
        r = (t // (2 * m)) * (2 * m) + m
        upper = (t % (2 * m)) >= m
        ops.append(np.where(upper, (s > r) & (s <= t), (s > t) & (s <= r)))
        i, j = t, s
        masks.append((i // (2 * m) == j // (2 * m)) & ((i % (2 * m)) >= m) & ((j % (2 * m)) < m))
    masks.append(t == s)
    heads = np.eye(GLA_HEADS)
    masks = np.stack([np.kron(heads, m) for m in masks])
    return np.concatenate(ops, 0).astype(np.float32), masks.astype(np.float32)


def _gla_body(x_ref, wlr_ref, blr_ref, ops_ref, lm_ref, o_ref, s_ref, *, tb):
    @pl.when(pl.program_id(1) == 0)
    def _():
        s_ref[...] = jnp.zeros_like(s_ref)

    hw = GLA_DV
    kw = GLA_HEADS * hw
    qscale = GLA_DK ** -0.5
    n_lvl = len(GLA_LEVELS)
    hrows = [slice(h * CHUNK, (h + 1) * CHUNK) for h in range(GLA_HEADS)]

    def stack(fn):
        return jnp.concatenate([fn(h) for h in range(GLA_HEADS)], axis=0)

    for c in range(tb // CHUNK):
        rows = slice(c * CHUNK, (c + 1) * CHUNK)
        lr = x_ref[rows, 3 * kw:3 * kw + LANE]
        z = _dot(lr, wlr_ref[...]) + blr_ref[...]
        log_a = (jnp.minimum(z, 0.0) - jnp.log1p(jnp.exp(-jnp.abs(z)))) / GLA_TAU
        e_all = jnp.exp(_dot_exact_lhs(ops_ref[...], log_a))

        def factor(i):
            return stack(lambda h: e_all[i * CHUNK:(i + 1) * CHUNK, h * hw:(h + 1) * hw])

        qs_ = stack(lambda h: x_ref[rows, h * hw:(h + 1) * hw]) * qscale
        ks_ = stack(lambda h: x_ref[rows, kw + h * hw:kw + (h + 1) * hw])
        vs_b = stack(lambda h: x_ref[rows, 2 * kw + h * hw:2 * kw + (h + 1) * hw]).astype(BF16)
        scores = lm_ref[n_lvl] * _dot_nt(qs_.astype(BF16), ks_.astype(BF16))
        for lvl in range(n_lvl):
            xl = factor(2 + lvl)
            scores = scores + lm_ref[lvl] * _dot_nt((qs_ * xl).astype(BF16), (ks_ * xl).astype(BF16))
        e_cum = factor(0)
        q_dec = (qs_ * e_cum).astype(BF16)
        k_tail = (ks_ * factor(1)).astype(BF16)
        states = [s_ref[h] for h in range(GLA_HEADS)]
        o = (_dot(scores.astype(BF16), vs_b)
             + stack(lambda h: _dot_nt(q_dec[hrows[h]], states[h].astype(BF16))))
        for h in range(GLA_HEADS):
            s_ref[h] = (states[h] * e_cum[hrows[h]][CHUNK - 1:CHUNK]
                        + _dot_tn(vs_b[hrows[h]], k_tail[hrows[h]]))
            o_ref[rows, h * hw:(h + 1) * hw] = o[hrows[h]]


def _gla(x, wlr, blr, ops, lmask, batch, seq, tb=256):
    nt = seq // tb
    width = x.shape[1]
    return pl.pallas_call(
        functools.partial(_gla_body, tb=tb),
        grid=(batch, nt),
        in_specs=[pl.BlockSpec((tb, width), lambda b, i: (b * nt + i, 0)),
                  pl.BlockSpec(wlr.shape, lambda b, i: (0, 0)),
                  pl.BlockSpec(blr.shape, lambda b, i: (0, 0)),
                  pl.BlockSpec(ops.shape, lambda b, i: (0, 0)),
                  pl.BlockSpec(lmask.shape, lambda b, i: (0, 0, 0))],
        out_specs=pl.BlockSpec((tb, MIX_W), lambda b, i: (b * nt + i, 0)),
        out_shape=jax.ShapeDtypeStruct((batch * seq, MIX_W), F32),
        scratch_shapes=[pltpu.VMEM((GLA_HEADS, GLA_DV, GLA_DV), F32)],
        compiler_params=_params("parallel", "arbitrary"),
        name="gla",
    )(x, wlr, blr, ops, lmask)


def _merge_body(x_ref, yn_ref, og_ref, ol_ref, wzr_ref, wmg_ref, ng_ref, nl_ref, wup_ref, wout_ref,
                lg_ref, lb_ref, o_ref, *, alpha):
    x = x_ref[...]
    xb = x.astype(BF16)
    zr = _dot(xb, wzr_ref[...])

    def head_rms(o, gain):
        parts = []
        for h in range(MIX_W // LANE):
            oh = o[:, h * LANE:(h + 1) * LANE]
            parts.append(oh * lax.rsqrt(jnp.mean(oh * oh, -1, keepdims=True) + NORM_EPS) * gain)
        return jnp.concatenate(parts, axis=1)

    z = zr[:, :MIX_W]
    r = zr[:, MIX_W:]
    y_gdn = head_rms(og_ref[...], ng_ref[...]) * (z * _sigmoid(z))
    y_gla = head_rms(ol_ref[...], nl_ref[...]) * (r * _sigmoid(r))
    ys = (yn_ref[...], y_gdn.astype(BF16), y_gla.astype(BF16))
    merged = jnp.zeros(x.shape, F32)
    for br in range(N_BRANCH):
        gate = _sigmoid(_dot(xb, wmg_ref[:, br * D_MODEL:(br + 1) * D_MODEL]))
        merged = merged + gate * _dot(ys[br], wup_ref[br])
    h = _dot(merged.astype(BF16), wout_ref[...])
    o_ref[...] = _layer_norm(alpha * x + h, lg_ref[...], lb_ref[...])


def _const_spec(shape):
    nd = len(shape)
    return pl.BlockSpec(shape, lambda i: (0,) * nd, pipeline_mode=pl.Buffered(1))


def _merge(x, y_nsa, o_gdn, o_gla, wzr, wmg, ng, nl, wup, wout, lg, lb, alpha, tm=256):
    m = x.shape[0]
    row = lambda w: pl.BlockSpec((tm, w), lambda i: (i, 0))
    return pl.pallas_call(
        functools.partial(_merge_body, alpha=alpha),
        grid=(m // tm,),
        in_specs=[row(D_MODEL), row(MIX_W), row(MIX_W), row(MIX_W),
                  _const_spec(wzr.shape), _const_spec(wmg.shape), _const_spec(ng.shape), _const_spec(nl.shape),
                  _const_spec(wup.shape), _const_spec(wout.shape), _const_spec(lg.shape), _const_spec(lb.shape)],
        out_specs=row(D_MODEL),
        out_shape=jax.ShapeDtypeStruct((m, D_MODEL), F32),
        compiler_params=_params("parallel"),
        name="mixer_merge",
    )(x, y_nsa, o_gdn, o_gla, wzr, wmg, ng, nl, wup, wout, lg, lb)


def _xattn_body(x_ref, kv_ref, wq_ref, wo_ref, lg_ref, lb_ref, o_ref, *, alpha):
    x = x_ref[...]
    q = _dot(x.astype(BF16), wq_ref[...]).astype(BF16)
    scale = XA_HD ** -0.5
    outs = []
    for h in range(XA_HEADS):
        hs = slice(h * XA_HD, (h + 1) * XA_HD)
        s = _dot_nt(q[:, hs], kv_ref[:, hs]) * scale
        m = jnp.max(s, -1, keepdims=True)
        e = jnp.exp(s - m)
        p = e / jnp.sum(e, -1, keepdims=True)
        outs.append(_dot(p.astype(BF16), kv_ref[:, D_MODEL + h * XA_HD:D_MODEL + (h + 1) * XA_HD]))
    o = jnp.concatenate(outs, axis=1)
    h_out = _dot(o.astype(BF16), wo_ref[...])
    o_ref[...] = _layer_norm(alpha * x + h_out, lg_ref[...], lb_ref[...])


def _xattn(x, kvm, wq, wo, lg, lb, alpha, batch, seq, tm=512):
    nt = seq // tm
    mt = kvm.shape[0] // batch
    cs = lambda shape: pl.BlockSpec(shape, lambda b, i: (0,) * len(shape), pipeline_mode=pl.Buffered(1))
    return pl.pallas_call(
        functools.partial(_xattn_body, alpha=alpha),
        grid=(batch, nt),
        in_specs=[pl.BlockSpec((tm, D_MODEL), lambda b, i: (b * nt + i, 0)),
                  pl.BlockSpec((mt, 2 * D_MODEL), lambda b, i: (b, 0)),
                  cs(wq.shape), cs(wo.shape), cs(lg.shape), cs(lb.shape)],
        out_specs=pl.BlockSpec((tm, D_MODEL), lambda b, i: (b * nt + i, 0)),
        out_shape=jax.ShapeDtypeStruct((batch * seq, D_MODEL), F32),
        compiler_params=_params("parallel", "parallel"),
        name="mem_xattn",
    )(x, kvm, wq, wo, lg, lb)


def _mlp_body(x_ref, w1_ref, w2_ref, lg_ref, lb_ref, o_ref, *, alpha):
    x = x_ref[...]
    h = jnp.maximum(_dot(x.astype(BF16), w1_ref[...]), 0.0)
    h = (h * h).astype(BF16)
    o_ref[...] = _layer_norm(alpha * x + _dot(h, w2_ref[...]), lg_ref[...], lb_ref[...])


def _mlp(x, w1, w2, lg, lb, alpha, tm=512):
    m = x.shape[0]
    return pl.pallas_call(
        functools.partial(_mlp_body, alpha=alpha),
        grid=(m // tm,),
        in_specs=[pl.BlockSpec((tm, D_MODEL), lambda i: (i, 0)),
                  _const_spec(w1.shape), _const_spec(w2.shape), _const_spec(lg.shape), _const_spec(lb.shape)],
        out_specs=pl.BlockSpec((tm, D_MODEL), lambda i: (i, 0)),
        out_shape=jax.ShapeDtypeStruct((m, D_MODEL), F32),
        compiler_params=_params("parallel"),
        name="sq_relu_mlp",
    )(x, w1, w2, lg, lb)


def _pad_cols(w, width):
    return jnp.pad(w, ((0, 0), (0, width - w.shape[1])))


def _nsa_weights(w_in):
    def kv(kind, g):
        o = O_NSA_KV + (kind * NSA_GROUPS + g) * NSA_HD
        return w_in[:, o:o + NSA_HD]
    cols = [w_in[:, O_NSA_Q:O_NSA_Q + NSA_HEADS * NSA_HD]]
    for g in range(NSA_GROUPS):
        cols += [kv(2, g), kv(4, g), kv(3, g), kv(5, g)]
    cols += [kv(0, 0), kv(0, 1), kv(1, 0), kv(1, 1)]
    main = jnp.concatenate(cols, axis=1).astype(BF16)
    gw = NSA_HPG * 3
    gates = [_pad_cols(w_in[:, O_NSA_G + g * gw:O_NSA_G + (g + 1) * gw], LANE) for g in range(NSA_GROUPS)]
    return main, jnp.concatenate(gates, axis=1).astype(BF16)


def _gla_weights(w_in, w_lr, b_lr):
    def pad_heads(w):
        lead = w.shape[0]
        w = w.reshape(lead, GLA_HEADS, GLA_DK)
        return jnp.pad(w, ((0, 0), (0, 0), (0, GLA_DV - GLA_DK))).reshape(lead, GLA_HEADS * GLA_DV)
    cols = [pad_heads(w_in[:, O_GLA_Q:O_GLA_Q + GLA_HEADS * GLA_DK]),
            pad_heads(w_in[:, O_GLA_K:O_GLA_K + GLA_HEADS * GLA_DK]),
            w_in[:, O_GLA_V:O_GLA_V + GLA_HEADS * GLA_DV],
            _pad_cols(w_in[:, O_GLA_LR:O_GLA_LR + GLA_RANK], LANE)]
    wlr = jnp.pad(pad_heads(w_lr), ((0, LANE - GLA_RANK), (0, 0)))
    return jnp.concatenate(cols, axis=1).astype(BF16), wlr, pad_heads(b_lr[None, :])


def _overlap_matrix(seq):
    nc = seq // CMP_STRIDE
    ns = seq // SEL_BLK
    c_start = np.arange(nc)[:, None] * CMP_STRIDE
    s_start = np.arange(ns)[None, :] * SEL_BLK
    return jnp.asarray((c_start < s_start + SEL_BLK) & (c_start + CMP_LEN > s_start), BF16)


def _nsa_branch(x, p, l, batch, seq):
    w_in = p["w_in"][l]
    w_main, w_gates = _nsa_weights(w_in)
    cmp_w = 2 * NSA_GROUPS * NSA_HD
    qkv, cmp_src, gate_logits = _mm(x, jnp.concatenate([w_main, w_gates], axis=1),
                                    [(2 * MIX_W, BF16), (cmp_w, BF16), (NSA_GROUPS * LANE, F32)])
    nc = seq // CMP_STRIDE
    half = CMP_STRIDE * NSA_HD
    cmp_in = cmp_src.reshape(batch * seq, 2 * NSA_GROUPS, NSA_HD).transpose(1, 0, 2)
    cmp_in = cmp_in.reshape(2 * NSA_GROUPS, batch * nc, half)
    w1 = p["cmp_w1"][l].reshape(2, 2, half, CMP_HID).astype(BF16)
    pe = jnp.broadcast_to(p["cmp_pe"][l].reshape(2, 2, 1, half), (2, 2, SUBLANE, half)).astype(BF16)
    cmp_tok = _compress(cmp_in, w1, p["cmp_b1"][l][:, None, :], pe, p["cmp_w2"][l].astype(BF16), batch)
    cmp_tok = cmp_tok.reshape(2 * NSA_GROUPS, batch, nc, NSA_HD)
    return _nsa_attention(qkv, gate_logits, cmp_tok, _overlap_matrix(seq), batch, seq)


def _gdn_branch(x, p, l, batch, seq):
    w_in = p["w_in"][l]
    w_gdn = jnp.concatenate([w_in[:, O_GDN_QKV:O_GDN_QKV + 3 * MIX_W],
                             _pad_cols(w_in[:, O_GDN_A:O_GDN_A + 2 * GDN_HEADS], LANE)], axis=1).astype(BF16)
    raw, ab = _mm(x, w_gdn, [(3 * MIX_W, F32), (LANE, F32)])
    alog = _pad_cols(p["gdn_a_log"][l][None, :], LANE)
    dtb = _pad_cols(p["gdn_dt_bias"][l][None, :], LANE)
    return _gdn(raw, ab, p["gdn_conv"][l], alog, dtb, batch, seq)


def _gla_branch(x, p, l, batch, seq):
    w_gla, wlr, blr = _gla_weights(p["w_in"][l], p["gla_w_lr"][l], p["gla_b_lr"][l])
    gla_in, = _mm(x, w_gla, [(w_gla.shape[1], F32)])
    ops, lmask = _gla_constants()
    return _gla(gla_in, wlr, blr, jnp.asarray(ops, BF16), jnp.asarray(lmask), batch, seq)


def _layer(x, kvm_src, p, l, batch, seq, alpha):
    w_in = p["w_in"][l]
    y_nsa = _nsa_branch(x, p, l, batch, seq)
    o_gdn = _gdn_branch(x, p, l, batch, seq)
    o_gla = _gla_branch(x, p, l, batch, seq)

    wzr = jnp.concatenate([w_in[:, O_GDN_Z:O_GDN_Z + MIX_W], w_in[:, O_GLA_R:O_GLA_R + MIX_W]], axis=1).astype(BF16)
    wmg = w_in[:, O_MERGE:O_MERGE + N_BRANCH * D_MODEL].astype(BF16)
    x = _merge(x, y_nsa, o_gdn, o_gla, wzr, wmg, p["gdn_norm"][l][None, :], p["gla_norm"][l][None, :],
               p["w_up"][l].astype(BF16), p["w_out"][l].astype(BF16),
               p["ln_g"][l, 0][None, :], p["ln_b"][l, 0][None, :], alpha)

    wkv = jnp.concatenate([p["xa_wk"][l], p["xa_wv"][l]], axis=1).astype(BF16)
    kvm, = _mm(kvm_src, wkv, [(2 * D_MODEL, BF16)])
    x = _xattn(x, kvm, p["xa_wq"][l].astype(BF16), p["xa_wo"][l].astype(BF16),
               p["ln_g"][l, 1][None, :], p["ln_b"][l, 1][None, :], alpha, batch, seq)

    return _mlp(x, p["mlp_w1"][l].astype(BF16), p["mlp_w2"][l].astype(BF16),
                p["ln_g"][l, 2][None, :], p["ln_b"][l, 2][None, :], alpha)


def kernel(x, mem, w_in, cmp_pe, cmp_w1, cmp_b1, cmp_w2, gdn_conv, gdn_a_log, gdn_dt_bias, gdn_norm, gla_w_lr,
           gla_b_lr, gla_norm, w_up, w_out, xa_wq, xa_wk, xa_wv, xa_wo, mlp_w1, mlp_w2, ln_g, ln_b):
    batch, seq, d = x.shape
    depth = w_in.shape[0]
    alpha = (2.0 * depth) ** 0.25
    p = dict(w_in=w_in, cmp_pe=cmp_pe, cmp_w1=cmp_w1, cmp_b1=cmp_b1, cmp_w2=cmp_w2, gdn_conv=gdn_conv,
             gdn_a_log=gdn_a_log, gdn_dt_bias=gdn_dt_bias, gdn_norm=gdn_norm, gla_w_lr=gla_w_lr,
             gla_b_lr=gla_b_lr, gla_norm=gla_norm, w_up=w_up, w_out=w_out, xa_wq=xa_wq, xa_wk=xa_wk,
             xa_wv=xa_wv, xa_wo=xa_wo, mlp_w1=mlp_w1, mlp_w2=mlp_w2, ln_g=ln_g, ln_b=ln_b)
    h = x.reshape(batch * seq, d)
    mem2 = mem.reshape(batch * mem.shape[1], d)
    for l in range(depth):
        h = _layer(h, mem2, p, l, batch, seq, alpha)
    return h.reshape(batch, seq, d)
```

```python
import functools

import numpy as np
import jax
import jax.numpy as jnp
from jax import lax
from jax.experimental import pallas as pl
from jax.experimental.pallas import tpu as pltpu

F32 = jnp.float32
BF16 = jnp.bfloat16

D_MODEL = 1024
MIX_W = D_MODEL // 2
N_BRANCH = 3
NSA_HEADS = 8
NSA_HD = MIX_W // NSA_HEADS
NSA_GROUPS = 2
NSA_HPG = NSA_HEADS // NSA_GROUPS
CMP_LEN = 32
CMP_STRIDE = 16
CMP_HID = D_MODEL // 4
SEL_BLK = 64
SEL_TOPK = 16
WIN = 512
Q_BLK = 256
GDN_HEADS = 4
GDN_HD = MIX_W // GDN_HEADS
CONV_K = 4
CHUNK = 64
GLA_HEADS = 4
GLA_DK = MIX_W // (2 * GLA_HEADS)
GLA_DV = MIX_W // GLA_HEADS
GLA_RANK = 16
GLA_TAU = 16.0
XA_HEADS = 4
XA_HD = D_MODEL // XA_HEADS
D_FF = 4 * D_MODEL
LN_EPS = 1e-5
NORM_EPS = 1e-6
NEG_INF = -1e30
FORCE_BONUS = 1e4

LANE = 128
SUBLANE = 8
VMEM_LIMIT_BYTES = 48 * 1024 * 1024
SEL_TILE = 512
SEL_HEAD = 128
NSA_ROWS = 16
NSA_RANGES = 4
REMOVED = -3e38
GLA_LEVELS = (32, 16, 8, 4, 2, 1)
Q_BLK_LOG2 = Q_BLK.bit_length() - 1
SEL_BLK_LOG2 = SEL_BLK.bit_length() - 1
CHUNK_LOG2 = CHUNK.bit_length() - 1

_IN_SIZES = (NSA_HEADS * NSA_HD, 6 * NSA_GROUPS * NSA_HD, NSA_HEADS * 3, 3 * MIX_W, GDN_HEADS, GDN_HEADS,
             MIX_W, GLA_HEADS * GLA_DK, GLA_HEADS * GLA_DK, GLA_HEADS * GLA_DV, GLA_RANK, MIX_W,
             N_BRANCH * D_MODEL)
_OFF = tuple(int(v) for v in np.cumsum((0,) + _IN_SIZES))
(O_NSA_Q, O_NSA_KV, O_NSA_G, O_GDN_QKV, O_GDN_A, O_GDN_B, O_GDN_Z, O_GLA_Q, O_GLA_K, O_GLA_V, O_GLA_LR,
 O_GLA_R, O_MERGE, _) = _OFF


def _params(*sem):
    return pltpu.CompilerParams(dimension_semantics=sem, vmem_limit_bytes=VMEM_LIMIT_BYTES)


def _dot(a, b):
    return jnp.dot(a, b, preferred_element_type=F32)


def _dot_nt(a, b):
    return lax.dot_general(a, b, (((1,), (1,)), ((), ())), preferred_element_type=F32)


def _dot_tn(a, b):
    return lax.dot_general(a, b, (((0,), (0,)), ((), ())), preferred_element_type=F32)


def _sigmoid(x):
    return 1.0 / (1.0 + jnp.exp(-x))


def _layer_norm(v, g, b):
    mu = jnp.mean(v, -1, keepdims=True)
    c = v - mu
    var = jnp.mean(c * c, -1, keepdims=True)
    return c * lax.rsqrt(var + LN_EPS) * g + b


def _split2(a):
    hi = a.astype(BF16)
    return hi, (a - hi.astype(F32)).astype(BF16)


def _dot_split(a, b):
    return _dot(a[0], b[0]) + (_dot(a[0], b[1]) + _dot(a[1], b[0]))


def _dot_exact_lhs(op, x):
    x1 = x.astype(BF16)
    r1 = x - x1.astype(F32)
    x2 = r1.astype(BF16)
    x3 = (r1 - x2.astype(F32)).astype(BF16)
    return _dot(op, x1) + (_dot(op, x2) + _dot(op, x3))


def _mm_body(x_ref, w_ref, *o_refs):
    res = _dot(x_ref[...].astype(BF16), w_ref[...])
    off = 0
    for o_ref in o_refs:
        o_ref[...] = res[:, off:off + o_ref.shape[1]].astype(o_ref.dtype)
        off += o_ref.shape[1]


def _mm(x, w, outs, tm=1024):
    m, k = x.shape
    n = w.shape[1]
    assert n == sum(wd for wd, _ in outs) and all(wd % LANE == 0 for wd, _ in outs)
    tm = min(tm, m)
    return pl.pallas_call(
        _mm_body,
        grid=(m // tm,),
        in_specs=[pl.BlockSpec((tm, k), lambda i: (i, 0)),
                  pl.BlockSpec((k, n), lambda i: (0, 0))],
        out_specs=[pl.BlockSpec((tm, wd), lambda i: (i, 0)) for wd, _ in outs],
        out_shape=[jax.ShapeDtypeStruct((m, wd), dt) for wd, dt in outs],
        compiler_params=_params("parallel"),
        name="proj_mm",
    )(x, w)


def _compress_body(ch_ref, w1_ref, b1_ref, pe_ref, w2_ref, o_ref):
    nc = ch_ref.shape[0]
    ch = ch_ref[...]
    top = _dot(ch, w1_ref[0])
    bot = _dot(ch, w1_ref[1])
    bot_next = pltpu.roll(bot, nc - 1, axis=0)
    pe_term = _dot(pe_ref[0], w1_ref[0]) + _dot(pe_ref[1], w1_ref[1])
    h = top + bot_next + pe_term[0:1] + b1_ref[...]
    h = jax.nn.gelu(h, approximate=True)
    o_ref[...] = _dot(h.astype(BF16), w2_ref[...]).astype(o_ref.dtype)


def _compress(ch, w1, b1, pe, w2, batch):
    nc = ch.shape[1] // batch
    half = CMP_STRIDE * NSA_HD
    return pl.pallas_call(
        _compress_body,
        grid=(2, NSA_GROUPS, batch),
        in_specs=[pl.BlockSpec((None, nc, half), lambda t, g, b: (t * NSA_GROUPS + g, b, 0)),
                  pl.BlockSpec((None, 2, half, CMP_HID), lambda t, g, b: (t, 0, 0, 0)),
                  pl.BlockSpec((None, 1, CMP_HID), lambda t, g, b: (t, 0, 0)),
                  pl.BlockSpec((None, 2, SUBLANE, half), lambda t, g, b: (t, 0, 0, 0)),
                  pl.BlockSpec((None, CMP_HID, NSA_HD), lambda t, g, b: (t, 0, 0))],
        out_specs=pl.BlockSpec((None, nc, NSA_HD), lambda t, g, b: (t * NSA_GROUPS + g, b, 0)),
        out_shape=jax.ShapeDtypeStruct((2 * NSA_GROUPS, batch * nc, NSA_HD), BF16),
        compiler_params=_params("parallel", "parallel", "parallel"),
        name="nsa_compress",
    )(ch, w1, b1, pe, w2)


def _nsa_body(q_ref, kv_ref, kc_ref, vc_ref, gl_ref, ov_ref, o_ref, m_scr, acc_scr, tiles_ref, *,
              nc, ns, qb0):
    g = pl.program_id(1)
    qb = qb0 + pl.program_id(2)
    qs = qb * Q_BLK
    scale = NSA_HD ** -0.5
    n_chunks = Q_BLK // NSA_ROWS

    q4 = q_ref[...] * jnp.asarray(scale, q_ref.dtype)
    qst = jnp.concatenate([q4[:, h * NSA_HD:(h + 1) * NSA_HD] for h in range(NSA_HPG)], axis=0)
    slopes = [jnp.where(g == 0, 2.0 ** -(h + 1), 2.0 ** -(NSA_HPG + h + 1)).astype(F32) for h in range(NSA_HPG)]
    tq1 = qs + lax.broadcasted_iota(jnp.int32, (Q_BLK, 1), 0)

    def chunk_rows(h, rc):
        return slice(h * Q_BLK + rc * NSA_ROWS, h * Q_BLK + (rc + 1) * NSA_ROWS)

    order = [(h, rc) for h in range(NSA_HPG) for rc in range(n_chunks)]

    def stacked(parts):
        return jnp.concatenate([parts[i] for i in order], axis=0)

    def lane_blocks(x, op):
        out = x[:, 0:LANE]
        for j in range(1, x.shape[1] // LANE):
            out = op(out, x[:, j * LANE:(j + 1) * LANE])
        return out

    def lane_tiled(x, width):
        return jnp.concatenate([x] * (width // LANE), axis=1)

    def exp_pass(logit_parts, m_all, masks=None, want_sums=True):
        width = logit_parts[order[0]].shape[1]
        m_lanes = jnp.broadcast_to(m_all, (m_all.shape[0], LANE))
        e_parts, sums = {}, {}
        for h, rc in order:
            e = jnp.exp(logit_parts[h, rc] - lane_tiled(m_lanes[chunk_rows(h, rc)], width))
            if masks is not None:
                e = jnp.where(masks[rc], e, 0.0)
            e_parts[h, rc] = e
            if want_sums:
                sums[h, rc] = lane_blocks(e, jnp.add)
        return e_parts, (jnp.sum(stacked(sums), -1, keepdims=True) if want_sums else None)

    def masked_logits(s_all, key_pos, is_valid):
        parts, maxes, masks = {}, {}, {}
        for rc in range(n_chunks):
            dist = tq1[rc * NSA_ROWS:(rc + 1) * NSA_ROWS] - key_pos
            masks[rc] = is_valid(dist)
            d_f = dist.astype(F32)
            for h in range(NSA_HPG):
                s = jnp.where(masks[rc], s_all[chunk_rows(h, rc)] - slopes[h] * d_f, NEG_INF)
                parts[h, rc] = s
                maxes[h, rc] = lane_blocks(s, jnp.maximum)
        return parts, masks, jnp.max(stacked(maxes), -1, keepdims=True)

    c_end = lax.broadcasted_iota(jnp.int32, (1, nc), 1) * CMP_STRIDE + (CMP_LEN - 1)
    logit_c, mask_c, m_c = masked_logits(_dot_nt(qst, kc_ref[...]), c_end, lambda d: d >= 0)
    e_c, l_c = exp_pass(logit_c, m_c, mask_c)
    r_c = 1.0 / jnp.maximum(l_c, 1e-30)
    o_cmp = _dot(stacked({i: e_c[i].astype(BF16) for i in order}), vc_ref[...]) * r_c

    r_lanes = jnp.broadcast_to(r_c, (r_c.shape[0], LANE))
    p_hi, p_lo = [], []
    for rc in range(n_chunks):
        p_sum = None
        for h in range(NSA_HPG):
            p = e_c[h, rc] * lane_tiled(r_lanes[chunk_rows(h, rc)], nc)
            p_sum = p if p_sum is None else p_sum + p
        hi = p_sum.astype(BF16)
        p_hi.append(hi)
        p_lo.append((p_sum - hi.astype(F32)).astype(BF16))
    p_hi = jnp.concatenate(p_hi, axis=0)
    p_lo = jnp.concatenate(p_lo, axis=0)

    imp_t = (_dot(p_hi, ov_ref[...]) + _dot(p_lo, ov_ref[...])).T
    blk = lax.broadcasted_iota(jnp.int32, (ns, Q_BLK), 0)
    tq_row = qs + lax.broadcasted_iota(jnp.int32, (1, Q_BLK), 1)
    cur = tq_row >> SEL_BLK_LOG2
    forced = (blk == 0) | (blk == cur) | (blk == cur - 1)
    valid = blk * SEL_BLK <= tq_row
    score = jnp.where(forced, REMOVED, jnp.where(valid, imp_t, NEG_INF))
    assert FORCE_BONUS > NSA_HPG
    n_top = min(SEL_TOPK, ns)
    n_pick = (n_top - 1) - jnp.where(cur >= 1, 1, 0) - jnp.where(cur >= 2, 1, 0)
    blk_l = blk[:, 0:LANE]

    def pick_lanes(j):
        lanes = slice(j * LANE, (j + 1) * LANE)
        n_pick_l = n_pick[:, lanes]

        def pick(it, carry):
            sc, sel = carry
            m = jnp.max(sc, 0, keepdims=True)
            idx = jnp.min(jnp.where(sc == m, blk_l, ns), 0, keepdims=True)
            hit = blk_l == jnp.where(it < n_pick_l, idx, -1)
            return jnp.where(hit, REMOVED, sc), jnp.where(hit, 1.0, sel)

        trips = jnp.where(qs + j * LANE >= 2 * SEL_BLK, n_top - 3, n_top - 1)
        return lax.fori_loop(0, trips, pick, (score[:, lanes], jnp.where(forced[:, lanes], 1.0, 0.0)))[1]

    sel_t = jnp.concatenate([pick_lanes(j) for j in range(Q_BLK // LANE)], axis=1)
    sel = sel_t.T
    sel_bias = ((sel - 1.0) * -NEG_INF).astype(BF16)
    blk_any = jnp.max(sel, 0, keepdims=True)

    tile_blks = SEL_TILE // SEL_BLK
    head_blks = SEL_HEAD // SEL_BLK
    m_scr[...] = jnp.full(m_scr.shape, NEG_INF, F32)
    acc_scr[...] = jnp.zeros(acc_scr.shape, F32)
    v_lane = lax.broadcasted_iota(jnp.int32, (1, 2 * NSA_HD), 1)
    one_b = jnp.ones((), kv_ref.dtype)
    in_head = lax.broadcasted_iota(jnp.int32, (Q_BLK, ns), 1) < head_blks
    sel_bias_rest = jnp.where(in_head, NEG_INF, (sel - 1.0) * -NEG_INF).astype(BF16)
    blk_any_rest = jnp.where(in_head[0:1], 0.0, blk_any)

    def attend(k0, blk0, width, bias_src):
        st_all = _dot_nt(qst, kv_ref[pl.ds(k0, width), 0:NSA_HD])
        blk_row = lax.broadcasted_iota(jnp.int32, (ns, width), 0)
        blk_of_key = lax.broadcasted_iota(jnp.int32, (ns, width), 1) >> SEL_BLK_LOG2
        expand = jnp.where(blk_row == blk_of_key + blk0, 1.0, 0.0).astype(BF16)
        key_pos = k0 + lax.broadcasted_iota(jnp.int32, (1, width), 1)
        bias_all = jnp.where(tq1 - key_pos >= 0, _dot(bias_src, expand), NEG_INF)
        logits, maxes = {}, {}
        for rc in range(n_chunks):
            qr = slice(rc * NSA_ROWS, (rc + 1) * NSA_ROWS)
            d_f = (tq1[qr] - key_pos).astype(F32)
            for h in range(NSA_HPG):
                s = st_all[chunk_rows(h, rc)] + (bias_all[qr] - slopes[h] * d_f)
                logits[h, rc] = s
                maxes[h, rc] = lane_blocks(s, jnp.maximum)
        m_old = m_scr[...]
        m_new = jnp.maximum(m_old, jnp.max(stacked(maxes), -1, keepdims=True))
        p_parts, _ = exp_pass(logits, m_new, want_sums=False)
        alpha = jnp.exp(m_old - m_new)
        m_scr[...] = m_new
        v_one = jnp.where(v_lane < NSA_HD, kv_ref[pl.ds(k0, width), 2 * NSA_HD:4 * NSA_HD], one_b)
        p_all = stacked({i: p_parts[i].astype(BF16) for i in order})
        acc_scr[...] = alpha * acc_scr[...] + _dot(p_all, v_one)

    attend(0, 0, SEL_HEAD, sel_bias)

    tile_of_blk = lax.broadcasted_iota(jnp.int32, (ns, LANE), 0) >> (SEL_TILE.bit_length() - 1 - SEL_BLK_LOG2)
    in_tile = jnp.where(tile_of_blk == lax.broadcasted_iota(jnp.int32, (ns, LANE), 1), 1.0, 0.0).astype(BF16)
    tile_cnt = _dot(jnp.broadcast_to(blk_any_rest, (SUBLANE, ns)).astype(BF16), in_tile)
    n_visible = (qs + Q_BLK + SEL_TILE - 1) // SEL_TILE
    n_active = jnp.int32(0)
    for kt in range(ns // tile_blks):
        tiles_ref[n_active] = kt
        n_active = n_active + ((tile_cnt[0, kt] > 0.0) & (kt < n_visible)).astype(jnp.int32)

    def sel_tile(j, carry):
        kt = tiles_ref[j]
        attend(pl.multiple_of(kt * SEL_TILE, SEL_TILE), kt * tile_blks, SEL_TILE, sel_bias_rest)
        return carry

    lax.fori_loop(0, n_active, sel_tile, 0)
    acc = acc_scr[...]
    o_slc = acc[:, 0:NSA_HD] * (1.0 / jnp.maximum(acc[:, NSA_HD:NSA_HD + 1], 1e-30))

    wlen = WIN + Q_BLK
    w0 = pl.multiple_of(jnp.maximum(qs - WIN, 0), Q_BLK)
    pos_w = w0 + lax.broadcasted_iota(jnp.int32, (1, wlen), 1)
    logit_w, _, m_w = masked_logits(_dot_nt(qst, kv_ref[pl.ds(w0, wlen), NSA_HD:2 * NSA_HD]), pos_w,
                                    lambda d: (d >= 0) & (d < WIN))
    e_w, l_w = exp_pass(logit_w, m_w)
    o_win = (_dot(stacked({i: e_w[i].astype(BF16) for i in order}), kv_ref[pl.ds(w0, wlen), 3 * NSA_HD:4 * NSA_HD])
             * (1.0 / jnp.maximum(l_w, 1e-30)))

    gates = _sigmoid(gl_ref[...])
    outs = []
    for h in range(NSA_HPG):
        r = slice(h * Q_BLK, (h + 1) * Q_BLK)
        outs.append(gates[:, 3 * h:3 * h + 1] * o_cmp[r] + gates[:, 3 * h + 1:3 * h + 2] * o_slc[r]
                    + gates[:, 3 * h + 2:3 * h + 3] * o_win[r])
    o_ref[...] = jnp.concatenate(outs, axis=1).astype(o_ref.dtype)


def _round_up(v, m):
    return (v + m - 1) // m * m


def _nsa_attention(main, gate_logits, cmp_tok, overlap, batch, seq):
    nq = seq // Q_BLK
    gw = NSA_HPG * NSA_HD
    hq = NSA_HPG * Q_BLK
    n_ranges = min(NSA_RANGES, nq)
    per = nq // n_ranges
    outs = []
    for r in range(n_ranges):
        q0 = r * per
        seen = (q0 + per) * Q_BLK
        nc = min(seq // CMP_STRIDE, _round_up(seen // CMP_STRIDE, LANE))
        ns = min(seq // SEL_BLK, _round_up(seen // SEL_BLK, LANE))
        out = pl.pallas_call(
            functools.partial(_nsa_body, nc=nc, ns=ns, qb0=q0),
            grid=(batch, NSA_GROUPS, per),
            in_specs=[pl.BlockSpec((Q_BLK, gw), lambda b, g, i, q0=q0: (b * nq + q0 + i, g)),
                      pl.BlockSpec((seq, gw), lambda b, g, i: (b, NSA_GROUPS + g)),
                      pl.BlockSpec((None, None, nc, NSA_HD), lambda b, g, i: (g, b, 0, 0)),
                      pl.BlockSpec((None, None, nc, NSA_HD), lambda b, g, i: (NSA_GROUPS + g, b, 0, 0)),
                      pl.BlockSpec((Q_BLK, LANE), lambda b, g, i, q0=q0: (b * nq + q0 + i, g)),
                      pl.BlockSpec((nc, ns), lambda b, g, i: (0, 0))],
            out_specs=pl.BlockSpec((Q_BLK, gw), lambda b, g, i: (b * per + i, g)),
            out_shape=jax.ShapeDtypeStruct((batch * per * Q_BLK, MIX_W), BF16),
            scratch_shapes=[pltpu.VMEM((hq, 1), F32), pltpu.VMEM((hq, 2 * NSA_HD), F32),
                            pltpu.SMEM((ns * SEL_BLK // SEL_TILE,), jnp.int32)],
            compiler_params=_params("parallel", "parallel", "arbitrary"),
            name="nsa_attention",
        )(main, main, cmp_tok, cmp_tok, gate_logits, overlap)
        outs.append(out.reshape(batch, per * Q_BLK, MIX_W))
    return jnp.concatenate(outs, axis=1).reshape(batch * seq, MIX_W)


def _unit_lower_inverses(mats):
    n = mats[0].shape[0]
    eye = (lax.broadcasted_iota(jnp.int32, (n, n), 0) == lax.broadcasted_iota(jnp.int32, (n, n), 1)).astype(F32)
    xs = [_split2(eye - a) for a in mats]
    ps = [_split2(a) for a in mats]
    power = 1
    while 2 * power < CHUNK:
        ps = [_split2(_dot_split(p, p)) for p in ps]
        xs = [_split2((x[0].astype(F32) + x[1].astype(F32)) + _dot_split(x, p)) for x, p in zip(xs, ps)]
        power *= 2
    return xs


def _gdn_body(raw_ref, ab_ref, conv_ref, alog_ref, dtb_ref, o_ref, s_ref, tail_ref, buf_ref, *, tb):
    @pl.when(pl.program_id(1) == 0)
    def _():
        s_ref[...] = jnp.zeros_like(s_ref)
        tail_ref[...] = jnp.zeros_like(tail_ref)

    raw = raw_ref[...]
    buf_ref[0:SUBLANE] = tail_ref[...]
    buf_ref[SUBLANE:SUBLANE + tb] = raw
    tail_ref[...] = raw[tb - SUBLANE:tb]
    x = jnp.zeros_like(raw)
    for k in range(CONV_K):
        x = x + buf_ref[pl.ds(SUBLANE - (CONV_K - 1) + k, tb)] * conv_ref[k:k + 1]
    x = x * _sigmoid(x)

    ab = ab_ref[...]
    zed = ab + dtb_ref[...]
    softplus = jnp.maximum(zed, 0.0) + jnp.log1p(jnp.exp(-jnp.abs(zed)))
    g_all = -jnp.exp(alog_ref[...]) * softplus
    beta_all = _sigmoid(ab)

    c2 = 2 * CHUNK
    ri = lax.broadcasted_iota(jnp.int32, (c2, c2), 0)
    ci = lax.broadcasted_iota(jnp.int32, (c2, c2), 1)
    tri2 = jnp.where((ri >= ci) & ((ri >> CHUNK_LOG2) == (ci >> CHUNK_LOG2)), 1.0, 0.0).astype(BF16)
    hc = GDN_HEADS * CHUNK
    rs = lax.broadcasted_iota(jnp.int32, (hc, hc), 0)
    cs = lax.broadcasted_iota(jnp.int32, (hc, hc), 1)
    same_head = (rs >> CHUNK_LOG2) == (cs >> CHUNK_LOG2)
    tri = same_head & (rs >= cs)
    strict = same_head & (rs > cs)
    qscale = GDN_HD ** -0.5

    def l2n(v):
        return v * lax.rsqrt(jnp.sum(v * v, -1, keepdims=True) + NORM_EPS)

    def stack(fn):
        return jnp.concatenate([fn(h) for h in range(GDN_HEADS)], axis=0)

    chunks = []
    for pair in range(tb // c2):
        pr = slice(pair * c2, (pair + 1) * c2)
        gc2 = _dot_exact_lhs(tri2, g_all[pr])
        gc2t = gc2.T
        for half in range(2):
            rows = slice(pair * c2 + half * CHUNK, pair * c2 + (half + 1) * CHUNK)
            hr = slice(half * CHUNK, (half + 1) * CHUNK)
            gcol = stack(lambda h: gc2[hr, h:h + 1])
            grow = jnp.concatenate([gc2t[h:h + 1, hr] for h in range(GDN_HEADS)], axis=1)
            g_last = stack(lambda h: jnp.broadcast_to(gc2[hr, h:h + 1][CHUNK - 1:CHUNK], (CHUNK, 1)))
            beta = stack(lambda h: beta_all[rows, GDN_HEADS + h:GDN_HEADS + h + 1])
            qs_ = stack(lambda h: l2n(x[rows, h * GDN_HD:(h + 1) * GDN_HD])) * qscale
            ks_ = stack(lambda h: l2n(x[rows, MIX_W + h * GDN_HD:MIX_W + (h + 1) * GDN_HD]))
            vs_ = stack(lambda h: x[rows, 2 * MIX_W + h * GDN_HD:2 * MIX_W + (h + 1) * GDN_HD])
            decay = jnp.where(tri, jnp.exp(jnp.where(tri, gcol - grow, 0.0)), 0.0)
            kb = ks_ * beta
            ks_b = ks_.astype(BF16)
            egc = jnp.exp(gcol)
            chunks.append(dict(
                rows=rows,
                a_mat=jnp.where(strict, _dot_nt(kb.astype(BF16), ks_b) * decay, 0.0),
                rhs=_split2(jnp.concatenate([vs_ * beta, kb * egc], axis=1)),
                qk=jnp.where(tri, _dot_nt(qs_.astype(BF16), ks_b) * decay, 0.0).astype(BF16),
                q_dec=(qs_ * egc).astype(BF16),
                k_tail=(ks_ * jnp.exp(g_last - gcol)).astype(BF16),
                a_last=jnp.exp(g_last)))

    t_invs = _unit_lower_inverses([c["a_mat"] for c in chunks])
    sols = [_dot_split(t, c["rhs"]) for t, c in zip(t_invs, chunks)]

    hrows = [slice(h * CHUNK, (h + 1) * CHUNK) for h in range(GDN_HEADS)]
    for c, sol in zip(chunks, sols):
        states = [s_ref[h] for h in range(GDN_HEADS)]
        states_b = [st.astype(BF16) for st in states]
        v_new = stack(lambda h: sol[hrows[h], :GDN_HD] - _dot(sol[hrows[h], GDN_HD:].astype(BF16), states_b[h]))
        v_new_b = v_new.astype(BF16)
        o = _dot(c["qk"], v_new_b) + stack(lambda h: _dot(c["q_dec"][hrows[h]], states_b[h]))
        for h in range(GDN_HEADS):
            s_ref[h] = (states[h] * c["a_last"][hrows[h]][0:1]
                        + _dot_tn(c["k_tail"][hrows[h]], v_new_b[hrows[h]]))
            o_ref[c["rows"], h * GDN_HD:(h + 1) * GDN_HD] = o[hrows[h]]


def _gdn(raw, ab, conv, alog, dtb, batch, seq, tb=512):
    nt = seq // tb
    return pl.pallas_call(
        functools.partial(_gdn_body, tb=tb),
        grid=(batch, nt),
        in_specs=[pl.BlockSpec((tb, 3 * MIX_W), lambda b, i: (b * nt + i, 0)),
                  pl.BlockSpec((tb, LANE), lambda b, i: (b * nt + i, 0)),
                  pl.BlockSpec((CONV_K, 3 * MIX_W), lambda b, i: (0, 0)),
                  pl.BlockSpec((1, LANE), lambda b, i: (0, 0)),
                  pl.BlockSpec((1, LANE), lambda b, i: (0, 0))],
        out_specs=pl.BlockSpec((tb, MIX_W), lambda b, i: (b * nt + i, 0)),
        out_shape=jax.ShapeDtypeStruct((batch * seq, MIX_W), F32),
        scratch_shapes=[pltpu.VMEM((GDN_HEADS, GDN_HD, GDN_HD), F32),
                        pltpu.VMEM((SUBLANE, 3 * MIX_W), F32),
                        pltpu.VMEM((SUBLANE + tb, 3 * MIX_W), F32)],
        compiler_params=_params("parallel", "arbitrary"),
        name="gated_delta_net",
    )(raw, ab, conv, alog, dtb)


def _gla_constants():
    c = CHUNK
    t = np.arange(c)[:, None]
    s = np.arange(c)[None, :]
    ops = [(s <= t), (s > t)]
    masks = []
    for m in GLA_LEVELS:
        r = (t // (2 * m)) * (2 * m) + m
        upper = (t % (2 * m)) >= m
        ops.append(np.where(upper, (s > r) & (s <= t), (s > t) & (s <= r)))
        i, j = t, s
        masks.append((i // (2 * m) == j // (2 * m)) & ((i % (2 * m)) >= m) & ((j % (2 * m)) < m))
    masks.append(t == s)
    heads = np.eye(GLA_HEADS)
    masks = np.stack([np.kron(heads, m) for m in masks])
    return np.concatenate(ops, 0).astype(np.float32), masks.astype(np.float32)


def _gla_body(x_ref, wlr_ref, blr_ref, ops_ref, lm_ref, o_ref, s_ref, *, tb):
    @pl.when(pl.program_id(1) == 0)
    def _():
        s_ref[...] = jnp.zeros_like(s_ref)

    hw = GLA_DV
    kw = GLA_HEADS * hw
    qscale = GLA_DK ** -0.5
    n_lvl = len(GLA_LEVELS)
    hrows = [slice(h * CHUNK, (h + 1) * CHUNK) for h in range(GLA_HEADS)]

    def stack(fn):
        return jnp.concatenate([fn(h) for h in range(GLA_HEADS)], axis=0)

    for c in range(tb // CHUNK):
        rows = slice(c * CHUNK, (c + 1) * CHUNK)
        lr = x_ref[rows, 3 * kw:3 * kw + LANE]
        z = _dot(lr, wlr_ref[...]) + blr_ref[...]
        log_a = (jnp.minimum(z, 0.0) - jnp.log1p(jnp.exp(-jnp.abs(z)))) / GLA_TAU
        e_all = jnp.exp(_dot_exact_lhs(ops_ref[...], log_a))

        def factor(i):
            return stack(lambda h: e_all[i * CHUNK:(i + 1) * CHUNK, h * hw:(h + 1) * hw])

        qs_ = stack(lambda h: x_ref[rows, h * hw:(h + 1) * hw]) * qscale
        ks_ = stack(lambda h: x_ref[rows, kw + h * hw:kw + (h + 1) * hw])
        vs_b = stack(lambda h: x_ref[rows, 2 * kw + h * hw:2 * kw + (h + 1) * hw]).astype(BF16)
        scores = lm_ref[n_lvl] * _dot_nt(qs_.astype(BF16), ks_.astype(BF16))
        for lvl in range(n_lvl):
            xl = factor(2 + lvl)
            scores = scores + lm_ref[lvl] * _dot_nt((qs_ * xl).astype(BF16), (ks_ * xl).astype(BF16))
        e_cum = factor(0)
        q_dec = (qs_ * e_cum).astype(BF16)
        k_tail = (ks_ * factor(1)).astype(BF16)
        states = [s_ref[h] for h in range(GLA_HEADS)]
        o = (_dot(scores.astype(BF16), vs_b)
             + stack(lambda h: _dot_nt(q_dec[hrows[h]], states[h].astype(BF16))))
        for h in range(GLA_HEADS):
            s_ref[h] = (states[h] * e_cum[hrows[h]][CHUNK - 1:CHUNK]
                        + _dot_tn(vs_b[hrows[h]], k_tail[hrows[h]]))
            o_ref[rows, h * hw:(h + 1) * hw] = o[hrows[h]]


def _gla(x, wlr, blr, ops, lmask, batch, seq, tb=256):
    nt = seq // tb
    width = x.shape[1]
    return pl.pallas_call(
        functools.partial(_gla_body, tb=tb),
        grid=(batch, nt),
        in_specs=[pl.BlockSpec((tb, width), lambda b, i: (b * nt + i, 0)),
                  pl.BlockSpec(wlr.shape, lambda b, i: (0, 0)),
                  pl.BlockSpec(blr.shape, lambda b, i: (0, 0)),
                  pl.BlockSpec(ops.shape, lambda b, i: (0, 0)),
                  pl.BlockSpec(lmask.shape, lambda b, i: (0, 0, 0))],
        out_specs=pl.BlockSpec((tb, MIX_W), lambda b, i: (b * nt + i, 0)),
        out_shape=jax.ShapeDtypeStruct((batch * seq, MIX_W), F32),
        scratch_shapes=[pltpu.VMEM((GLA_HEADS, GLA_DV, GLA_DV), F32)],
        compiler_params=_params("parallel", "arbitrary"),
        name="gla",
    )(x, wlr, blr, ops, lmask)


def _merge_body(x_ref, yn_ref, og_ref, ol_ref, wzr_ref, wmg_ref, ng_ref, nl_ref, wup_ref, wout_ref,
                lg_ref, lb_ref, o_ref, *, alpha):
    x = x_ref[...]
    xb = x.astype(BF16)
    zr = _dot(xb, wzr_ref[...])

    def head_rms(o, gain):
        parts = []
        for h in range(MIX_W // LANE):
            oh = o[:, h * LANE:(h + 1) * LANE]
            parts.append(oh * lax.rsqrt(jnp.mean(oh * oh, -1, keepdims=True) + NORM_EPS) * gain)
        return jnp.concatenate(parts, axis=1)

    z = zr[:, :MIX_W]
    r = zr[:, MIX_W:]
    y_gdn = head_rms(og_ref[...], ng_ref[...]) * (z * _sigmoid(z))
    y_gla = head_rms(ol_ref[...], nl_ref[...]) * (r * _sigmoid(r))
    ys = (yn_ref[...], y_gdn.astype(BF16), y_gla.astype(BF16))
    merged = jnp.zeros(x.shape, F32)
    for br in range(N_BRANCH):
        gate = _sigmoid(_dot(xb, wmg_ref[:, br * D_MODEL:(br + 1) * D_MODEL]))
        merged = merged + gate * _dot(ys[br], wup_ref[br])
    h = _dot(merged.astype(BF16), wout_ref[...])
    o_ref[...] = _layer_norm(alpha * x + h, lg_ref[...], lb_ref[...])


def _const_spec(shape):
    nd = len(shape)
    return pl.BlockSpec(shape, lambda i: (0,) * nd, pipeline_mode=pl.Buffered(1))


def _merge(x, y_nsa, o_gdn, o_gla, wzr, wmg, ng, nl, wup, wout, lg, lb, alpha, tm=512):
    m = x.shape[0]
    row = lambda w: pl.BlockSpec((tm, w), lambda i: (i, 0))
    return pl.pallas_call(
        functools.partial(_merge_body, alpha=alpha),
        grid=(m // tm,),
        in_specs=[row(D_MODEL), row(MIX_W), row(MIX_W), row(MIX_W),
                  _const_spec(wzr.shape), _const_spec(wmg.shape), _const_spec(ng.shape), _const_spec(nl.shape),
                  _const_spec(wup.shape), _const_spec(wout.shape), _const_spec(lg.shape), _const_spec(lb.shape)],
        out_specs=row(D_MODEL),
        out_shape=jax.ShapeDtypeStruct((m, D_MODEL), F32),
        compiler_params=_params("parallel"),
        name="mixer_merge",
    )(x, y_nsa, o_gdn, o_gla, wzr, wmg, ng, nl, wup, wout, lg, lb)


def _xattn_body(x_ref, kv_ref, wq_ref, wo_ref, lg_ref, lb_ref, o_ref, *, alpha):
    x = x_ref[...]
    q = _dot(x.astype(BF16), wq_ref[...]).astype(BF16)
    scale = XA_HD ** -0.5
    outs = []
    for h in range(XA_HEADS):
        hs = slice(h * XA_HD, (h + 1) * XA_HD)
        s = _dot_nt(q[:, hs], kv_ref[:, hs]) * scale
        m = jnp.max(s, -1, keepdims=True)
        e = jnp.exp(s - m)
        p = e / jnp.sum(e, -1, keepdims=True)
        outs.append(_dot(p.astype(BF16), kv_ref[:, D_MODEL + h * XA_HD:D_MODEL + (h + 1) * XA_HD]))
    o = jnp.concatenate(outs, axis=1)
    h_out = _dot(o.astype(BF16), wo_ref[...])
    o_ref[...] = _layer_norm(alpha * x + h_out, lg_ref[...], lb_ref[...])


def _xattn(x, kvm, wq, wo, lg, lb, alpha, batch, seq, tm=1024):
    nt = seq // tm
    mt = kvm.shape[0] // batch
    cs = lambda shape: pl.BlockSpec(shape, lambda b, i: (0,) * len(shape), pipeline_mode=pl.Buffered(1))
    return pl.pallas_call(
        functools.partial(_xattn_body, alpha=alpha),
        grid=(batch, nt),
        in_specs=[pl.BlockSpec((tm, D_MODEL), lambda b, i: (b * nt + i, 0)),
                  pl.BlockSpec((mt, 2 * D_MODEL), lambda b, i: (b, 0)),
                  cs(wq.shape), cs(wo.shape), cs(lg.shape), cs(lb.shape)],
        out_specs=pl.BlockSpec((tm, D_MODEL), lambda b, i: (b * nt + i, 0)),
        out_shape=jax.ShapeDtypeStruct((batch * seq, D_MODEL), F32),
        compiler_params=_params("parallel", "parallel"),
        name="mem_xattn",
    )(x, kvm, wq, wo, lg, lb)


def _mlp_body(x_ref, w1_ref, w2_ref, lg_ref, lb_ref, o_ref, *, alpha):
    x = x_ref[...]
    h = jnp.maximum(_dot(x.astype(BF16), w1_ref[...]), 0.0)
    h = (h * h).astype(BF16)
    o_ref[...] = _layer_norm(alpha * x + _dot(h, w2_ref[...]), lg_ref[...], lb_ref[...])


def _mlp(x, w1, w2, lg, lb, alpha, tm=1024):
    m = x.shape[0]
    return pl.pallas_call(
        functools.partial(_mlp_body, alpha=alpha),
        grid=(m // tm,),
        in_specs=[pl.BlockSpec((tm, D_MODEL), lambda i: (i, 0)),
                  _const_spec(w1.shape), _const_spec(w2.shape), _const_spec(lg.shape), _const_spec(lb.shape)],
        out_specs=pl.BlockSpec((tm, D_MODEL), lambda i: (i, 0)),
        out_shape=jax.ShapeDtypeStruct((m, D_MODEL), F32),
        compiler_params=_params("parallel"),
        name="sq_relu_mlp",
    )(x, w1, w2, lg, lb)


def _pad_cols(w, width):
    return jnp.pad(w, ((0, 0), (0, width - w.shape[1])))


def _nsa_weights(w_in):
    def kv(kind, g):
        o = O_NSA_KV + (kind * NSA_GROUPS + g) * NSA_HD
        return w_in[:, o:o + NSA_HD]
    cols = [w_in[:, O_NSA_Q:O_NSA_Q + NSA_HEADS * NSA_HD]]
    for g in range(NSA_GROUPS):
        cols += [kv(2, g), kv(4, g), kv(3, g), kv(5, g)]
    cols += [kv(0, 0), kv(0, 1), kv(1, 0), kv(1, 1)]
    main = jnp.concatenate(cols, axis=1).astype(BF16)
    gw = NSA_HPG * 3
    gates = [_pad_cols(w_in[:, O_NSA_G + g * gw:O_NSA_G + (g + 1) * gw], LANE) for g in range(NSA_GROUPS)]
    return main, jnp.concatenate(gates, axis=1).astype(BF16)


def _gla_weights(w_in, w_lr, b_lr):
    def pad_heads(w):
        lead = w.shape[0]
        w = w.reshape(lead, GLA_HEADS, GLA_DK)
        return jnp.pad(w, ((0, 0), (0, 0), (0, GLA_DV - GLA_DK))).reshape(lead, GLA_HEADS * GLA_DV)
    cols = [pad_heads(w_in[:, O_GLA_Q:O_GLA_Q + GLA_HEADS * GLA_DK]),
            pad_heads(w_in[:, O_GLA_K:O_GLA_K + GLA_HEADS * GLA_DK]),
            w_in[:, O_GLA_V:O_GLA_V + GLA_HEADS * GLA_DV],
            _pad_cols(w_in[:, O_GLA_LR:O_GLA_LR + GLA_RANK], LANE)]
    wlr = jnp.pad(pad_heads(w_lr), ((0, LANE - GLA_RANK), (0, 0)))
    return jnp.concatenate(cols, axis=1).astype(BF16), wlr, pad_heads(b_lr[None, :])


def _overlap_matrix(seq):
    nc = seq // CMP_STRIDE
    ns = seq // SEL_BLK
    c_start = np.arange(nc)[:, None] * CMP_STRIDE
    s_start = np.arange(ns)[None, :] * SEL_BLK
    return jnp.asarray((c_start < s_start + SEL_BLK) & (c_start + CMP_LEN > s_start), BF16)


def _nsa_branch(x, p, l, batch, seq):
    w_in = p["w_in"][l]
    w_main, w_gates = _nsa_weights(w_in)
    cmp_w = 2 * NSA_GROUPS * NSA_HD
    qkv, cmp_src, gate_logits = _mm(x, jnp.concatenate([w_main, w_gates], axis=1),
                                    [(2 * MIX_W, BF16), (cmp_w, BF16), (NSA_GROUPS * LANE, F32)])
    nc = seq // CMP_STRIDE
    half = CMP_STRIDE * NSA_HD
    cmp_in = cmp_src.reshape(batch * seq, 2 * NSA_GROUPS, NSA_HD).transpose(1, 0, 2)
    cmp_in = cmp_in.reshape(2 * NSA_GROUPS, batch * nc, half)
    w1 = p["cmp_w1"][l].reshape(2, 2, half, CMP_HID).astype(BF16)
    pe = jnp.broadcast_to(p["cmp_pe"][l].reshape(2, 2, 1, half), (2, 2, SUBLANE, half)).astype(BF16)
    cmp_tok = _compress(cmp_in, w1, p["cmp_b1"][l][:, None, :], pe, p["cmp_w2"][l].astype(BF16), batch)
    cmp_tok = cmp_tok.reshape(2 * NSA_GROUPS, batch, nc, NSA_HD)
    return _nsa_attention(qkv, gate_logits, cmp_tok, _overlap_matrix(seq), batch, seq)


def _gdn_branch(x, p, l, batch, seq):
    w_in = p["w_in"][l]
    w_gdn = jnp.concatenate([w_in[:, O_GDN_QKV:O_GDN_QKV + 3 * MIX_W],
                             _pad_cols(w_in[:, O_GDN_A:O_GDN_A + 2 * GDN_HEADS], LANE)], axis=1).astype(BF16)
    raw, ab = _mm(x, w_gdn, [(3 * MIX_W, F32), (LANE, F32)])
    alog = _pad_cols(p["gdn_a_log"][l][None, :], LANE)
    dtb = _pad_cols(p["gdn_dt_bias"][l][None, :], LANE)
    return _gdn(raw, ab, p["gdn_conv"][l], alog, dtb, batch, seq)


def _gla_branch(x, p, l, batch, seq):
    w_gla, wlr, blr = _gla_weights(p["w_in"][l], p["gla_w_lr"][l], p["gla_b_lr"][l])
    gla_in, = _mm(x, w_gla, [(w_gla.shape[1], F32)])
    ops, lmask = _gla_constants()
    return _gla(gla_in, wlr, blr, jnp.asarray(ops, BF16), jnp.asarray(lmask), batch, seq)


def _layer(x, kvm_src, p, l, batch, seq, alpha):
    w_in = p["w_in"][l]
    y_nsa = _nsa_branch(x, p, l, batch, seq)
    o_gdn = _gdn_branch(x, p, l, batch, seq)
    o_gla = _gla_branch(x, p, l, batch, seq)

    wzr = jnp.concatenate([w_in[:, O_GDN_Z:O_GDN_Z + MIX_W], w_in[:, O_GLA_R:O_GLA_R + MIX_W]], axis=1).astype(BF16)
    wmg = w_in[:, O_MERGE:O_MERGE + N_BRANCH * D_MODEL].astype(BF16)
    x = _merge(x, y_nsa, o_gdn, o_gla, wzr, wmg, p["gdn_norm"][l][None, :], p["gla_norm"][l][None, :],
               p["w_up"][l].astype(BF16), p["w_out"][l].astype(BF16),
               p["ln_g"][l, 0][None, :], p["ln_b"][l, 0][None, :], alpha)

    wkv = jnp.concatenate([p["xa_wk"][l], p["xa_wv"][l]], axis=1).astype(BF16)
    kvm, = _mm(kvm_src, wkv, [(2 * D_MODEL, BF16)])
    x = _xattn(x, kvm, p["xa_wq"][l].astype(BF16), p["xa_wo"][l].astype(BF16),
               p["ln_g"][l, 1][None, :], p["ln_b"][l, 1][None, :], alpha, batch, seq)

    return _mlp(x, p["mlp_w1"][l].astype(BF16), p["mlp_w2"][l].astype(BF16),
                p["ln_g"][l, 2][None, :], p["ln_b"][l, 2][None, :], alpha)


def kernel(x, mem, w_in, cmp_pe, cmp_w1, cmp_b1, cmp_w2, gdn_conv, gdn_a_log, gdn_dt_bias, gdn_norm, gla_w_lr,
           gla_b_lr, gla_norm, w_up, w_out, xa_wq, xa_wk, xa_wv, xa_wo, mlp_w1, mlp_w2, ln_g, ln_b):
    batch, seq, d = x.shape
    depth = w_in.shape[0]
    alpha = (2.0 * depth) ** 0.25
    p = dict(w_in=w_in, cmp_pe=cmp_pe, cmp_w1=cmp_w1, cmp_b1=cmp_b1, cmp_w2=cmp_w2, gdn_conv=gdn_conv,
             gdn_a_log=gdn_a_log, gdn_dt_bias=gdn_dt_bias, gdn_norm=gdn_norm, gla_w_lr=gla_w_lr,
             gla_b_lr=gla_b_lr, gla_norm=gla_norm, w_up=w_up, w_out=w_out, xa_wq=xa_wq, xa_wk=xa_wk,
             xa_wv=xa_wv, xa_wo=xa_wo, mlp_w1=mlp_w1, mlp_w2=mlp_w2, ln_g=ln_g, ln_b=ln_b)
    h = x.reshape(batch * seq, d)
    mem2 = mem.reshape(batch * mem.shape[1], d)
    for l in range(depth):
        h = _layer(h, mem2, p, l, batch, seq, alpha)
    return h.reshape(batch, seq, d)
```

```python
import functools

import numpy as np
import jax
import jax.numpy as jnp
from jax import lax
from jax.experimental import pallas as pl
from jax.experimental.pallas import tpu as pltpu

F32 = jnp.float32
BF16 = jnp.bfloat16

D_MODEL = 1024
MIX_W = D_MODEL // 2
N_BRANCH = 3
NSA_HEADS = 8
NSA_HD = MIX_W // NSA_HEADS
NSA_GROUPS = 2
NSA_HPG = NSA_HEADS // NSA_GROUPS
CMP_LEN = 32
CMP_STRIDE = 16
CMP_HID = D_MODEL // 4
SEL_BLK = 64
SEL_TOPK = 16
WIN = 512
Q_BLK = 256
GDN_HEADS = 4
GDN_HD = MIX_W // GDN_HEADS
CONV_K = 4
CHUNK = 64
GLA_HEADS = 4
GLA_DK = MIX_W // (2 * GLA_HEADS)
GLA_DV = MIX_W // GLA_HEADS
GLA_RANK = 16
GLA_TAU = 16.0
XA_HEADS = 4
XA_HD = D_MODEL // XA_HEADS
D_FF = 4 * D_MODEL
LN_EPS = 1e-5
NORM_EPS = 1e-6
NEG_INF = -1e30
FORCE_BONUS = 1e4

LANE = 128
SUBLANE = 8
VMEM_LIMIT_BYTES = 48 * 1024 * 1024
SEL_TILE = 512
SEL_HEAD = 128
NSA_ROWS = 16
NSA_RANGES = 4
REMOVED = -3e38
GLA_LEVELS = (32, 16, 8, 4, 2, 1)
Q_BLK_LOG2 = Q_BLK.bit_length() - 1
SEL_BLK_LOG2 = SEL_BLK.bit_length() - 1
CHUNK_LOG2 = CHUNK.bit_length() - 1

_IN_SIZES = (NSA_HEADS * NSA_HD, 6 * NSA_GROUPS * NSA_HD, NSA_HEADS * 3, 3 * MIX_W, GDN_HEADS, GDN_HEADS,
             MIX_W, GLA_HEADS * GLA_DK, GLA_HEADS * GLA_DK, GLA_HEADS * GLA_DV, GLA_RANK, MIX_W,
             N_BRANCH * D_MODEL)
_OFF = tuple(int(v) for v in np.cumsum((0,) + _IN_SIZES))
(O_NSA_Q, O_NSA_KV, O_NSA_G, O_GDN_QKV, O_GDN_A, O_GDN_B, O_GDN_Z, O_GLA_Q, O_GLA_K, O_GLA_V, O_GLA_LR,
 O_GLA_R, O_MERGE, _) = _OFF


def _params(*sem):
    return pltpu.CompilerParams(dimension_semantics=sem, vmem_limit_bytes=VMEM_LIMIT_BYTES)


def _dot(a, b):
    return jnp.dot(a, b, preferred_element_type=F32)


def _dot_nt(a, b):
    return lax.dot_general(a, b, (((1,), (1,)), ((), ())), preferred_element_type=F32)


def _dot_tn(a, b):
    return lax.dot_general(a, b, (((0,), (0,)), ((), ())), preferred_element_type=F32)


def _sigmoid(x):
    return 1.0 / (1.0 + jnp.exp(-x))


def _layer_norm(v, g, b):
    mu = jnp.mean(v, -1, keepdims=True)
    c = v - mu
    var = jnp.mean(c * c, -1, keepdims=True)
    return c * lax.rsqrt(var + LN_EPS) * g + b


def _split2(a):
    hi = a.astype(BF16)
    return hi, (a - hi.astype(F32)).astype(BF16)


def _dot_split(a, b):
    return _dot(a[0], b[0]) + (_dot(a[0], b[1]) + _dot(a[1], b[0]))


def _dot_exact_lhs(op, x):
    x1 = x.astype(BF16)
    r1 = x - x1.astype(F32)
    x2 = r1.astype(BF16)
    x3 = (r1 - x2.astype(F32)).astype(BF16)
    return _dot(op, x1) + (_dot(op, x2) + _dot(op, x3))


def _mm_body(x_ref, w_ref, *o_refs):
    res = _dot(x_ref[...].astype(BF16), w_ref[...])
    off = 0
    for o_ref in o_refs:
        o_ref[...] = res[:, off:off + o_ref.shape[1]].astype(o_ref.dtype)
        off += o_ref.shape[1]


def _mm(x, w, outs, tm=1024):
    m, k = x.shape
    n = w.shape[1]
    assert n == sum(wd for wd, _ in outs) and all(wd % LANE == 0 for wd, _ in outs)
    tm = min(tm, m)
    return pl.pallas_call(
        _mm_body,
        grid=(m // tm,),
        in_specs=[pl.BlockSpec((tm, k), lambda i: (i, 0)),
                  pl.BlockSpec((k, n), lambda i: (0, 0))],
        out_specs=[pl.BlockSpec((tm, wd), lambda i: (i, 0)) for wd, _ in outs],
        out_shape=[jax.ShapeDtypeStruct((m, wd), dt) for wd, dt in outs],
        compiler_params=_params("parallel"),
        name="proj_mm",
    )(x, w)


def _compress_body(ch_ref, w1_ref, b1_ref, pe_ref, w2_ref, o_ref):
    nc = ch_ref.shape[0]
    ch = ch_ref[...]
    top = _dot(ch, w1_ref[0])
    bot = _dot(ch, w1_ref[1])
    bot_next = pltpu.roll(bot, nc - 1, axis=0)
    pe_term = _dot(pe_ref[0], w1_ref[0]) + _dot(pe_ref[1], w1_ref[1])
    h = top + bot_next + pe_term[0:1] + b1_ref[...]
    h = jax.nn.gelu(h, approximate=True)
    o_ref[...] = _dot(h.astype(BF16), w2_ref[...]).astype(o_ref.dtype)


def _compress(ch, w1, b1, pe, w2, batch):
    nc = ch.shape[1] // batch
    half = CMP_STRIDE * NSA_HD
    return pl.pallas_call(
        _compress_body,
        grid=(2, NSA_GROUPS, batch),
        in_specs=[pl.BlockSpec((None, nc, half), lambda t, g, b: (t * NSA_GROUPS + g, b, 0)),
                  pl.BlockSpec((None, 2, half, CMP_HID), lambda t, g, b: (t, 0, 0, 0)),
                  pl.BlockSpec((None, 1, CMP_HID), lambda t, g, b: (t, 0, 0)),
                  pl.BlockSpec((None, 2, SUBLANE, half), lambda t, g, b: (t, 0, 0, 0)),
                  pl.BlockSpec((None, CMP_HID, NSA_HD), lambda t, g, b: (t, 0, 0))],
        out_specs=pl.BlockSpec((None, nc, NSA_HD), lambda t, g, b: (t * NSA_GROUPS + g, b, 0)),
        out_shape=jax.ShapeDtypeStruct((2 * NSA_GROUPS, batch * nc, NSA_HD), BF16),
        compiler_params=_params("parallel", "parallel", "parallel"),
        name="nsa_compress",
    )(ch, w1, b1, pe, w2)


def _nsa_body(q_ref, kv_ref, kc_ref, vc_ref, gl_ref, ov_ref, o_ref, m_scr, acc_scr, tiles_ref, *,
              nc, ns, qb0):
    g = pl.program_id(1)
    qb = qb0 + pl.program_id(2)
    qs = qb * Q_BLK
    scale = NSA_HD ** -0.5
    n_chunks = Q_BLK // NSA_ROWS

    q4 = q_ref[...] * jnp.asarray(scale, q_ref.dtype)
    qst = jnp.concatenate([q4[:, h * NSA_HD:(h + 1) * NSA_HD] for h in range(NSA_HPG)], axis=0)
    slopes = [jnp.where(g == 0, 2.0 ** -(h + 1), 2.0 ** -(NSA_HPG + h + 1)).astype(F32) for h in range(NSA_HPG)]
    tq1 = qs + lax.broadcasted_iota(jnp.int32, (Q_BLK, 1), 0)

    def chunk_rows(h, rc):
        return slice(h * Q_BLK + rc * NSA_ROWS, h * Q_BLK + (rc + 1) * NSA_ROWS)

    order = [(h, rc) for h in range(NSA_HPG) for rc in range(n_chunks)]

    def stacked(parts):
        return jnp.concatenate([parts[i] for i in order], axis=0)

    def lane_blocks(x, op):
        out = x[:, 0:LANE]
        for j in range(1, x.shape[1] // LANE):
            out = op(out, x[:, j * LANE:(j + 1) * LANE])
        return out

    def lane_tiled(x, width):
        return jnp.concatenate([x] * (width // LANE), axis=1)

    def exp_pass(logit_parts, m_all, masks=None, want_sums=True):
        width = logit_parts[order[0]].shape[1]
        m_lanes = jnp.broadcast_to(m_all, (m_all.shape[0], LANE))
        e_parts, sums = {}, {}
        for h, rc in order:
            e = jnp.exp(logit_parts[h, rc] - lane_tiled(m_lanes[chunk_rows(h, rc)], width))
            if masks is not None:
                e = jnp.where(masks[rc], e, 0.0)
            e_parts[h, rc] = e
            if want_sums:
                sums[h, rc] = lane_blocks(e, jnp.add)
        return e_parts, (jnp.sum(stacked(sums), -1, keepdims=True) if want_sums else None)

    def masked_logits(s_all, key_pos, is_valid):
        parts, maxes, masks = {}, {}, {}
        for rc in range(n_chunks):
            dist = tq1[rc * NSA_ROWS:(rc + 1) * NSA_ROWS] - key_pos
            masks[rc] = is_valid(dist)
            d_f = dist.astype(F32)
            for h in range(NSA_HPG):
                s = jnp.where(masks[rc], s_all[chunk_rows(h, rc)] - slopes[h] * d_f, NEG_INF)
                parts[h, rc] = s
                maxes[h, rc] = lane_blocks(s, jnp.maximum)
        return parts, masks, jnp.max(stacked(maxes), -1, keepdims=True)

    c_end = lax.broadcasted_iota(jnp.int32, (1, nc), 1) * CMP_STRIDE + (CMP_LEN - 1)
    logit_c, mask_c, m_c = masked_logits(_dot_nt(qst, kc_ref[...]), c_end, lambda d: d >= 0)
    e_c, l_c = exp_pass(logit_c, m_c, mask_c)
    r_c = 1.0 / jnp.maximum(l_c, 1e-30)
    o_cmp = _dot(stacked({i: e_c[i].astype(BF16) for i in order}), vc_ref[...]) * r_c

    r_lanes = jnp.broadcast_to(r_c, (r_c.shape[0], LANE))
    p_hi, p_lo = [], []
    for rc in range(n_chunks):
        p_sum = None
        for h in range(NSA_HPG):
            p = e_c[h, rc] * lane_tiled(r_lanes[chunk_rows(h, rc)], nc)
            p_sum = p if p_sum is None else p_sum + p
        hi = p_sum.astype(BF16)
        p_hi.append(hi)
        p_lo.append((p_sum - hi.astype(F32)).astype(BF16))
    p_hi = jnp.concatenate(p_hi, axis=0)
    p_lo = jnp.concatenate(p_lo, axis=0)

    imp_t = (_dot(p_hi, ov_ref[...]) + _dot(p_lo, ov_ref[...])).T
    blk = lax.broadcasted_iota(jnp.int32, (ns, Q_BLK), 0)
    tq_row = qs + lax.broadcasted_iota(jnp.int32, (1, Q_BLK), 1)
    cur = tq_row >> SEL_BLK_LOG2
    forced = (blk == 0) | (blk == cur) | (blk == cur - 1)
    valid = blk * SEL_BLK <= tq_row
    score = jnp.where(forced, REMOVED, jnp.where(valid, imp_t, NEG_INF))
    assert FORCE_BONUS > NSA_HPG
    n_top = min(SEL_TOPK, ns)
    n_pick = (n_top - 1) - jnp.where(cur >= 1, 1, 0) - jnp.where(cur >= 2, 1, 0)
    blk_l = blk[:, 0:LANE]

    def pick_lanes(j):
        lanes = slice(j * LANE, (j + 1) * LANE)
        n_pick_l = n_pick[:, lanes]

        def pick(it, carry):
            sc, sel = carry
            m = jnp.max(sc, 0, keepdims=True)
            idx = jnp.min(jnp.where(sc == m, blk_l, ns), 0, keepdims=True)
            hit = blk_l == jnp.where(it < n_pick_l, idx, -1)
            return jnp.where(hit, REMOVED, sc), jnp.where(hit, 1.0, sel)

        trips = jnp.where(qs + j * LANE >= 2 * SEL_BLK, n_top - 3, n_top - 1)
        return lax.fori_loop(0, trips, pick, (score[:, lanes], jnp.where(forced[:, lanes], 1.0, 0.0)))[1]

    sel_t = jnp.concatenate([pick_lanes(j) for j in range(Q_BLK // LANE)], axis=1)
    sel = sel_t.T
    sel_bias = ((sel - 1.0) * -NEG_INF).astype(BF16)
    blk_any = jnp.max(sel, 0, keepdims=True)

    tile_blks = SEL_TILE // SEL_BLK
    head_blks = SEL_HEAD // SEL_BLK
    m_scr[...] = jnp.full(m_scr.shape, NEG_INF, F32)
    acc_scr[...] = jnp.zeros(acc_scr.shape, F32)
    v_lane = lax.broadcasted_iota(jnp.int32, (1, 2 * NSA_HD), 1)
    one_b = jnp.ones((), kv_ref.dtype)
    in_head = lax.broadcasted_iota(jnp.int32, (Q_BLK, ns), 1) < head_blks
    sel_bias_rest = jnp.where(in_head, NEG_INF, (sel - 1.0) * -NEG_INF).astype(BF16)
    blk_any_rest = jnp.where(in_head[0:1], 0.0, blk_any)

    def attend(k0, blk0, width, bias_src):
        st_all = _dot_nt(qst, kv_ref[pl.ds(k0, width), 0:NSA_HD])
        blk_row = lax.broadcasted_iota(jnp.int32, (ns, width), 0)
        blk_of_key = lax.broadcasted_iota(jnp.int32, (ns, width), 1) >> SEL_BLK_LOG2
        expand = jnp.where(blk_row == blk_of_key + blk0, 1.0, 0.0).astype(BF16)
        key_pos = k0 + lax.broadcasted_iota(jnp.int32, (1, width), 1)
        bias_all = jnp.where(tq1 - key_pos >= 0, _dot(bias_src, expand), NEG_INF)
        logits, maxes = {}, {}
        for rc in range(n_chunks):
            qr = slice(rc * NSA_ROWS, (rc + 1) * NSA_ROWS)
            d_f = (tq1[qr] - key_pos).astype(F32)
            for h in range(NSA_HPG):
                s = st_all[chunk_rows(h, rc)] + (bias_all[qr] - slopes[h] * d_f)
                logits[h, rc] = s
                maxes[h, rc] = lane_blocks(s, jnp.maximum)
        m_old = m_scr[...]
        m_new = jnp.maximum(m_old, jnp.max(stacked(maxes), -1, keepdims=True))
        p_parts, _ = exp_pass(logits, m_new, want_sums=False)
        alpha = jnp.exp(m_old - m_new)
        m_scr[...] = m_new
        v_one = jnp.where(v_lane < NSA_HD, kv_ref[pl.ds(k0, width), 2 * NSA_HD:4 * NSA_HD], one_b)
        p_all = stacked({i: p_parts[i].astype(BF16) for i in order})
        acc_scr[...] = alpha * acc_scr[...] + _dot(p_all, v_one)

    attend(0, 0, SEL_HEAD, sel_bias)

    tile_of_blk = lax.broadcasted_iota(jnp.int32, (ns, LANE), 0) >> (SEL_TILE.bit_length() - 1 - SEL_BLK_LOG2)
    in_tile = jnp.where(tile_of_blk == lax.broadcasted_iota(jnp.int32, (ns, LANE), 1), 1.0, 0.0).astype(BF16)
    tile_cnt = _dot(jnp.broadcast_to(blk_any_rest, (SUBLANE, ns)).astype(BF16), in_tile)
    n_visible = (qs + Q_BLK + SEL_TILE - 1) // SEL_TILE
    n_active = jnp.int32(0)
    for kt in range(ns // tile_blks):
        tiles_ref[n_active] = kt
        n_active = n_active + ((tile_cnt[0, kt] > 0.0) & (kt < n_visible)).astype(jnp.int32)

    def sel_tile(j, carry):
        kt = tiles_ref[j]
        attend(pl.multiple_of(kt * SEL_TILE, SEL_TILE), kt * tile_blks, SEL_TILE, sel_bias_rest)
        return carry

    lax.fori_loop(0, n_active, sel_tile, 0)
    acc = acc_scr[...]
    o_slc = acc[:, 0:NSA_HD] * (1.0 / jnp.maximum(acc[:, NSA_HD:NSA_HD + 1], 1e-30))

    wlen = WIN + Q_BLK
    w0 = pl.multiple_of(jnp.maximum(qs - WIN, 0), Q_BLK)
    pos_w = w0 + lax.broadcasted_iota(jnp.int32, (1, wlen), 1)
    logit_w, _, m_w = masked_logits(_dot_nt(qst, kv_ref[pl.ds(w0, wlen), NSA_HD:2 * NSA_HD]), pos_w,
                                    lambda d: (d >= 0) & (d < WIN))
    e_w, l_w = exp_pass(logit_w, m_w)
    o_win = (_dot(stacked({i: e_w[i].astype(BF16) for i in order}), kv_ref[pl.ds(w0, wlen), 3 * NSA_HD:4 * NSA_HD])
             * (1.0 / jnp.maximum(l_w, 1e-30)))

    gates = _sigmoid(gl_ref[...])
    outs = []
    for h in range(NSA_HPG):
        r = slice(h * Q_BLK, (h + 1) * Q_BLK)
        outs.append(gates[:, 3 * h:3 * h + 1] * o_cmp[r] + gates[:, 3 * h + 1:3 * h + 2] * o_slc[r]
                    + gates[:, 3 * h + 2:3 * h + 3] * o_win[r])
    o_ref[...] = jnp.concatenate(outs, axis=1).astype(o_ref.dtype)


def _round_up(v, m):
    return (v + m - 1) // m * m


def _nsa_attention(main, gate_logits, cmp_tok, overlap, batch, seq):
    nq = seq // Q_BLK
    gw = NSA_HPG * NSA_HD
    hq = NSA_HPG * Q_BLK
    n_ranges = min(NSA_RANGES, nq)
    per = nq // n_ranges
    outs = []
    for r in range(n_ranges):
        q0 = r * per
        seen = (q0 + per) * Q_BLK
        nc = min(seq // CMP_STRIDE, _round_up(seen // CMP_STRIDE, LANE))
        ns = min(seq // SEL_BLK, _round_up(seen // SEL_BLK, LANE))
        out = pl.pallas_call(
            functools.partial(_nsa_body, nc=nc, ns=ns, qb0=q0),
            grid=(batch, NSA_GROUPS, per),
            in_specs=[pl.BlockSpec((Q_BLK, gw), lambda b, g, i, q0=q0: (b * nq + q0 + i, g)),
                      pl.BlockSpec((seq, gw), lambda b, g, i: (b, NSA_GROUPS + g)),
                      pl.BlockSpec((None, None, nc, NSA_HD), lambda b, g, i: (g, b, 0, 0)),
                      pl.BlockSpec((None, None, nc, NSA_HD), lambda b, g, i: (NSA_GROUPS + g, b, 0, 0)),
                      pl.BlockSpec((Q_BLK, LANE), lambda b, g, i, q0=q0: (b * nq + q0 + i, g)),
                      pl.BlockSpec((nc, ns), lambda b, g, i: (0, 0))],
            out_specs=pl.BlockSpec((Q_BLK, gw), lambda b, g, i: (b * per + i, g)),
            out_shape=jax.ShapeDtypeStruct((batch * per * Q_BLK, MIX_W), BF16),
            scratch_shapes=[pltpu.VMEM((hq, 1), F32), pltpu.VMEM((hq, 2 * NSA_HD), F32),
                            pltpu.SMEM((ns * SEL_BLK // SEL_TILE,), jnp.int32)],
            compiler_params=_params("parallel", "parallel", "arbitrary"),
            name="nsa_attention",
        )(main, main, cmp_tok, cmp_tok, gate_logits, overlap)
        outs.append(out.reshape(batch, per * Q_BLK, MIX_W))
    return jnp.concatenate(outs, axis=1).reshape(batch * seq, MIX_W)


def _unit_lower_inverses(mats):
    n = mats[0].shape[0]
    eye = (lax.broadcasted_iota(jnp.int32, (n, n), 0) == lax.broadcasted_iota(jnp.int32, (n, n), 1)).astype(F32)
    xs = [_split2(eye - a) for a in mats]
    ps = [_split2(a) for a in mats]
    power = 1
    while 2 * power < CHUNK:
        ps = [_split2(_dot_split(p, p)) for p in ps]
        xs = [_split2((x[0].astype(F32) + x[1].astype(F32)) + _dot_split(x, p)) for x, p in zip(xs, ps)]
        power *= 2
    return xs


def _gdn_body(x_ref, w_ref, conv_ref, alog_ref, dtb_ref, o_ref, s_ref, tail_ref, buf_ref, *, tb):
    @pl.when(pl.program_id(1) == 0)
    def _():
        s_ref[...] = jnp.zeros_like(s_ref)
        tail_ref[...] = jnp.zeros_like(tail_ref)

    x_in = x_ref[...].astype(BF16)
    raw = _dot(x_in, w_ref[:, 0:3 * MIX_W])
    ab = _dot(x_in, w_ref[:, 3 * MIX_W:])
    buf_ref[0:SUBLANE] = tail_ref[...]
    buf_ref[SUBLANE:SUBLANE + tb] = raw
    tail_ref[...] = raw[tb - SUBLANE:tb]
    x = jnp.zeros_like(raw)
    for k in range(CONV_K):
        x = x + buf_ref[pl.ds(SUBLANE - (CONV_K - 1) + k, tb)] * conv_ref[k:k + 1]
    x = x * _sigmoid(x)

    zed = ab + dtb_ref[...]
    softplus = jnp.maximum(zed, 0.0) + jnp.log1p(jnp.exp(-jnp.abs(zed)))
    g_all = -jnp.exp(alog_ref[...]) * softplus
    beta_all = _sigmoid(ab)

    c2 = 2 * CHUNK
    ri = lax.broadcasted_iota(jnp.int32, (c2, c2), 0)
    ci = lax.broadcasted_iota(jnp.int32, (c2, c2), 1)
    tri2 = jnp.where((ri >= ci) & ((ri >> CHUNK_LOG2) == (ci >> CHUNK_LOG2)), 1.0, 0.0).astype(BF16)
    hc = GDN_HEADS * CHUNK
    rs = lax.broadcasted_iota(jnp.int32, (hc, hc), 0)
    cs = lax.broadcasted_iota(jnp.int32, (hc, hc), 1)
    same_head = (rs >> CHUNK_LOG2) == (cs >> CHUNK_LOG2)
    tri = same_head & (rs >= cs)
    strict = same_head & (rs > cs)
    qscale = GDN_HD ** -0.5

    def l2n(v):
        return v * lax.rsqrt(jnp.sum(v * v, -1, keepdims=True) + NORM_EPS)

    def stack(fn):
        return jnp.concatenate([fn(h) for h in range(GDN_HEADS)], axis=0)

    chunks = []
    for pair in range(tb // c2):
        pr = slice(pair * c2, (pair + 1) * c2)
        gc2 = _dot_exact_lhs(tri2, g_all[pr])
        gc2t = gc2.T
        for half in range(2):
            rows = slice(pair * c2 + half * CHUNK, pair * c2 + (half + 1) * CHUNK)
            hr = slice(half * CHUNK, (half + 1) * CHUNK)
            gcol = stack(lambda h: gc2[hr, h:h + 1])
            grow = jnp.concatenate([gc2t[h:h + 1, hr] for h in range(GDN_HEADS)], axis=1)
            g_last = stack(lambda h: jnp.broadcast_to(gc2[hr, h:h + 1][CHUNK - 1:CHUNK], (CHUNK, 1)))
            beta = stack(lambda h: beta_all[rows, GDN_HEADS + h:GDN_HEADS + h + 1])
            qs_ = stack(lambda h: l2n(x[rows, h * GDN_HD:(h + 1) * GDN_HD])) * qscale
            ks_ = stack(lambda h: l2n(x[rows, MIX_W + h * GDN_HD:MIX_W + (h + 1) * GDN_HD]))
            vs_ = stack(lambda h: x[rows, 2 * MIX_W + h * GDN_HD:2 * MIX_W + (h + 1) * GDN_HD])
            decay = jnp.where(tri, jnp.exp(jnp.where(tri, gcol - grow, 0.0)), 0.0)
            kb = ks_ * beta
            ks_b = ks_.astype(BF16)
            egc = jnp.exp(gcol)
            chunks.append(dict(
                rows=rows,
                a_mat=jnp.where(strict, _dot_nt(kb.astype(BF16), ks_b) * decay, 0.0),
                rhs=_split2(jnp.concatenate([vs_ * beta, kb * egc], axis=1)),
                qk=jnp.where(tri, _dot_nt(qs_.astype(BF16), ks_b) * decay, 0.0).astype(BF16),
                q_dec=(qs_ * egc).astype(BF16),
                k_tail=(ks_ * jnp.exp(g_last - gcol)).astype(BF16),
                a_last=jnp.exp(g_last)))

    t_invs = _unit_lower_inverses([c["a_mat"] for c in chunks])
    sols = [_dot_split(t, c["rhs"]) for t, c in zip(t_invs, chunks)]

    hrows = [slice(h * CHUNK, (h + 1) * CHUNK) for h in range(GDN_HEADS)]
    for c, sol in zip(chunks, sols):
        states = [s_ref[h] for h in range(GDN_HEADS)]
        states_b = [st.astype(BF16) for st in states]
        v_new = stack(lambda h: sol[hrows[h], :GDN_HD] - _dot(sol[hrows[h], GDN_HD:].astype(BF16), states_b[h]))
        v_new_b = v_new.astype(BF16)
        o = _dot(c["qk"], v_new_b) + stack(lambda h: _dot(c["q_dec"][hrows[h]], states_b[h]))
        for h in range(GDN_HEADS):
            s_ref[h] = (states[h] * c["a_last"][hrows[h]][0:1]
                        + _dot_tn(c["k_tail"][hrows[h]], v_new_b[hrows[h]]))
            o_ref[c["rows"], h * GDN_HD:(h + 1) * GDN_HD] = o[hrows[h]]


def _gdn(x, w, conv, alog, dtb, batch, seq, tb=512):
    nt = seq // tb
    return pl.pallas_call(
        functools.partial(_gdn_body, tb=tb),
        grid=(batch, nt),
        in_specs=[pl.BlockSpec((tb, D_MODEL), lambda b, i: (b * nt + i, 0)),
                  pl.BlockSpec(w.shape, lambda b, i: (0, 0), pipeline_mode=pl.Buffered(1)),
                  pl.BlockSpec((CONV_K, 3 * MIX_W), lambda b, i: (0, 0)),
                  pl.BlockSpec((1, LANE), lambda b, i: (0, 0)),
                  pl.BlockSpec((1, LANE), lambda b, i: (0, 0))],
        out_specs=pl.BlockSpec((tb, MIX_W), lambda b, i: (b * nt + i, 0)),
        out_shape=jax.ShapeDtypeStruct((batch * seq, MIX_W), F32),
        scratch_shapes=[pltpu.VMEM((GDN_HEADS, GDN_HD, GDN_HD), F32),
                        pltpu.VMEM((SUBLANE, 3 * MIX_W), F32),
                        pltpu.VMEM((SUBLANE + tb, 3 * MIX_W), F32)],
        compiler_params=_params("parallel", "arbitrary"),
        name="gated_delta_net",
    )(x, w, conv, alog, dtb)


def _gla_constants():
    c = CHUNK
    t = np.arange(c)[:, None]
    s = np.arange(c)[None, :]
    ops = [(s <= t), (s > t)]
    masks = []
    for m in GLA_LEVELS:
        r = (t // (2 * m)) * (2 * m) + m
        upper = (t % (2 * m)) >= m
        ops.append(np.where(upper, (s > r) & (s <= t), (s > t) & (s <= r)))
        i, j = t, s
        masks.append((i // (2 * m) == j // (2 * m)) & ((i % (2 * m)) >= m) & ((j % (2 * m)) < m))
    masks.append(t == s)
    heads = np.eye(GLA_HEADS)
    masks = np.stack([np.kron(heads, m) for m in masks])
    return np.concatenate(ops, 0).astype(np.float32), masks.astype(np.float32)


def _gla_body(xin_ref, w_ref, wlr_ref, blr_ref, ops_ref, lm_ref, o_ref, s_ref, *, tb):
    @pl.when(pl.program_id(1) == 0)
    def _():
        s_ref[...] = jnp.zeros_like(s_ref)

    x_ref = _dot(xin_ref[...].astype(BF16), w_ref[...])

    hw = GLA_DV
    kw = GLA_HEADS * hw
    qscale = GLA_DK ** -0.5
    n_lvl = len(GLA_LEVELS)
    hrows = [slice(h * CHUNK, (h + 1) * CHUNK) for h in range(GLA_HEADS)]

    def stack(fn):
        return jnp.concatenate([fn(h) for h in range(GLA_HEADS)], axis=0)

    for c in range(tb // CHUNK):
        rows = slice(c * CHUNK, (c + 1) * CHUNK)
        lr = x_ref[rows, 3 * kw:3 * kw + LANE]
        z = _dot(lr, wlr_ref[...]) + blr_ref[...]
        log_a = (jnp.minimum(z, 0.0) - jnp.log1p(jnp.exp(-jnp.abs(z)))) / GLA_TAU
        e_all = jnp.exp(_dot_exact_lhs(ops_ref[...], log_a))

        def factor(i):
            return stack(lambda h: e_all[i * CHUNK:(i + 1) * CHUNK, h * hw:(h + 1) * hw])

        qs_ = stack(lambda h: x_ref[rows, h * hw:(h + 1) * hw]) * qscale
        ks_ = stack(lambda h: x_ref[rows, kw + h * hw:kw + (h + 1) * hw])
        vs_b = stack(lambda h: x_ref[rows, 2 * kw + h * hw:2 * kw + (h + 1) * hw]).astype(BF16)
        scores = lm_ref[n_lvl] * _dot_nt(qs_.astype(BF16), ks_.astype(BF16))
        for lvl in range(n_lvl):
            xl = factor(2 + lvl)
            scores = scores + lm_ref[lvl] * _dot_nt((qs_ * xl).astype(BF16), (ks_ * xl).astype(BF16))
        e_cum = factor(0)
        q_dec = (qs_ * e_cum).astype(BF16)
        k_tail = (ks_ * factor(1)).astype(BF16)
        states = [s_ref[h] for h in range(GLA_HEADS)]
        o = (_dot(scores.astype(BF16), vs_b)
             + stack(lambda h: _dot_nt(q_dec[hrows[h]], states[h].astype(BF16))))
        for h in range(GLA_HEADS):
            s_ref[h] = (states[h] * e_cum[hrows[h]][CHUNK - 1:CHUNK]
                        + _dot_tn(vs_b[hrows[h]], k_tail[hrows[h]]))
            o_ref[rows, h * hw:(h + 1) * hw] = o[hrows[h]]


def _gla(x, w, wlr, blr, ops, lmask, batch, seq, tb=256):
    nt = seq // tb
    return pl.pallas_call(
        functools.partial(_gla_body, tb=tb),
        grid=(batch, nt),
        in_specs=[pl.BlockSpec((tb, D_MODEL), lambda b, i: (b * nt + i, 0)),
                  pl.BlockSpec(w.shape, lambda b, i: (0, 0), pipeline_mode=pl.Buffered(1)),
                  pl.BlockSpec(wlr.shape, lambda b, i: (0, 0)),
                  pl.BlockSpec(blr.shape, lambda b, i: (0, 0)),
                  pl.BlockSpec(ops.shape, lambda b, i: (0, 0)),
                  pl.BlockSpec(lmask.shape, lambda b, i: (0, 0, 0))],
        out_specs=pl.BlockSpec((tb, MIX_W), lambda b, i: (b * nt + i, 0)),
        out_shape=jax.ShapeDtypeStruct((batch * seq, MIX_W), F32),
        scratch_shapes=[pltpu.VMEM((GLA_HEADS, GLA_DV, GLA_DV), F32)],
        compiler_params=_params("parallel", "arbitrary"),
        name="gla",
    )(x, w, wlr, blr, ops, lmask)


def _merge_body(x_ref, yn_ref, og_ref, ol_ref, wzr_ref, wmg_ref, ng_ref, nl_ref, wup_ref, wout_ref,
                lg_ref, lb_ref, o_ref, *, alpha):
    x = x_ref[...]
    xb = x.astype(BF16)
    zr = _dot(xb, wzr_ref[...])

    def head_rms(o, gain):
        parts = []
        for h in range(MIX_W // LANE):
            oh = o[:, h * LANE:(h + 1) * LANE]
            parts.append(oh * lax.rsqrt(jnp.mean(oh * oh, -1, keepdims=True) + NORM_EPS) * gain)
        return jnp.concatenate(parts, axis=1)

    z = zr[:, :MIX_W]
    r = zr[:, MIX_W:]
    y_gdn = head_rms(og_ref[...], ng_ref[...]) * (z * _sigmoid(z))
    y_gla = head_rms(ol_ref[...], nl_ref[...]) * (r * _sigmoid(r))
    ys = (yn_ref[...], y_gdn.astype(BF16), y_gla.astype(BF16))
    merged = jnp.zeros(x.shape, F32)
    for br in range(N_BRANCH):
        gate = _sigmoid(_dot(xb, wmg_ref[:, br * D_MODEL:(br + 1) * D_MODEL]))
        merged = merged + gate * _dot(ys[br], wup_ref[br])
    h = _dot(merged.astype(BF16), wout_ref[...])
    o_ref[...] = _layer_norm(alpha * x + h, lg_ref[...], lb_ref[...])


def _const_spec(shape):
    nd = len(shape)
    return pl.BlockSpec(shape, lambda i: (0,) * nd, pipeline_mode=pl.Buffered(1))


def _merge(x, y_nsa, o_gdn, o_gla, wzr, wmg, ng, nl, wup, wout, lg, lb, alpha, tm=512):
    m = x.shape[0]
    row = lambda w: pl.BlockSpec((tm, w), lambda i: (i, 0))
    return pl.pallas_call(
        functools.partial(_merge_body, alpha=alpha),
        grid=(m // tm,),
        in_specs=[row(D_MODEL), row(MIX_W), row(MIX_W), row(MIX_W),
                  _const_spec(wzr.shape), _const_spec(wmg.shape), _const_spec(ng.shape), _const_spec(nl.shape),
                  _const_spec(wup.shape), _const_spec(wout.shape), _const_spec(lg.shape), _const_spec(lb.shape)],
        out_specs=row(D_MODEL),
        out_shape=jax.ShapeDtypeStruct((m, D_MODEL), F32),
        compiler_params=_params("parallel"),
        name="mixer_merge",
    )(x, y_nsa, o_gdn, o_gla, wzr, wmg, ng, nl, wup, wout, lg, lb)


def _xattn_body(x_ref, kv_ref, wq_ref, wo_ref, lg_ref, lb_ref, o_ref, *, alpha):
    x = x_ref[...]
    q = _dot(x.astype(BF16), wq_ref[...]).astype(BF16)
    scale = XA_HD ** -0.5
    outs = []
    for h in range(XA_HEADS):
        hs = slice(h * XA_HD, (h + 1) * XA_HD)
        s = _dot_nt(q[:, hs], kv_ref[:, hs]) * scale
        m = jnp.max(s, -1, keepdims=True)
        e = jnp.exp(s - m)
        p = e / jnp.sum(e, -1, keepdims=True)
        outs.append(_dot(p.astype(BF16), kv_ref[:, D_MODEL + h * XA_HD:D_MODEL + (h + 1) * XA_HD]))
    o = jnp.concatenate(outs, axis=1)
    h_out = _dot(o.astype(BF16), wo_ref[...])
    o_ref[...] = _layer_norm(alpha * x + h_out, lg_ref[...], lb_ref[...])


def _xattn(x, kvm, wq, wo, lg, lb, alpha, batch, seq, tm=1024):
    nt = seq // tm
    mt = kvm.shape[0] // batch
    cs = lambda shape: pl.BlockSpec(shape, lambda b, i: (0,) * len(shape), pipeline_mode=pl.Buffered(1))
    return pl.pallas_call(
        functools.partial(_xattn_body, alpha=alpha),
        grid=(batch, nt),
        in_specs=[pl.BlockSpec((tm, D_MODEL), lambda b, i: (b * nt + i, 0)),
                  pl.BlockSpec((mt, 2 * D_MODEL), lambda b, i: (b, 0)),
                  cs(wq.shape), cs(wo.shape), cs(lg.shape), cs(lb.shape)],
        out_specs=pl.BlockSpec((tm, D_MODEL), lambda b, i: (b * nt + i, 0)),
        out_shape=jax.ShapeDtypeStruct((batch * seq, D_MODEL), F32),
        compiler_params=_params("parallel", "parallel"),
        name="mem_xattn",
    )(x, kvm, wq, wo, lg, lb)


def _mlp_body(x_ref, w1_ref, w2_ref, lg_ref, lb_ref, o_ref, *, alpha):
    x = x_ref[...]
    h = jnp.maximum(_dot(x.astype(BF16), w1_ref[...]), 0.0)
    h = (h * h).astype(BF16)
    o_ref[...] = _layer_norm(alpha * x + _dot(h, w2_ref[...]), lg_ref[...], lb_ref[...])


def _mlp(x, w1, w2, lg, lb, alpha, tm=1024):
    m = x.shape[0]
    return pl.pallas_call(
        functools.partial(_mlp_body, alpha=alpha),
        grid=(m // tm,),
        in_specs=[pl.BlockSpec((tm, D_MODEL), lambda i: (i, 0)),
                  _const_spec(w1.shape), _const_spec(w2.shape), _const_spec(lg.shape), _const_spec(lb.shape)],
        out_specs=pl.BlockSpec((tm, D_MODEL), lambda i: (i, 0)),
        out_shape=jax.ShapeDtypeStruct((m, D_MODEL), F32),
        compiler_params=_params("parallel"),
        name="sq_relu_mlp",
    )(x, w1, w2, lg, lb)


def _pad_cols(w, width):
    return jnp.pad(w, ((0, 0), (0, width - w.shape[1])))


def _nsa_weights(w_in):
    def kv(kind, g):
        o = O_NSA_KV + (kind * NSA_GROUPS + g) * NSA_HD
        return w_in[:, o:o + NSA_HD]
    cols = [w_in[:, O_NSA_Q:O_NSA_Q + NSA_HEADS * NSA_HD]]
    for g in range(NSA_GROUPS):
        cols += [kv(2, g), kv(4, g), kv(3, g), kv(5, g)]
    cols += [kv(0, 0), kv(0, 1), kv(1, 0), kv(1, 1)]
    main = jnp.concatenate(cols, axis=1).astype(BF16)
    gw = NSA_HPG * 3
    gates = [_pad_cols(w_in[:, O_NSA_G + g * gw:O_NSA_G + (g + 1) * gw], LANE) for g in range(NSA_GROUPS)]
    return main, jnp.concatenate(gates, axis=1).astype(BF16)


def _gla_weights(w_in, w_lr, b_lr):
    def pad_heads(w):
        lead = w.shape[0]
        w = w.reshape(lead, GLA_HEADS, GLA_DK)
        return jnp.pad(w, ((0, 0), (0, 0), (0, GLA_DV - GLA_DK))).reshape(lead, GLA_HEADS * GLA_DV)
    cols = [pad_heads(w_in[:, O_GLA_Q:O_GLA_Q + GLA_HEADS * GLA_DK]),
            pad_heads(w_in[:, O_GLA_K:O_GLA_K + GLA_HEADS * GLA_DK]),
            w_in[:, O_GLA_V:O_GLA_V + GLA_HEADS * GLA_DV],
            _pad_cols(w_in[:, O_GLA_LR:O_GLA_LR + GLA_RANK], LANE)]
    wlr = jnp.pad(pad_heads(w_lr), ((0, LANE - GLA_RANK), (0, 0)))
    return jnp.concatenate(cols, axis=1).astype(BF16), wlr, pad_heads(b_lr[None, :])


def _overlap_matrix(seq):
    nc = seq // CMP_STRIDE
    ns = seq // SEL_BLK
    c_start = np.arange(nc)[:, None] * CMP_STRIDE
    s_start = np.arange(ns)[None, :] * SEL_BLK
    return jnp.asarray((c_start < s_start + SEL_BLK) & (c_start + CMP_LEN > s_start), BF16)


def _nsa_branch(x, p, l, batch, seq):
    w_in = p["w_in"][l]
    w_main, w_gates = _nsa_weights(w_in)
    cmp_w = 2 * NSA_GROUPS * NSA_HD
    qkv, cmp_src, gate_logits = _mm(x, jnp.concatenate([w_main, w_gates], axis=1),
                                    [(2 * MIX_W, BF16), (cmp_w, BF16), (NSA_GROUPS * LANE, F32)])
    nc = seq // CMP_STRIDE
    half = CMP_STRIDE * NSA_HD
    cmp_in = cmp_src.reshape(batch * seq, 2 * NSA_GROUPS, NSA_HD).transpose(1, 0, 2)
    cmp_in = cmp_in.reshape(2 * NSA_GROUPS, batch * nc, half)
    w1 = p["cmp_w1"][l].reshape(2, 2, half, CMP_HID).astype(BF16)
    pe = jnp.broadcast_to(p["cmp_pe"][l].reshape(2, 2, 1, half), (2, 2, SUBLANE, half)).astype(BF16)
    cmp_tok = _compress(cmp_in, w1, p["cmp_b1"][l][:, None, :], pe, p["cmp_w2"][l].astype(BF16), batch)
    cmp_tok = cmp_tok.reshape(2 * NSA_GROUPS, batch, nc, NSA_HD)
    return _nsa_attention(qkv, gate_logits, cmp_tok, _overlap_matrix(seq), batch, seq)


def _gdn_branch(x, p, l, batch, seq):
    w_in = p["w_in"][l]
    w_gdn = jnp.concatenate([w_in[:, O_GDN_QKV:O_GDN_QKV + 3 * MIX_W],
                             _pad_cols(w_in[:, O_GDN_A:O_GDN_A + 2 * GDN_HEADS], LANE)], axis=1).astype(BF16)
    alog = _pad_cols(p["gdn_a_log"][l][None, :], LANE)
    dtb = _pad_cols(p["gdn_dt_bias"][l][None, :], LANE)
    return _gdn(x, w_gdn, p["gdn_conv"][l], alog, dtb, batch, seq)


def _gla_branch(x, p, l, batch, seq):
    w_gla, wlr, blr = _gla_weights(p["w_in"][l], p["gla_w_lr"][l], p["gla_b_lr"][l])
    ops, lmask = _gla_constants()
    return _gla(x, w_gla, wlr, blr, jnp.asarray(ops, BF16), jnp.asarray(lmask), batch, seq)


def _layer(x, kvm_src, p, l, batch, seq, alpha):
    w_in = p["w_in"][l]
    y_nsa = _nsa_branch(x, p, l, batch, seq)
    o_gdn = _gdn_branch(x, p, l, batch, seq)
    o_gla = _gla_branch(x, p, l, batch, seq)

    wzr = jnp.concatenate([w_in[:, O_GDN_Z:O_GDN_Z + MIX_W], w_in[:, O_GLA_R:O_GLA_R + MIX_W]], axis=1).astype(BF16)
    wmg = w_in[:, O_MERGE:O_MERGE + N_BRANCH * D_MODEL].astype(BF16)
    x = _merge(x, y_nsa, o_gdn, o_gla, wzr, wmg, p["gdn_norm"][l][None, :], p["gla_norm"][l][None, :],
               p["w_up"][l].astype(BF16), p["w_out"][l].astype(BF16),
               p["ln_g"][l, 0][None, :], p["ln_b"][l, 0][None, :], alpha)

    wkv = jnp.concatenate([p["xa_wk"][l], p["xa_wv"][l]], axis=1).astype(BF16)
    kvm, = _mm(kvm_src, wkv, [(2 * D_MODEL, BF16)])
    x = _xattn(x, kvm, p["xa_wq"][l].astype(BF16), p["xa_wo"][l].astype(BF16),
               p["ln_g"][l, 1][None, :], p["ln_b"][l, 1][None, :], alpha, batch, seq)

    return _mlp(x, p["mlp_w1"][l].astype(BF16), p["mlp_w2"][l].astype(BF16),
                p["ln_g"][l, 2][None, :], p["ln_b"][l, 2][None, :], alpha)


def kernel(x, mem, w_in, cmp_pe, cmp_w1, cmp_b1, cmp_w2, gdn_conv, gdn_a_log, gdn_dt_bias, gdn_norm, gla_w_lr,
           gla_b_lr, gla_norm, w_up, w_out, xa_wq, xa_wk, xa_wv, xa_wo, mlp_w1, mlp_w2, ln_g, ln_b):
    batch, seq, d = x.shape
    depth = w_in.shape[0]
    alpha = (2.0 * depth) ** 0.25
    p = dict(w_in=w_in, cmp_pe=cmp_pe, cmp_w1=cmp_w1, cmp_b1=cmp_b1, cmp_w2=cmp_w2, gdn_conv=gdn_conv,
             gdn_a_log=gdn_a_log, gdn_dt_bias=gdn_dt_bias, gdn_norm=gdn_norm, gla_w_lr=gla_w_lr,
             gla_b_lr=gla_b_lr, gla_norm=gla_norm, w_up=w_up, w_out=w_out, xa_wq=xa_wq, xa_wk=xa_wk,
             xa_wv=xa_wv, xa_wo=xa_wo, mlp_w1=mlp_w1, mlp_w2=mlp_w2, ln_g=ln_g, ln_b=ln_b)
    h = x.reshape(batch * seq, d)
    mem2 = mem.reshape(batch * mem.shape[1], d)
    for l in range(depth):
        h = _layer(h, mem2, p, l, batch, seq, alpha)
    return h.reshape(batch, seq, d)
```
